```python
import jax
import jax.numpy as jnp
from jax import lax
import numpy as np

D_MODEL = 1024
BATCH = 32
SEQ = 256
DEPTH = 1
DEC_BATCH = 2
DEC_SEQ = 2048
PAST_LEN = 512

GRID_W = 64
N_HEADS = 8
HEAD_DIM = 64
D_NA = N_HEADS * HEAD_DIM
D_POOL = 512
POOL_WINDOWS = (2, 4, 8, 16)
N_POOL_GROUPS = len(POOL_WINDOWS)
POOL_GROUP = D_POOL // N_POOL_GROUPS
WIN_H = 8
WIN_W = 16
QB_W = 16
KB_W = 2 * QB_W
NQB = GRID_W // QB_W
D_FF = 2816
CONV_W = 3
Q_BLOCK = 128
EPS = 1e-6
NEG_INF = -1e30
D_IN = D_POOL + 3 * D_NA + 2 * D_MODEL
IN_SPLITS = (D_POOL, D_POOL + D_NA, D_POOL + 2 * D_NA, D_POOL + 3 * D_NA, D_POOL + 3 * D_NA + D_MODEL)

kernel_name = "hybrid_pool_natten_dit_step"


def _rms(x, g):
    xf = x.astype(jnp.float32)
    y = xf * lax.rsqrt(jnp.mean(xf * xf, axis=-1, keepdims=True) + EPS)
    return (y * g.astype(jnp.float32)).astype(x.dtype)


def _modulation(cond, w_mod, b_mod):
    m = (jax.nn.silu(cond) @ w_mod + b_mod)[:, None, :]
    return jnp.split(m, 6, axis=-1)


def _modulated_norm(x, g, shift, scale):
    return _rms(x, g) * (1 + scale) + shift


def _mixer_inputs(h, w_in, q_g, k_g):
    B, L, _ = h.shape
    z = h @ w_in
    p, q, k, v, gp, gn = jnp.split(z, IN_SPLITS, axis=-1)

    def heads(t):
        return t.reshape(B, L, N_HEADS, HEAD_DIM).transpose(0, 2, 1, 3)

    q = _rms(heads(q), q_g)
    k = _rms(heads(k), k_g)
    return p, q, k, heads(v), gp, gn


def _pool_mixer(p, w_pool, s_pool):
    B, L, _ = p.shape
    pf = p.astype(jnp.float32)
    cs = jnp.concatenate([jnp.zeros((B, 1, D_POOL), jnp.float32), jnp.cumsum(pf, axis=1)], axis=1)
    t = np.arange(L)
    groups = []
    for gi, w in enumerate(POOL_WINDOWS):
        lo = np.clip(t - w // 2, 0, L)
        hi = np.clip(t + w - w // 2, 0, L)
        cnt = jnp.asarray((hi - lo).astype(np.float32))[None, :, None]
        sl = slice(gi * POOL_GROUP, (gi + 1) * POOL_GROUP)
        seg = cs[:, :, sl]
        groups.append((seg[:, hi] - seg[:, lo]) / cnt - pf[:, :, sl])
    pooled = jnp.stack(groups, axis=2).astype(p.dtype)
    mixed = jnp.einsum("blgc,gcd->blgd", pooled, w_pool).reshape(B, L, D_POOL)
    return mixed * s_pool


def _context_attention(q, k, v):
    B, H, Lc, dh = q.shape
    nb = Lc // Q_BLOCK
    scale = HEAD_DIM ** -0.5
    qb = q.reshape(B, H, nb, Q_BLOCK, dh).transpose(2, 0, 1, 3, 4)

    def blk(qi):
        s = jnp.einsum("bhqd,bhkd->bhqk", qi, k).astype(jnp.float32) * scale
        pr = jax.nn.softmax(s, axis=-1).astype(v.dtype)
        return jnp.einsum("bhqk,bhkd->bhqd", pr, v)

    o = lax.map(blk, qb)
    return o.transpose(1, 3, 0, 2, 4).reshape(B, Lc, H * dh)


def _na_geometry(rows):
    kh = min(WIN_H, rows)
    r = np.arange(rows)
    row_start = np.clip(r - kh // 2, 0, rows - kh)
    row_idx = row_start[:, None] + np.arange(kh)[None, :]
    c0 = np.arange(NQB) * QB_W
    band_start = np.clip(c0 - WIN_W // 2, 0, GRID_W - KB_W)
    col_idx = band_start[:, None] + np.arange(KB_W)[None, :]
    qc = c0[:, None] + np.arange(QB_W)[None, :]
    win_start = np.clip(qc - WIN_W // 2, 0, GRID_W - WIN_W)
    kc = col_idx[:, None, :]
    valid = (kc >= win_start[:, :, None]) & (kc < win_start[:, :, None] + WIN_W)
    dr_idx = row_idx - r[:, None] + (WIN_H - 1)
    dc_idx = np.clip(kc - qc[:, :, None] + (WIN_W - 1), 0, 2 * WIN_W - 2)
    return kh, row_idx, col_idx, valid, dr_idx, dc_idx


def _latent_attention(q, k, v, k_ctx, v_ctx, rpb):
    B, H, L, dh = q.shape
    rows = L // GRID_W
    kh, row_idx, col_idx, valid, dr_idx, dc_idx = _na_geometry(rows)
    nk = kh * KB_W
    scale = HEAD_DIM ** -0.5
    qb = q.reshape(B, H, rows, NQB, QB_W, dh)

    def band(t):
        t = t.reshape(B, H, rows, GRID_W, dh)
        t = jnp.take(t, row_idx, axis=2)
        t = jnp.take(t, col_idx, axis=4)
        return t.transpose(0, 1, 2, 4, 3, 5, 6).reshape(B, H, rows, NQB, nk, dh)

    kb, vb = band(k), band(v)
    bias = rpb[:, dr_idx[:, None, None, :, None], dc_idx[None, :, :, None, :]]
    bias = bias.reshape(H, rows, NQB, QB_W, nk).astype(jnp.float32)
    mask = np.broadcast_to(valid[:, :, None, :], (NQB, QB_W, kh, KB_W)).reshape(NQB, QB_W, nk)
    s_loc = jnp.einsum("bhrnqd,bhrnkd->bhrnqk", qb, kb).astype(jnp.float32) * scale + bias
    s_loc = jnp.where(mask, s_loc, NEG_INF)
    s_ctx = jnp.einsum("bhrnqd,bhkd->bhrnqk", qb, k_ctx).astype(jnp.float32) * scale
    prob = jax.nn.softmax(jnp.concatenate([s_loc, s_ctx], axis=-1), axis=-1).astype(v.dtype)
    o = (jnp.einsum("bhrnqk,bhrnkd->bhrnqd", prob[..., :nk], vb)
         + jnp.einsum("bhrnqk,bhkd->bhrnqd", prob[..., nk:], v_ctx))
    return o.transpose(0, 2, 3, 4, 1, 5).reshape(B, L, H * dh)


def _merge(pool_out, na_out, gp, gn, w_pool_proj, w_na_proj, w_o):
    a = pool_out @ w_pool_proj
    b = na_out @ w_na_proj
    return (jax.nn.sigmoid(gp) * a + jax.nn.sigmoid(gn) * b) @ w_o


def _conv_ffn(h, w_up, conv_w, conv_b, w_down):
    u = h @ w_up
    up = jnp.pad(u, ((0, 0), (1, 1), (0, 0)))
    u = up[:, :-2] * conv_w[0] + up[:, 1:-1] * conv_w[1] + up[:, 2:] * conv_w[2] + conv_b
    a, g = jnp.split(u, 2, axis=-1)
    return (jax.nn.silu(a) * g) @ w_down


def _layer(x, cond, attend, norm_mix_g, norm_ffn_g, w_mod, b_mod, w_in, q_norm_g, k_norm_g,
           pool_w, pool_scale, w_pool_proj, w_na_proj, w_o, w_up, ffn_conv_w, ffn_conv_b, w_down):
    sa, ca, ga, sf, cf, gf = _modulation(cond, w_mod, b_mod)
    h = _modulated_norm(x, norm_mix_g, sa, ca)
    p, q, k, v, gp, gn = _mixer_inputs(h, w_in, q_norm_g, k_norm_g)
    na = attend(q, k, v)
    pool = _pool_mixer(p, pool_w, pool_scale)
    x = x + ga * _merge(pool, na, gp, gn, w_pool_proj, w_na_proj, w_o)
    h = _modulated_norm(x, norm_ffn_g, sf, cf)
    x = x + gf * _conv_ffn(h, w_up, ffn_conv_w, ffn_conv_b, w_down)
    return x, k, v


def setup_inputs(seed: int = 0) -> dict:
    key = jax.random.key(seed)
    ks = jax.random.split(key, 24)

    def nrm(k, shape, scale):
        return jax.random.normal(k, shape, jnp.float32) * scale

    return {
        "x_prompt": nrm(ks[0], (BATCH, SEQ, D_MODEL), 1.0),
        "x_sample": nrm(ks[1], (DEC_BATCH, DEC_SEQ, D_MODEL), 1.0),
        "cache_k": nrm(ks[2], (DEC_BATCH, DEPTH, N_HEADS, PAST_LEN, HEAD_DIM), 1.0),
        "cache_v": nrm(ks[3], (DEC_BATCH, DEPTH, N_HEADS, PAST_LEN, HEAD_DIM), 1.0),
        "c": nrm(ks[4], (DEC_BATCH, D_MODEL), 1.0),
        "c_ctx": nrm(ks[5], (D_MODEL,), 1.0),
        "norm_mix_g": 1.0 + nrm(ks[6], (DEPTH, D_MODEL), 0.1),
        "norm_ffn_g": 1.0 + nrm(ks[7], (DEPTH, D_MODEL), 0.1),
        "w_mod": nrm(ks[8], (DEPTH, D_MODEL, 6 * D_MODEL), 0.5 * D_MODEL ** -0.5),
        "b_mod": nrm(ks[9], (DEPTH, 6 * D_MODEL), 0.02),
        "w_in": nrm(ks[10], (DEPTH, D_MODEL, D_IN), D_MODEL ** -0.5),
        "q_norm_g": 1.0 + nrm(ks[11], (DEPTH, HEAD_DIM), 0.1),
        "k_norm_g": 1.0 + nrm(ks[12], (DEPTH, HEAD_DIM), 0.1),
        "pool_w": nrm(ks[13], (DEPTH, N_POOL_GROUPS, POOL_GROUP, POOL_GROUP), POOL_GROUP ** -0.5),
        "pool_scale": 1.0 + nrm(ks[14], (DEPTH, D_POOL), 0.1),
        "na_rpb": nrm(ks[15], (DEPTH, N_HEADS, 2 * WIN_H - 1, 2 * WIN_W - 1), 0.1),
        "w_pool_proj": nrm(ks[16], (DEPTH, D_POOL, D_MODEL), D_POOL ** -0.5),
        "w_na_proj": nrm(ks[17], (DEPTH, D_NA, D_MODEL), D_NA ** -0.5),
        "w_o": nrm(ks[18], (DEPTH, D_MODEL, D_MODEL), D_MODEL ** -0.5),
        "w_up": nrm(ks[19], (DEPTH, D_MODEL, 2 * D_FF), D_MODEL ** -0.5),
        "ffn_conv_w": nrm(ks[20], (DEPTH, CONV_W, 2 * D_FF), CONV_W ** -0.5),
        "ffn_conv_b": nrm(ks[21], (DEPTH, 2 * D_FF), 0.02),
        "w_down": nrm(ks[22], (DEPTH, D_FF, D_MODEL), D_FF ** -0.5),
    }


def reference(x_prompt, x_sample, cache_k, cache_v, c, c_ctx, norm_mix_g, norm_ffn_g, w_mod, b_mod,
              w_in, q_norm_g, k_norm_g, pool_w, pool_scale, na_rpb, w_pool_proj, w_na_proj, w_o,
              w_up, ffn_conv_w, ffn_conv_b, w_down):
    x = x_prompt
    ks, vs = [], []
    for l in range(DEPTH):
        x, k, v = _layer(x, c_ctx[None, :], _context_attention,
                         norm_mix_g[l], norm_ffn_g[l], w_mod[l], b_mod[l], w_in[l], q_norm_g[l],
                         k_norm_g[l], pool_w[l], pool_scale[l], w_pool_proj[l], w_na_proj[l], w_o[l],
                         w_up[l], ffn_conv_w[l], ffn_conv_b[l], w_down[l])
        ks.append(k)
        vs.append(v)
    y_prompt = x
    new_k = jnp.stack(ks, axis=1)
    new_v = jnp.stack(vs, axis=1)

    x = x_sample
    for l in range(DEPTH):
        k_ctx, v_ctx, rpb = cache_k[:, l], cache_v[:, l], na_rpb[l]

        def attend(q, k, v, k_ctx=k_ctx, v_ctx=v_ctx, rpb=rpb):
            return _latent_attention(q, k, v, k_ctx, v_ctx, rpb)

        x, _, _ = _layer(x, c, attend,
                         norm_mix_g[l], norm_ffn_g[l], w_mod[l], b_mod[l], w_in[l], q_norm_g[l],
                         k_norm_g[l], pool_w[l], pool_scale[l], w_pool_proj[l], w_na_proj[l], w_o[l],
                         w_up[l], ffn_conv_w[l], ffn_conv_b[l], w_down[l])
    y_sample = x
    return (y_prompt, y_sample, new_k, new_v)
```

```python
import functools

import jax
import jax.numpy as jnp
import numpy as np
from jax import lax
from jax.experimental import pallas as pl
from jax.experimental.pallas import tpu as pltpu

D_MODEL = 1024
GRID_W = 64
N_HEADS = 8
HEAD_DIM = 64
D_NA = N_HEADS * HEAD_DIM
D_POOL = 512
POOL_WINDOWS = (2, 4, 8, 16)
POOL_GROUP = D_POOL // len(POOL_WINDOWS)
WIN_H = 8
WIN_W = 16
D_FF = 2816
Q_BLOCK = 128
EPS = 1e-6
NEG_INF = -1e30
OFF_Q = D_POOL
OFF_K = D_POOL + D_NA
OFF_V = D_POOL + 2 * D_NA
OFF_GP = D_POOL + 3 * D_NA
OFF_GN = OFF_GP + D_MODEL
D_IN = OFF_GN + D_MODEL

TILE = 256
TILE_ROWS = TILE // GRID_W
KEY_ROWS = 12
N_LOCAL = KEY_ROWS * GRID_W
POOL_HALO = 128
CONV_HALO = 8
FF_CHUNKS = (768, 768, 768, 512)
MOD_ROWS = 8
MOD_TILE = 1536
VMEM_LIMIT = 56 * 1024 * 1024

BF16 = jnp.bfloat16
F32 = jnp.float32


def _dot(a, b):
    return jnp.dot(a, b, preferred_element_type=F32)


def _dot_nt(a, b):
    return lax.dot_general(a, b, (((1,), (1,)), ((), ())), preferred_element_type=F32)


def _modulated_norm(x, g, shift, scale):
    ms = jnp.mean(x * x, axis=-1, keepdims=True)
    return (x * lax.rsqrt(ms + EPS) * g) * (1.0 + scale) + shift


def _sigmoid(x):
    return 1.0 / (1.0 + jnp.exp(-x))


def _split_bf16(x):
    hi = x.astype(BF16)
    lo = (x - hi.astype(F32)).astype(BF16)
    return hi, lo


def _mod_kernel(cond_ref, w_ref, b_ref, o_ref):
    cnd = cond_ref[...]
    s = (cnd * _sigmoid(cnd)).astype(BF16)
    o_ref[...] = _dot(s, w_ref[...].astype(BF16)) + b_ref[...]


def _modulation(cond, w_mod, b_mod):
    n = w_mod.shape[1]
    return pl.pallas_call(
        _mod_kernel,
        grid=(n // MOD_TILE,),
        in_specs=[
            pl.BlockSpec((MOD_ROWS, D_MODEL), lambda j: (0, 0)),
            pl.BlockSpec((D_MODEL, MOD_TILE), lambda j: (0, j)),
            pl.BlockSpec((1, MOD_TILE), lambda j: (0, j)),
        ],
        out_specs=pl.BlockSpec((MOD_ROWS, MOD_TILE), lambda j: (0, j)),
        out_shape=jax.ShapeDtypeStruct((MOD_ROWS, n), F32),
        compiler_params=pltpu.CompilerParams(
            dimension_semantics=("arbitrary",), vmem_limit_bytes=VMEM_LIMIT),
        name="modulation",
    )(cond, w_mod, b_mod.reshape(1, n))


def _mixer_in_kernel(x_ref, mod_ref, g_ref, w_ref, qg_ref, kg_ref, hsum_ref, *out_refs, emit_kv):
    p_ref, q_ref, k_ref, v_ref, gp_ref, gn_ref = out_refs[:6]
    mod = mod_ref[0]
    h = _modulated_norm(x_ref[...], g_ref[...], mod[:, 0:D_MODEL], mod[:, D_MODEL:2 * D_MODEL])
    h = h.astype(BF16)

    def proj(lo, hi):
        return _dot(h, w_ref[:, lo:hi])

    def head_norm(t, g):
        hi, lo = _split_bf16(t * t)
        ss = _dot(hi, hsum_ref[...]) + _dot(lo, hsum_ref[...])
        return t * lax.rsqrt(ss * (1.0 / HEAD_DIM) + EPS) * g

    p_ref[...] = proj(0, OFF_Q)
    q = head_norm(proj(OFF_Q, OFF_K), qg_ref[...])
    q = (q * (HEAD_DIM ** -0.5)).astype(BF16)
    if emit_kv:
        t = lax.broadcasted_iota(jnp.int32, (TILE, TILE), 0)
        j = lax.broadcasted_iota(jnp.int32, (TILE, TILE), 1)
        n_blocks = TILE // Q_BLOCK
        perm = jnp.where(j == (t % n_blocks) * Q_BLOCK + t // n_blocks, 1.0, 0.0).astype(BF16)
        q = _dot(perm, q).astype(BF16)
    q_ref[...] = q
    k = head_norm(proj(OFF_K, OFF_V), kg_ref[...])
    k_ref[...] = k.astype(BF16)
    v = proj(OFF_V, OFF_GP)
    v_ref[...] = v.astype(BF16)
    gp_ref[...] = proj(OFF_GP, OFF_GN)
    gn_ref[...] = proj(OFF_GN, D_IN)
    if emit_kv:
        nk_ref, nv_ref = out_refs[6:]
        for hd in range(N_HEADS):
            sl = slice(hd * HEAD_DIM, (hd + 1) * HEAD_DIM)
            nk_ref[0, 0, hd] = k[:, sl]
            nv_ref[0, 0, hd] = v[:, sl]


def _mixer_in(x, mod, mod_index, norm_g, w_in, q_g, k_g, hsum, *, emit_kv):
    n_tok = x.shape[0]
    n_tiles = n_tok // TILE
    tok = lambda i: (i, 0)
    const = lambda i: (0, 0)
    out_shape = [
        jax.ShapeDtypeStruct((n_tok, D_POOL), F32),
        jax.ShapeDtypeStruct((n_tok, D_NA), BF16),
        jax.ShapeDtypeStruct((n_tok, D_NA), BF16),
        jax.ShapeDtypeStruct((n_tok, D_NA), BF16),
        jax.ShapeDtypeStruct((n_tok, D_MODEL), F32),
        jax.ShapeDtypeStruct((n_tok, D_MODEL), F32),
    ]
    out_specs = [
        pl.BlockSpec((TILE, D_POOL), tok),
        pl.BlockSpec((TILE, D_NA), tok),
        pl.BlockSpec((TILE, D_NA), tok),
        pl.BlockSpec((TILE, D_NA), tok),
        pl.BlockSpec((TILE, D_MODEL), tok),
        pl.BlockSpec((TILE, D_MODEL), tok),
    ]
    if emit_kv:
        kv_shape = jax.ShapeDtypeStruct((n_tiles, 1, N_HEADS, TILE, HEAD_DIM), F32)
        kv_spec = pl.BlockSpec((1, 1, N_HEADS, TILE, HEAD_DIM), lambda i: (i, 0, 0, 0, 0))
        out_shape += [kv_shape, kv_shape]
        out_specs += [kv_spec, kv_spec]
    return pl.pallas_call(
        functools.partial(_mixer_in_kernel, emit_kv=emit_kv),
        grid=(n_tiles,),
        in_specs=[
            pl.BlockSpec((TILE, D_MODEL), tok),
            pl.BlockSpec((1, 1, 6 * D_MODEL), lambda i: (mod_index(i), 0, 0)),
            pl.BlockSpec((1, D_MODEL), const),
            pl.BlockSpec((D_MODEL, D_IN), const),
            pl.BlockSpec((1, D_NA), const),
            pl.BlockSpec((1, D_NA), const),
            pl.BlockSpec((D_NA, D_NA), const),
        ],
        out_specs=out_specs,
        out_shape=out_shape,
        compiler_params=pltpu.CompilerParams(
            dimension_semantics=("arbitrary",), vmem_limit_bytes=VMEM_LIMIT),
        name="mixer_in_ctx" if emit_kv else "mixer_in_lat",
    )(x, mod, norm_g, w_in, q_g, k_g, hsum)


def _pool_band(window, n_prev):
    n_k = TILE + 2 * n_prev
    t = lax.broadcasted_iota(jnp.int32, (TILE, n_k), 0)
    j = lax.broadcasted_iota(jnp.int32, (TILE, n_k), 1) - n_prev
    lo = t - window // 2
    return jnp.where((j >= lo) & (j < lo + window), 1.0, 0.0).astype(BF16)


def _mixer_out_kernel(*refs, latent, tiles_per_seq):
    if latent:
        (x_ref, mod_ref, p_ref, pprev_ref, pnext_ref, q_ref, k_ref, v_ref, gp_ref, gn_ref,
         ck_ref, cv_ref, bias_ref, wpool_ref, spool_ref, wpp_ref, wnp_ref, wo_ref,
         o_ref, na_ref) = refs
    else:
        (x_ref, mod_ref, p_ref, q_ref, k_ref, v_ref, gp_ref, gn_ref,
         wpool_ref, spool_ref, wpp_ref, wnp_ref, wo_ref, o_ref, na_ref) = refs
    i = pl.program_id(0)
    t_in_seq = i % tiles_per_seq
    seq_len = TILE * tiles_per_seq

    if latent:
        key_row0 = jnp.clip(t_in_seq * TILE_ROWS - WIN_H // 2, 0, seq_len // GRID_W - KEY_ROWS)
        key0 = pl.multiple_of(key_row0 * GRID_W, TILE)
    for hd in range(N_HEADS):
        sl = slice(hd * HEAD_DIM, (hd + 1) * HEAD_DIM)
        qh = q_ref[:, sl]
        if latent:
            s_loc = _dot_nt(qh, k_ref[0, pl.ds(key0, N_LOCAL), sl]) + bias_ref[0, hd]
            s_ctx = _dot_nt(qh, ck_ref[0, hd])
            m = jnp.maximum(jnp.max(s_loc, axis=-1, keepdims=True),
                            jnp.max(s_ctx, axis=-1, keepdims=True))
            e_loc = jnp.exp(s_loc - m)
            e_ctx = jnp.exp(s_ctx - m)
            denom = (jnp.sum(e_loc, axis=-1, keepdims=True)
                     + jnp.sum(e_ctx, axis=-1, keepdims=True))
            o = (_dot(e_loc.astype(BF16), v_ref[0, pl.ds(key0, N_LOCAL), sl])
                 + _dot(e_ctx.astype(BF16), cv_ref[0, hd]))
        else:
            s = _dot_nt(qh, k_ref[0, :, sl])
            m = jnp.max(s, axis=-1, keepdims=True)
            e = jnp.exp(s - m)
            denom = jnp.sum(e, axis=-1, keepdims=True)
            o = _dot(e.astype(BF16), v_ref[0, :, sl])
        na_ref[:, sl] = o / denom

    p = p_ref[...]
    pos = t_in_seq * TILE + lax.broadcasted_iota(jnp.int32, (TILE, 1), 0)
    if latent:
        prev_ok = (t_in_seq > 0).astype(F32)
        next_ok = (t_in_seq < tiles_per_seq - 1).astype(F32)
        p_ext = jnp.concatenate([pprev_ref[...] * prev_ok, p, pnext_ref[...] * next_ok], axis=0)
        n_prev = POOL_HALO
    else:
        p_ext = p
        n_prev = 0
    p_hi, p_lo = _split_bf16(p_ext)
    mixed = []
    for gi, w in enumerate(POOL_WINDOWS):
        gs = slice(gi * POOL_GROUP, (gi + 1) * POOL_GROUP)
        band = _pool_band(w, n_prev)
        sums = _dot(band, p_hi[:, gs]) + _dot(band, p_lo[:, gs])
        lo = jnp.maximum(pos - w // 2, 0)
        hi = jnp.minimum(pos + (w - w // 2), seq_len)
        pooled = sums / (hi - lo).astype(F32) - p[:, gs]
        mixed.append(_dot(pooled.astype(BF16), wpool_ref[gi]))
    pool_out = jnp.concatenate(mixed, axis=-1) * spool_ref[...]

    a = _dot(pool_out.astype(BF16), wpp_ref[...])
    b = _dot(na_ref[...].astype(BF16), wnp_ref[...])
    merged = _sigmoid(gp_ref[...]) * a + _sigmoid(gn_ref[...]) * b
    y = _dot(merged.astype(BF16), wo_ref[...])
    gate = mod_ref[0][:, 2 * D_MODEL:3 * D_MODEL]
    o_ref[...] = x_ref[...] + gate * y


def _mixer_out(x, mod, mod_index, p, q, k, v, gp, gn, w_pool, s_pool, w_pp, w_np, w_o,
               *, seq_len, cache=None):
    n_tok = x.shape[0]
    n_tiles = n_tok // TILE
    tiles_per_seq = seq_len // TILE
    latent = cache is not None
    tok = lambda i: (i, 0)
    const2 = lambda i: (0, 0)
    seq = lambda i: (i // tiles_per_seq, 0, 0)
    k = k.reshape(n_tok // seq_len, seq_len, D_NA)
    v = v.reshape(n_tok // seq_len, seq_len, D_NA)

    in_specs = [
        pl.BlockSpec((TILE, D_MODEL), tok),
        pl.BlockSpec((1, 1, 6 * D_MODEL), lambda i: (mod_index(i), 0, 0)),
        pl.BlockSpec((TILE, D_POOL), tok),
    ]
    args = [x, mod, p]
    if latent:
        hb = TILE // POOL_HALO
        n_hb = n_tok // POOL_HALO
        in_specs += [
            pl.BlockSpec((POOL_HALO, D_POOL), lambda i: (jnp.maximum(i * hb - 1, 0), 0)),
            pl.BlockSpec((POOL_HALO, D_POOL), lambda i: (jnp.minimum((i + 1) * hb, n_hb - 1), 0)),
        ]
        args += [p, p]
    in_specs += [
        pl.BlockSpec((TILE, D_NA), tok),
        pl.BlockSpec((1, seq_len, D_NA), seq),
        pl.BlockSpec((1, seq_len, D_NA), seq),
        pl.BlockSpec((TILE, D_MODEL), tok),
        pl.BlockSpec((TILE, D_MODEL), tok),
    ]
    args += [q, k, v, gp, gn]
    if latent:
        cache_k, cache_v, bias = cache
        past = cache_k.shape[2]
        n_rows = seq_len // GRID_W

        def pattern(i):
            t = i % tiles_per_seq
            return jnp.where(t == 0, 0, jnp.where(t == tiles_per_seq - 1, 2, 1))

        in_specs += [
            pl.BlockSpec((1, N_HEADS, past, HEAD_DIM), lambda i: (i // tiles_per_seq, 0, 0, 0)),
            pl.BlockSpec((1, N_HEADS, past, HEAD_DIM), lambda i: (i // tiles_per_seq, 0, 0, 0)),
            pl.BlockSpec((1, N_HEADS, TILE, N_LOCAL), lambda i: (pattern(i), 0, 0, 0)),
        ]
        args += [cache_k, cache_v, bias]
        assert n_rows >= KEY_ROWS
    in_specs += [
        pl.BlockSpec((len(POOL_WINDOWS), POOL_GROUP, POOL_GROUP), lambda i: (0, 0, 0)),
        pl.BlockSpec((1, D_POOL), const2),
        pl.BlockSpec((D_POOL, D_MODEL), const2),
        pl.BlockSpec((D_NA, D_MODEL), const2),
        pl.BlockSpec((D_MODEL, D_MODEL), const2),
    ]
    args += [w_pool, s_pool, w_pp, w_np, w_o]
    return pl.pallas_call(
        functools.partial(_mixer_out_kernel, latent=latent, tiles_per_seq=tiles_per_seq),
        grid=(n_tiles,),
        in_specs=in_specs,
        out_specs=pl.BlockSpec((TILE, D_MODEL), tok),
        out_shape=jax.ShapeDtypeStruct((n_tok, D_MODEL), F32),
        scratch_shapes=[pltpu.VMEM((TILE, D_NA), F32)],
        compiler_params=pltpu.CompilerParams(
            dimension_semantics=("arbitrary",), vmem_limit_bytes=VMEM_LIMIT),
        name="mixer_out_lat" if latent else "mixer_out_ctx",
    )(*args)


def _conv_ffn_kernel(*refs, halo, tiles_per_seq):
    if halo:
        (x_ref, xprev_ref, xnext_ref, mod_ref, g_ref, wup_ref, cw_ref, cb_ref, wdn_ref,
         o_ref, acc_ref) = refs
    else:
        x_ref, mod_ref, g_ref, wup_ref, cw_ref, cb_ref, wdn_ref, o_ref, acc_ref = refs
    i = pl.program_id(0)
    t_in_seq = i % tiles_per_seq
    mod = mod_ref[0]
    shift = mod[:, 3 * D_MODEL:4 * D_MODEL]
    scale = mod[:, 4 * D_MODEL:5 * D_MODEL]
    gate = mod[:, 5 * D_MODEL:6 * D_MODEL]
    x = x_ref[...]
    if halo:
        x_ext = jnp.concatenate([xprev_ref[...], x, xnext_ref[...]], axis=0)
        pad = CONV_HALO
    else:
        x_ext = x
        pad = 0
    n_ext = TILE + 2 * pad
    h = _modulated_norm(x_ext, g_ref[...], shift, scale).astype(BF16)

    row = lax.broadcasted_iota(jnp.int32, (TILE, 1), 0)
    keep_prev = jnp.where((row == 0) & (t_in_seq == 0), 0.0, 1.0)
    keep_next = jnp.where((row == TILE - 1) & (t_in_seq == tiles_per_seq - 1), 0.0, 1.0)

    def conv(u, lo, hi):
        u_prev = pltpu.roll(u, 1, axis=0)[pad:pad + TILE] * keep_prev
        u_next = pltpu.roll(u, n_ext - 1, axis=0)[pad:pad + TILE] * keep_next
        return (u_prev * cw_ref[0:1, lo:hi] + u[pad:pad + TILE] * cw_ref[1:2, lo:hi]
                + u_next * cw_ref[2:3, lo:hi] + cb_ref[:, lo:hi])

    start = 0
    for ci, width in enumerate(FF_CHUNKS):
        lo, hi = start, start + width
        a = conv(_dot(h, wup_ref[:, lo:hi]), lo, hi)
        g = conv(_dot(h, wup_ref[:, D_FF + lo:D_FF + hi]), D_FF + lo, D_FF + hi)
        act = (a * _sigmoid(a) * g).astype(BF16)
        part = _dot(act, wdn_ref[lo:hi, :])
        if ci == 0:
            acc_ref[...] = part
        else:
            acc_ref[...] += part
        start = hi
    o_ref[...] = x + gate * acc_ref[...]


def _conv_ffn(x, mod, mod_index, norm_g, w_up, conv_w, conv_b, w_down, *, seq_len):
    n_tok = x.shape[0]
    n_tiles = n_tok // TILE
    tiles_per_seq = seq_len // TILE
    halo = tiles_per_seq > 1
    tok = lambda i: (i, 0)
    const = lambda i: (0, 0)
    in_specs = [pl.BlockSpec((TILE, D_MODEL), tok)]
    args = [x]
    if halo:
        hb = TILE // CONV_HALO
        n_hb = n_tok // CONV_HALO
        in_specs += [
            pl.BlockSpec((CONV_HALO, D_MODEL), lambda i: (jnp.maximum(i * hb - 1, 0), 0)),
            pl.BlockSpec((CONV_HALO, D_MODEL), lambda i: (jnp.minimum((i + 1) * hb, n_hb - 1), 0)),
        ]
        args += [x, x]
    in_specs += [
        pl.BlockSpec((1, 1, 6 * D_MODEL), lambda i: (mod_index(i), 0, 0)),
        pl.BlockSpec((1, D_MODEL), const),
        pl.BlockSpec((D_MODEL, 2 * D_FF), const),
        pl.BlockSpec((3, 2 * D_FF), const),
        pl.BlockSpec((1, 2 * D_FF), const),
        pl.BlockSpec((D_FF, D_MODEL), const),
    ]
    args += [mod, norm_g, w_up, conv_w, conv_b, w_down]
    return pl.pallas_call(
        functools.partial(_conv_ffn_kernel, halo=halo, tiles_per_seq=tiles_per_seq),
        grid=(n_tiles,),
        in_specs=in_specs,
        out_specs=pl.BlockSpec((TILE, D_MODEL), tok),
        out_shape=jax.ShapeDtypeStruct((n_tok, D_MODEL), F32),
        scratch_shapes=[pltpu.VMEM((TILE, D_MODEL), F32)],
        compiler_params=pltpu.CompilerParams(
            dimension_semantics=("arbitrary",), vmem_limit_bytes=VMEM_LIMIT),
        name="conv_ffn_lat" if halo else "conv_ffn_ctx",
    )(*args)


def _local_bias_indices(n_rows):
    n_tiles = n_rows // TILE_ROWS
    qi = np.arange(TILE)
    kj = np.arange(N_LOCAL)
    idx, valid = [], []
    for t in (0, 1, n_tiles - 1):
        r = t * TILE_ROWS + qi // GRID_W
        qc = qi % GRID_W
        key_row0 = np.clip(t * TILE_ROWS - WIN_H // 2, 0, n_rows - KEY_ROWS)
        kr = key_row0 + kj // GRID_W
        kc = kj % GRID_W
        row_start = np.clip(r - WIN_H // 2, 0, n_rows - WIN_H)
        win_start = np.clip(qc - WIN_W // 2, 0, GRID_W - WIN_W)
        ok = ((kr[None, :] >= row_start[:, None]) & (kr[None, :] < row_start[:, None] + WIN_H)
              & (kc[None, :] >= win_start[:, None]) & (kc[None, :] < win_start[:, None] + WIN_W))
        dr = np.clip(kr[None, :] - r[:, None] + (WIN_H - 1), 0, 2 * WIN_H - 2)
        dc = np.clip(kc[None, :] - qc[:, None] + (WIN_W - 1), 0, 2 * WIN_W - 2)
        idx.append(dr * (2 * WIN_W - 1) + dc)
        valid.append(ok)
    return np.stack(idx).astype(np.int32), np.stack(valid)


def kernel(x_prompt, x_sample, cache_k, cache_v, c, c_ctx, norm_mix_g, norm_ffn_g, w_mod, b_mod,
           w_in, q_norm_g, k_norm_g, pool_w, pool_scale, na_rpb, w_pool_proj, w_na_proj, w_o,
           w_up, ffn_conv_w, ffn_conv_b, w_down):
    depth = w_in.shape[0]
    assert depth == 1
    batch, seq, _ = x_prompt.shape
    dec_batch, dec_seq, _ = x_sample.shape
    assert seq == TILE and dec_seq % TILE == 0 and dec_batch + 1 <= MOD_ROWS
    l = 0

    cond = jnp.zeros((MOD_ROWS, D_MODEL), F32).at[0].set(c_ctx).at[1:1 + dec_batch].set(c)
    mod = _modulation(cond, w_mod[l], b_mod[l]).reshape(MOD_ROWS, 1, 6 * D_MODEL)

    w_in_b = w_in[l].astype(BF16)
    w_pool_b = pool_w[l].astype(BF16)
    w_pp_b = w_pool_proj[l].astype(BF16)
    w_np_b = w_na_proj[l].astype(BF16)
    w_o_b = w_o[l].astype(BF16)
    w_up_b = w_up[l].astype(BF16)
    w_down_b = w_down[l].astype(BF16)
    g_mix = norm_mix_g[l].reshape(1, D_MODEL)
    g_ffn = norm_ffn_g[l].reshape(1, D_MODEL)
    q_g = jnp.tile(q_norm_g[l], N_HEADS).reshape(1, D_NA)
    k_g = jnp.tile(k_norm_g[l], N_HEADS).reshape(1, D_NA)
    s_pool = pool_scale[l].reshape(1, D_POOL)
    conv_b = ffn_conv_b[l].reshape(1, 2 * D_FF)
    head_of = np.arange(D_NA) // HEAD_DIM
    hsum = jnp.asarray(head_of[:, None] == head_of[None, :], BF16)

    ctx_mod = lambda i: 0
    xc = x_prompt.reshape(batch * seq, D_MODEL)
    p, q, k, v, gp, gn, new_k, new_v = _mixer_in(
        xc, mod, ctx_mod, g_mix, w_in_b, q_g, k_g, hsum, emit_kv=True)
    xc = _mixer_out(xc, mod, ctx_mod, p, q, k, v, gp, gn, w_pool_b, s_pool, w_pp_b, w_np_b, w_o_b,
                    seq_len=seq)
    xc = _conv_ffn(xc, mod, ctx_mod, g_ffn, w_up_b, ffn_conv_w[l], conv_b, w_down_b, seq_len=seq)
    y_prompt = xc.reshape(batch, seq, D_MODEL)

    tiles_per_seq = dec_seq // TILE
    lat_mod = lambda i: 1 + i // tiles_per_seq
    idx, valid = _local_bias_indices(dec_seq // GRID_W)
    rpb = na_rpb[l].reshape(N_HEADS, -1)
    bias = jnp.where(valid[:, None], jnp.take(rpb, idx, axis=1).transpose(1, 0, 2, 3), NEG_INF)
    cache = (cache_k[:, l].astype(BF16), cache_v[:, l].astype(BF16), bias)
    xs = x_sample.reshape(dec_batch * dec_seq, D_MODEL)
    p, q, k, v, gp, gn = _mixer_in(xs, mod, lat_mod, g_mix, w_in_b, q_g, k_g, hsum, emit_kv=False)
    xs = _mixer_out(xs, mod, lat_mod, p, q, k, v, gp, gn, w_pool_b, s_pool, w_pp_b, w_np_b, w_o_b,
                    seq_len=dec_seq, cache=cache)
    xs = _conv_ffn(xs, mod, lat_mod, g_ffn, w_up_b, ffn_conv_w[l], conv_b, w_down_b,
                   seq_len=dec_seq)
    y_sample = xs.reshape(dec_batch, dec_seq, D_MODEL)
    return (y_prompt, y_sample, new_k, new_v)
```

```python
import functools

import jax
import jax.numpy as jnp
import numpy as np
from jax import lax
from jax.experimental import pallas as pl
from jax.experimental.pallas import tpu as pltpu

D_MODEL = 1024
GRID_W = 64
N_HEADS = 8
HEAD_DIM = 64
D_NA = N_HEADS * HEAD_DIM
D_POOL = 512
POOL_WINDOWS = (2, 4, 8, 16)
POOL_GROUP = D_POOL // len(POOL_WINDOWS)
WIN_H = 8
WIN_W = 16
D_FF = 2816
Q_BLOCK = 128
EPS = 1e-6
NEG_INF = -1e30
OFF_Q = D_POOL
OFF_K = D_POOL + D_NA
OFF_V = D_POOL + 2 * D_NA
OFF_GP = D_POOL + 3 * D_NA
OFF_GN = OFF_GP + D_MODEL
D_IN = OFF_GN + D_MODEL

TILE = 256
TILE_ROWS = TILE // GRID_W
KEY_ROWS = 12
N_LOCAL = KEY_ROWS * GRID_W
POOL_HALO = 128
CONV_HALO = 8
FF_CHUNKS = (768, 768, 768, 512)
MOD_ROWS = 8
MOD_TILE = 1536
VMEM_LIMIT = 56 * 1024 * 1024

BF16 = jnp.bfloat16
F32 = jnp.float32


def _dot(a, b):
    return jnp.dot(a, b, preferred_element_type=F32)


def _dot_nt(a, b):
    return lax.dot_general(a, b, (((1,), (1,)), ((), ())), preferred_element_type=F32)


def _modulated_norm(x, g, shift, scale):
    ms = jnp.mean(x * x, axis=-1, keepdims=True)
    return (x * lax.rsqrt(ms + EPS) * g) * (1.0 + scale) + shift


def _sigmoid(x):
    return 1.0 / (1.0 + jnp.exp(-x))


def _split_bf16(x):
    hi = x.astype(BF16)
    lo = (x - hi.astype(F32)).astype(BF16)
    return hi, lo


def _mod_kernel(cond_ref, w_ref, b_ref, o_ref):
    cnd = cond_ref[...]
    s = (cnd * _sigmoid(cnd)).astype(BF16)
    o_ref[...] = _dot(s, w_ref[...].astype(BF16)) + b_ref[...]


def _modulation(cond, w_mod, b_mod):
    n = w_mod.shape[1]
    return pl.pallas_call(
        _mod_kernel,
        grid=(n // MOD_TILE,),
        in_specs=[
            pl.BlockSpec((MOD_ROWS, D_MODEL), lambda j: (0, 0)),
            pl.BlockSpec((D_MODEL, MOD_TILE), lambda j: (0, j)),
            pl.BlockSpec((1, MOD_TILE), lambda j: (0, j)),
        ],
        out_specs=pl.BlockSpec((MOD_ROWS, MOD_TILE), lambda j: (0, j)),
        out_shape=jax.ShapeDtypeStruct((MOD_ROWS, n), F32),
        compiler_params=pltpu.CompilerParams(
            dimension_semantics=("arbitrary",), vmem_limit_bytes=VMEM_LIMIT),
        name="modulation",
    )(cond, w_mod, b_mod.reshape(1, n))


def _mixer_in_kernel(x_ref, mod_ref, g_ref, w_ref, qg_ref, kg_ref, hsum_ref, *out_refs, emit_kv):
    p_ref, q_ref, k_ref, v_ref, gp_ref, gn_ref = out_refs[:6]
    mod = mod_ref[0]
    h = _modulated_norm(x_ref[...], g_ref[...], mod[:, 0:D_MODEL], mod[:, D_MODEL:2 * D_MODEL])
    h = h.astype(BF16)

    def proj(lo, hi):
        return _dot(h, w_ref[:, lo:hi])

    def head_norm(t, g):
        hi, lo = _split_bf16(t * t)
        ss = _dot(hi, hsum_ref[...]) + _dot(lo, hsum_ref[...])
        return t * lax.rsqrt(ss * (1.0 / HEAD_DIM) + EPS) * g

    p_ref[...] = proj(0, OFF_Q)
    q = head_norm(proj(OFF_Q, OFF_K), qg_ref[...])
    q = (q * (HEAD_DIM ** -0.5)).astype(BF16)
    if emit_kv:
        t = lax.broadcasted_iota(jnp.int32, (TILE, TILE), 0)
        j = lax.broadcasted_iota(jnp.int32, (TILE, TILE), 1)
        n_blocks = TILE // Q_BLOCK
        perm = jnp.where(j == (t % n_blocks) * Q_BLOCK + t // n_blocks, 1.0, 0.0).astype(BF16)
        q = _dot(perm, q).astype(BF16)
    q_ref[...] = q
    k = head_norm(proj(OFF_K, OFF_V), kg_ref[...])
    k_ref[...] = k.astype(BF16)
    v = proj(OFF_V, OFF_GP)
    v_ref[...] = v.astype(BF16)
    gp_ref[...] = proj(OFF_GP, OFF_GN)
    gn_ref[...] = proj(OFF_GN, D_IN)
    if emit_kv:
        nk_ref, nv_ref = out_refs[6:]
        for hd in range(N_HEADS):
            sl = slice(hd * HEAD_DIM, (hd + 1) * HEAD_DIM)
            nk_ref[0, 0, hd] = k[:, sl]
            nv_ref[0, 0, hd] = v[:, sl]


def _mixer_in(x, mod, mod_index, norm_g, w_in, q_g, k_g, hsum, *, emit_kv):
    n_tok = x.shape[0]
    n_tiles = n_tok // TILE
    tok = lambda i: (i, 0)
    const = lambda i: (0, 0)
    out_shape = [
        jax.ShapeDtypeStruct((n_tok, D_POOL), F32),
        jax.ShapeDtypeStruct((n_tok, D_NA), BF16),
        jax.ShapeDtypeStruct((n_tok, D_NA), BF16),
        jax.ShapeDtypeStruct((n_tok, D_NA), BF16),
        jax.ShapeDtypeStruct((n_tok, D_MODEL), F32),
        jax.ShapeDtypeStruct((n_tok, D_MODEL), F32),
    ]
    out_specs = [
        pl.BlockSpec((TILE, D_POOL), tok),
        pl.BlockSpec((TILE, D_NA), tok),
        pl.BlockSpec((TILE, D_NA), tok),
        pl.BlockSpec((TILE, D_NA), tok),
        pl.BlockSpec((TILE, D_MODEL), tok),
        pl.BlockSpec((TILE, D_MODEL), tok),
    ]
    if emit_kv:
        kv_shape = jax.ShapeDtypeStruct((n_tiles, 1, N_HEADS, TILE, HEAD_DIM), F32)
        kv_spec = pl.BlockSpec((1, 1, N_HEADS, TILE, HEAD_DIM), lambda i: (i, 0, 0, 0, 0))
        out_shape += [kv_shape, kv_shape]
        out_specs += [kv_spec, kv_spec]
    return pl.pallas_call(
        functools.partial(_mixer_in_kernel, emit_kv=emit_kv),
        grid=(n_tiles,),
        in_specs=[
            pl.BlockSpec((TILE, D_MODEL), tok),
            pl.BlockSpec((1, 1, 6 * D_MODEL), lambda i: (mod_index(i), 0, 0)),
            pl.BlockSpec((1, D_MODEL), const),
            pl.BlockSpec((D_MODEL, D_IN), const),
            pl.BlockSpec((1, D_NA), const),
            pl.BlockSpec((1, D_NA), const),
            pl.BlockSpec((D_NA, D_NA), const),
        ],
        out_specs=out_specs,
        out_shape=out_shape,
        compiler_params=pltpu.CompilerParams(
            dimension_semantics=("arbitrary",), vmem_limit_bytes=VMEM_LIMIT),
        name="mixer_in_ctx" if emit_kv else "mixer_in_lat",
    )(x, mod, norm_g, w_in, q_g, k_g, hsum)


def _pool_band(window, n_prev):
    n_k = TILE + 2 * n_prev
    t = lax.broadcasted_iota(jnp.int32, (TILE, n_k), 0)
    j = lax.broadcasted_iota(jnp.int32, (TILE, n_k), 1) - n_prev
    lo = t - window // 2
    return jnp.where((j >= lo) & (j < lo + window), 1.0, 0.0).astype(BF16)


def _mixer_out_kernel(*refs, latent, tiles_per_seq):
    if latent:
        (x_ref, mod_ref, p_ref, pprev_ref, pnext_ref, q_ref, k_ref, v_ref, gp_ref, gn_ref,
         ck_ref, cv_ref, bias_ref, wpool_ref, spool_ref, wpp_ref, wnp_ref, wo_ref,
         o_ref, na_ref) = refs
    else:
        (x_ref, mod_ref, p_ref, q_ref, k_ref, v_ref, gp_ref, gn_ref,
         wpool_ref, spool_ref, wpp_ref, wnp_ref, wo_ref, o_ref, na_ref) = refs
    i = pl.program_id(0)
    t_in_seq = i % tiles_per_seq
    seq_len = TILE * tiles_per_seq

    if latent:
        key_row0 = jnp.clip(t_in_seq * TILE_ROWS - WIN_H // 2, 0, seq_len // GRID_W - KEY_ROWS)
        key0 = pl.multiple_of(key_row0 * GRID_W, TILE)
    for hd in range(N_HEADS):
        sl = slice(hd * HEAD_DIM, (hd + 1) * HEAD_DIM)
        qh = q_ref[:, sl]
        if latent:
            s_loc = _dot_nt(qh, k_ref[0, pl.ds(key0, N_LOCAL), sl]) + bias_ref[0, hd]
            s_ctx = _dot_nt(qh, ck_ref[0, hd])
            m = jnp.maximum(jnp.max(s_loc, axis=-1, keepdims=True),
                            jnp.max(s_ctx, axis=-1, keepdims=True))
            e_loc = jnp.exp(s_loc - m)
            e_ctx = jnp.exp(s_ctx - m)
            denom = (jnp.sum(e_loc, axis=-1, keepdims=True)
                     + jnp.sum(e_ctx, axis=-1, keepdims=True))
            o = (_dot(e_loc.astype(BF16), v_ref[0, pl.ds(key0, N_LOCAL), sl])
                 + _dot(e_ctx.astype(BF16), cv_ref[0, hd]))
        else:
            s = _dot_nt(qh, k_ref[0, :, sl])
            m = jnp.max(s, axis=-1, keepdims=True)
            e = jnp.exp(s - m)
            denom = jnp.sum(e, axis=-1, keepdims=True)
            o = _dot(e.astype(BF16), v_ref[0, :, sl])
        na_ref[:, sl] = o / denom

    p = p_ref[...]
    pos = t_in_seq * TILE + lax.broadcasted_iota(jnp.int32, (TILE, 1), 0)
    if latent:
        prev_ok = (t_in_seq > 0).astype(F32)
        next_ok = (t_in_seq < tiles_per_seq - 1).astype(F32)
        p_ext = jnp.concatenate([pprev_ref[...] * prev_ok, p, pnext_ref[...] * next_ok], axis=0)
        n_prev = POOL_HALO
    else:
        p_ext = p
        n_prev = 0
    p_hi, p_lo = _split_bf16(p_ext)
    mixed = []
    for gi, w in enumerate(POOL_WINDOWS):
        gs = slice(gi * POOL_GROUP, (gi + 1) * POOL_GROUP)
        band = _pool_band(w, n_prev)
        sums = _dot(band, p_hi[:, gs]) + _dot(band, p_lo[:, gs])
        lo = jnp.maximum(pos - w // 2, 0)
        hi = jnp.minimum(pos + (w - w // 2), seq_len)
        pooled = sums / (hi - lo).astype(F32) - p[:, gs]
        mixed.append(_dot(pooled.astype(BF16), wpool_ref[gi]))
    pool_out = jnp.concatenate(mixed, axis=-1) * spool_ref[...]

    a = _dot(pool_out.astype(BF16), wpp_ref[...])
    b = _dot(na_ref[...].astype(BF16), wnp_ref[...])
    merged = _sigmoid(gp_ref[...]) * a + _sigmoid(gn_ref[...]) * b
    y = _dot(merged.astype(BF16), wo_ref[...])
    gate = mod_ref[0][:, 2 * D_MODEL:3 * D_MODEL]
    o_ref[...] = x_ref[...] + gate * y


def _mixer_out(x, mod, mod_index, p, q, k, v, gp, gn, w_pool, s_pool, w_pp, w_np, w_o,
               *, seq_len, cache=None):
    n_tok = x.shape[0]
    n_tiles = n_tok // TILE
    tiles_per_seq = seq_len // TILE
    latent = cache is not None
    tok = lambda i: (i, 0)
    const2 = lambda i: (0, 0)
    seq = lambda i: (i // tiles_per_seq, 0, 0)
    k = k.reshape(n_tok // seq_len, seq_len, D_NA)
    v = v.reshape(n_tok // seq_len, seq_len, D_NA)

    in_specs = [
        pl.BlockSpec((TILE, D_MODEL), tok),
        pl.BlockSpec((1, 1, 6 * D_MODEL), lambda i: (mod_index(i), 0, 0)),
        pl.BlockSpec((TILE, D_POOL), tok),
    ]
    args = [x, mod, p]
    if latent:
        hb = TILE // POOL_HALO
        n_hb = n_tok // POOL_HALO
        in_specs += [
            pl.BlockSpec((POOL_HALO, D_POOL), lambda i: (jnp.maximum(i * hb - 1, 0), 0)),
            pl.BlockSpec((POOL_HALO, D_POOL), lambda i: (jnp.minimum((i + 1) * hb, n_hb - 1), 0)),
        ]
        args += [p, p]
    in_specs += [
        pl.BlockSpec((TILE, D_NA), tok),
        pl.BlockSpec((1, seq_len, D_NA), seq),
        pl.BlockSpec((1, seq_len, D_NA), seq),
        pl.BlockSpec((TILE, D_MODEL), tok),
        pl.BlockSpec((TILE, D_MODEL), tok),
    ]
    args += [q, k, v, gp, gn]
    if latent:
        cache_k, cache_v, bias = cache
        past = cache_k.shape[2]
        n_rows = seq_len // GRID_W

        def pattern(i):
            t = i % tiles_per_seq
            return jnp.where(t == 0, 0, jnp.where(t == tiles_per_seq - 1, 2, 1))

        in_specs += [
            pl.BlockSpec((1, N_HEADS, past, HEAD_DIM), lambda i: (i // tiles_per_seq, 0, 0, 0)),
            pl.BlockSpec((1, N_HEADS, past, HEAD_DIM), lambda i: (i // tiles_per_seq, 0, 0, 0)),
            pl.BlockSpec((1, N_HEADS, TILE, N_LOCAL), lambda i: (pattern(i), 0, 0, 0)),
        ]
        args += [cache_k, cache_v, bias]
        assert n_rows >= KEY_ROWS
    in_specs += [
        pl.BlockSpec((len(POOL_WINDOWS), POOL_GROUP, POOL_GROUP), lambda i: (0, 0, 0)),
        pl.BlockSpec((1, D_POOL), const2),
        pl.BlockSpec((D_POOL, D_MODEL), const2),
        pl.BlockSpec((D_NA, D_MODEL), const2),
        pl.BlockSpec((D_MODEL, D_MODEL), const2),
    ]
    args += [w_pool, s_pool, w_pp, w_np, w_o]
    return pl.pallas_call(
        functools.partial(_mixer_out_kernel, latent=latent, tiles_per_seq=tiles_per_seq),
        grid=(n_tiles,),
        in_specs=in_specs,
        out_specs=pl.BlockSpec((TILE, D_MODEL), tok),
        out_shape=jax.ShapeDtypeStruct((n_tok, D_MODEL), F32),
        scratch_shapes=[pltpu.VMEM((TILE, D_NA), F32)],
        compiler_params=pltpu.CompilerParams(
            dimension_semantics=("arbitrary",), vmem_limit_bytes=VMEM_LIMIT),
        name="mixer_out_lat" if latent else "mixer_out_ctx",
    )(*args)


def _conv_ffn_kernel(*refs, halo, tiles_per_seq):
    if halo:
        (x_ref, xprev_ref, xnext_ref, mod_ref, g_ref, wup_ref, cw_ref, cb_ref, wdn_ref,
         o_ref, acc_ref) = refs
    else:
        x_ref, mod_ref, g_ref, wup_ref, cw_ref, cb_ref, wdn_ref, o_ref, acc_ref = refs
    i = pl.program_id(0)
    t_in_seq = i % tiles_per_seq
    mod = mod_ref[0]
    shift = mod[:, 3 * D_MODEL:4 * D_MODEL]
    scale = mod[:, 4 * D_MODEL:5 * D_MODEL]
    gate = mod[:, 5 * D_MODEL:6 * D_MODEL]
    x = x_ref[...]
    if halo:
        x_ext = jnp.concatenate([xprev_ref[...], x, xnext_ref[...]], axis=0)
        pad = CONV_HALO
    else:
        x_ext = x
        pad = 0
    n_ext = TILE + 2 * pad
    h = _modulated_norm(x_ext, g_ref[...], shift, scale).astype(BF16)

    row = lax.broadcasted_iota(jnp.int32, (TILE, 1), 0)
    keep_prev = jnp.where((row == 0) & (t_in_seq == 0), 0.0, 1.0)
    keep_next = jnp.where((row == TILE - 1) & (t_in_seq == tiles_per_seq - 1), 0.0, 1.0)

    def conv(u, lo, hi):
        u_prev = pltpu.roll(u, 1, axis=0)[pad:pad + TILE] * keep_prev
        u_next = pltpu.roll(u, n_ext - 1, axis=0)[pad:pad + TILE] * keep_next
        return (u_prev * cw_ref[0:1, lo:hi] + u[pad:pad + TILE] * cw_ref[1:2, lo:hi]
                + u_next * cw_ref[2:3, lo:hi] + cb_ref[:, lo:hi])

    start = 0
    for ci, width in enumerate(FF_CHUNKS):
        lo, hi = start, start + width
        a = conv(_dot(h, wup_ref[:, lo:hi]), lo, hi)
        g = conv(_dot(h, wup_ref[:, D_FF + lo:D_FF + hi]), D_FF + lo, D_FF + hi)
        act = (a * _sigmoid(a) * g).astype(BF16)
        part = _dot(act, wdn_ref[lo:hi, :])
        if ci == 0:
            acc_ref[...] = part
        else:
            acc_ref[...] += part
        start = hi
    o_ref[...] = x + gate * acc_ref[...]


def _conv_ffn(x, mod, mod_index, norm_g, w_up, conv_w, conv_b, w_down, *, seq_len):
    n_tok = x.shape[0]
    n_tiles = n_tok // TILE
    tiles_per_seq = seq_len // TILE
    halo = tiles_per_seq > 1
    tok = lambda i: (i, 0)
    const = lambda i: (0, 0)
    in_specs = [pl.BlockSpec((TILE, D_MODEL), tok)]
    args = [x]
    if halo:
        hb = TILE // CONV_HALO
        n_hb = n_tok // CONV_HALO
        in_specs += [
            pl.BlockSpec((CONV_HALO, D_MODEL), lambda i: (jnp.maximum(i * hb - 1, 0), 0)),
            pl.BlockSpec((CONV_HALO, D_MODEL), lambda i: (jnp.minimum((i + 1) * hb, n_hb - 1), 0)),
        ]
        args += [x, x]
    in_specs += [
        pl.BlockSpec((1, 1, 6 * D_MODEL), lambda i: (mod_index(i), 0, 0)),
        pl.BlockSpec((1, D_MODEL), const),
        pl.BlockSpec((D_MODEL, 2 * D_FF), const),
        pl.BlockSpec((3, 2 * D_FF), const),
        pl.BlockSpec((1, 2 * D_FF), const),
        pl.BlockSpec((D_FF, D_MODEL), const),
    ]
    args += [mod, norm_g, w_up, conv_w, conv_b, w_down]
    return pl.pallas_call(
        functools.partial(_conv_ffn_kernel, halo=halo, tiles_per_seq=tiles_per_seq),
        grid=(n_tiles,),
        in_specs=in_specs,
        out_specs=pl.BlockSpec((TILE, D_MODEL), tok),
        out_shape=jax.ShapeDtypeStruct((n_tok, D_MODEL), F32),
        scratch_shapes=[pltpu.VMEM((TILE, D_MODEL), F32)],
        compiler_params=pltpu.CompilerParams(
            dimension_semantics=("arbitrary",), vmem_limit_bytes=VMEM_LIMIT),
        name="conv_ffn_lat" if halo else "conv_ffn_ctx",
    )(*args)


def _local_bias_table(rpb, n_rows):
    n_tiles = n_rows // TILE_ROWS
    n_dr, n_dc = 2 * WIN_H - 1, 2 * WIN_W - 1
    col = np.arange(GRID_W)
    dc = col[None, :] - col[:, None] + (WIN_W - 1)
    win_start = np.clip(col - WIN_W // 2, 0, GRID_W - WIN_W)
    col_ok = (col[None, :] >= win_start[:, None]) & (col[None, :] < win_start[:, None] + WIN_W)
    col_sel = (dc[None] == np.arange(n_dc)[:, None, None]).astype(np.float32)
    row_sel = np.zeros((3, TILE_ROWS, KEY_ROWS, n_dr), np.float32)
    for pi, t in enumerate((0, 1, n_tiles - 1)):
        key_row0 = np.clip(t * TILE_ROWS - WIN_H // 2, 0, n_rows - KEY_ROWS)
        for a in range(TILE_ROWS):
            r = t * TILE_ROWS + a
            row_start = np.clip(r - WIN_H // 2, 0, n_rows - WIN_H)
            for b in range(KEY_ROWS):
                kr = key_row0 + b
                if row_start <= kr < row_start + WIN_H:
                    row_sel[pi, a, b, kr - r + (WIN_H - 1)] = 1.0
    valid = (row_sel.sum(-1) > 0)[:, :, None, :, None] & col_ok[None, None, :, None, :]
    toeplitz = jnp.einsum("hdc,cqk->hdqk", rpb, col_sel, precision=lax.Precision.HIGHEST)
    table = jnp.einsum("pabd,hdqk->phaqbk", row_sel, toeplitz, precision=lax.Precision.HIGHEST)
    table = jnp.where(valid[:, None], table, NEG_INF)
    return table.reshape(3, N_HEADS, TILE, N_LOCAL)


def kernel(x_prompt, x_sample, cache_k, cache_v, c, c_ctx, norm_mix_g, norm_ffn_g, w_mod, b_mod,
           w_in, q_norm_g, k_norm_g, pool_w, pool_scale, na_rpb, w_pool_proj, w_na_proj, w_o,
           w_up, ffn_conv_w, ffn_conv_b, w_down):
    depth = w_in.shape[0]
    assert depth == 1
    batch, seq, _ = x_prompt.shape
    dec_batch, dec_seq, _ = x_sample.shape
    assert seq == TILE and dec_seq % TILE == 0 and dec_batch + 1 <= MOD_ROWS
    l = 0

    cond = jnp.zeros((MOD_ROWS, D_MODEL), F32).at[0].set(c_ctx).at[1:1 + dec_batch].set(c)
    mod = _modulation(cond, w_mod[l], b_mod[l]).reshape(MOD_ROWS, 1, 6 * D_MODEL)

    w_in_b = w_in[l].astype(BF16)
    w_pool_b = pool_w[l].astype(BF16)
    w_pp_b = w_pool_proj[l].astype(BF16)
    w_np_b = w_na_proj[l].astype(BF16)
    w_o_b = w_o[l].astype(BF16)
    w_up_b = w_up[l].astype(BF16)
    w_down_b = w_down[l].astype(BF16)
    g_mix = norm_mix_g[l].reshape(1, D_MODEL)
    g_ffn = norm_ffn_g[l].reshape(1, D_MODEL)
    q_g = jnp.tile(q_norm_g[l], N_HEADS).reshape(1, D_NA)
    k_g = jnp.tile(k_norm_g[l], N_HEADS).reshape(1, D_NA)
    s_pool = pool_scale[l].reshape(1, D_POOL)
    conv_b = ffn_conv_b[l].reshape(1, 2 * D_FF)
    head_of = np.arange(D_NA) // HEAD_DIM
    hsum = jnp.asarray(head_of[:, None] == head_of[None, :], BF16)

    ctx_mod = lambda i: 0
    xc = x_prompt.reshape(batch * seq, D_MODEL)
    p, q, k, v, gp, gn, new_k, new_v = _mixer_in(
        xc, mod, ctx_mod, g_mix, w_in_b, q_g, k_g, hsum, emit_kv=True)
    xc = _mixer_out(xc, mod, ctx_mod, p, q, k, v, gp, gn, w_pool_b, s_pool, w_pp_b, w_np_b, w_o_b,
                    seq_len=seq)
    xc = _conv_ffn(xc, mod, ctx_mod, g_ffn, w_up_b, ffn_conv_w[l], conv_b, w_down_b, seq_len=seq)
    y_prompt = xc.reshape(batch, seq, D_MODEL)

    tiles_per_seq = dec_seq // TILE
    lat_mod = lambda i: 1 + i // tiles_per_seq
    bias = _local_bias_table(na_rpb[l], dec_seq // GRID_W)
    cache = (cache_k[:, l].astype(BF16), cache_v[:, l].astype(BF16), bias)
    xs = x_sample.reshape(dec_batch * dec_seq, D_MODEL)
    p, q, k, v, gp, gn = _mixer_in(xs, mod, lat_mod, g_mix, w_in_b, q_g, k_g, hsum, emit_kv=False)
    xs = _mixer_out(xs, mod, lat_mod, p, q, k, v, gp, gn, w_pool_b, s_pool, w_pp_b, w_np_b, w_o_b,
                    seq_len=dec_seq, cache=cache)
    xs = _conv_ffn(xs, mod, lat_mod, g_ffn, w_up_b, ffn_conv_w[l], conv_b, w_down_b,
                   seq_len=dec_seq)
    y_sample = xs.reshape(dec_batch, dec_seq, D_MODEL)
    return (y_prompt, y_sample, new_k, new_v)
```

```python
import functools

import jax
import jax.numpy as jnp
import numpy as np
from jax import lax
from jax.experimental import pallas as pl
from jax.experimental.pallas import tpu as pltpu

D_MODEL = 1024
GRID_W = 64
N_HEADS = 8
HEAD_DIM = 64
D_NA = N_HEADS * HEAD_DIM
D_POOL = 512
POOL_WINDOWS = (2, 4, 8, 16)
POOL_GROUP = D_POOL // len(POOL_WINDOWS)
WIN_H = 8
WIN_W = 16
D_FF = 2816
Q_BLOCK = 128
EPS = 1e-6
NEG_INF = -1e30
OFF_Q = D_POOL
OFF_K = D_POOL + D_NA
OFF_V = D_POOL + 2 * D_NA
OFF_GP = D_POOL + 3 * D_NA
OFF_GN = OFF_GP + D_MODEL
D_IN = OFF_GN + D_MODEL

TILE = 256
TILE_ROWS = TILE // GRID_W
KEY_ROWS = 12
N_LOCAL = KEY_ROWS * GRID_W
POOL_HALO = 128
CONV_HALO = 8
FF_CHUNKS = (768, 768, 768, 512)
MOD_ROWS = 8
MOD_TILE = 1536
VMEM_LIMIT = 56 * 1024 * 1024

BF16 = jnp.bfloat16
F32 = jnp.float32


def _dot(a, b):
    return jnp.dot(a, b, preferred_element_type=F32)


def _dot_nt(a, b):
    return lax.dot_general(a, b, (((1,), (1,)), ((), ())), preferred_element_type=F32)


def _modulated_norm(x, g, shift, scale):
    ms = jnp.mean(x * x, axis=-1, keepdims=True)
    return (x * lax.rsqrt(ms + EPS) * g) * (1.0 + scale) + shift


def _sigmoid(x):
    return 1.0 / (1.0 + jnp.exp(-x))


def _split_bf16(x):
    hi = x.astype(BF16)
    lo = (x - hi.astype(F32)).astype(BF16)
    return hi, lo


def _mod_kernel(cond_ref, w_ref, b_ref, o_ref):
    cnd = cond_ref[...]
    s = (cnd * _sigmoid(cnd)).astype(BF16)
    o_ref[...] = _dot(s, w_ref[...].astype(BF16)) + b_ref[...]


def _modulation(cond, w_mod, b_mod):
    n = w_mod.shape[1]
    return pl.pallas_call(
        _mod_kernel,
        grid=(n // MOD_TILE,),
        in_specs=[
            pl.BlockSpec((MOD_ROWS, D_MODEL), lambda j: (0, 0)),
            pl.BlockSpec((D_MODEL, MOD_TILE), lambda j: (0, j)),
            pl.BlockSpec((1, MOD_TILE), lambda j: (0, j)),
        ],
        out_specs=pl.BlockSpec((MOD_ROWS, MOD_TILE), lambda j: (0, j)),
        out_shape=jax.ShapeDtypeStruct((MOD_ROWS, n), F32),
        compiler_params=pltpu.CompilerParams(
            dimension_semantics=("arbitrary",), vmem_limit_bytes=VMEM_LIMIT),
        name="modulation",
    )(cond, w_mod, b_mod.reshape(1, n))


def _mixer_in_kernel(x_ref, mod_ref, g_ref, w_ref, qg_ref, kg_ref, hsum_ref, *out_refs, emit_kv):
    p_ref, q_ref, k_ref, v_ref, gp_ref, gn_ref = out_refs[:6]
    mod = mod_ref[0]
    h = _modulated_norm(x_ref[...], g_ref[...], mod[:, 0:D_MODEL], mod[:, D_MODEL:2 * D_MODEL])
    h = h.astype(BF16)

    def proj(lo, hi):
        return _dot(h, w_ref[:, lo:hi])

    def head_norm(t, g):
        hi, lo = _split_bf16(t * t)
        ss = _dot(hi, hsum_ref[...]) + _dot(lo, hsum_ref[...])
        return t * lax.rsqrt(ss * (1.0 / HEAD_DIM) + EPS) * g

    p_ref[...] = proj(0, OFF_Q)
    q = head_norm(proj(OFF_Q, OFF_K), qg_ref[...])
    q = (q * (HEAD_DIM ** -0.5)).astype(BF16)
    if emit_kv:
        t = lax.broadcasted_iota(jnp.int32, (TILE, TILE), 0)
        j = lax.broadcasted_iota(jnp.int32, (TILE, TILE), 1)
        n_blocks = TILE // Q_BLOCK
        perm = jnp.where(j == (t % n_blocks) * Q_BLOCK + t // n_blocks, 1.0, 0.0).astype(BF16)
        q = _dot(perm, q).astype(BF16)
    q_ref[...] = q
    k = head_norm(proj(OFF_K, OFF_V), kg_ref[...])
    k_ref[...] = k.astype(BF16)
    v = proj(OFF_V, OFF_GP)
    v_ref[...] = v.astype(BF16)
    gp_ref[...] = proj(OFF_GP, OFF_GN)
    gn_ref[...] = proj(OFF_GN, D_IN)
    if emit_kv:
        nk_ref, nv_ref = out_refs[6:]
        for hd in range(N_HEADS):
            sl = slice(hd * HEAD_DIM, (hd + 1) * HEAD_DIM)
            nk_ref[0, 0, hd] = k[:, sl]
            nv_ref[0, 0, hd] = v[:, sl]


def _mixer_in(x, mod, mod_index, norm_g, w_in, q_g, k_g, hsum, *, emit_kv):
    n_tok = x.shape[0]
    n_tiles = n_tok // TILE
    tok = lambda i: (i, 0)
    const = lambda i: (0, 0)
    out_shape = [
        jax.ShapeDtypeStruct((n_tok, D_POOL), F32),
        jax.ShapeDtypeStruct((n_tok, D_NA), BF16),
        jax.ShapeDtypeStruct((n_tok, D_NA), BF16),
        jax.ShapeDtypeStruct((n_tok, D_NA), BF16),
        jax.ShapeDtypeStruct((n_tok, D_MODEL), F32),
        jax.ShapeDtypeStruct((n_tok, D_MODEL), F32),
    ]
    out_specs = [
        pl.BlockSpec((TILE, D_POOL), tok),
        pl.BlockSpec((TILE, D_NA), tok),
        pl.BlockSpec((TILE, D_NA), tok),
        pl.BlockSpec((TILE, D_NA), tok),
        pl.BlockSpec((TILE, D_MODEL), tok),
        pl.BlockSpec((TILE, D_MODEL), tok),
    ]
    if emit_kv:
        kv_shape = jax.ShapeDtypeStruct((n_tiles, 1, N_HEADS, TILE, HEAD_DIM), F32)
        kv_spec = pl.BlockSpec((1, 1, N_HEADS, TILE, HEAD_DIM), lambda i: (i, 0, 0, 0, 0))
        out_shape += [kv_shape, kv_shape]
        out_specs += [kv_spec, kv_spec]
    return pl.pallas_call(
        functools.partial(_mixer_in_kernel, emit_kv=emit_kv),
        grid=(n_tiles,),
        in_specs=[
            pl.BlockSpec((TILE, D_MODEL), tok),
            pl.BlockSpec((1, 1, 6 * D_MODEL), lambda i: (mod_index(i), 0, 0)),
            pl.BlockSpec((1, D_MODEL), const),
            pl.BlockSpec((D_MODEL, D_IN), const),
            pl.BlockSpec((1, D_NA), const),
            pl.BlockSpec((1, D_NA), const),
            pl.BlockSpec((D_NA, D_NA), const),
        ],
        out_specs=out_specs,
        out_shape=out_shape,
        compiler_params=pltpu.CompilerParams(
            dimension_semantics=("arbitrary",), vmem_limit_bytes=VMEM_LIMIT),
        name="mixer_in_ctx" if emit_kv else "mixer_in_lat",
    )(x, mod, norm_g, w_in, q_g, k_g, hsum)


def _pool_band(window, n_prev):
    n_k = TILE + 2 * n_prev
    t = lax.broadcasted_iota(jnp.int32, (TILE, n_k), 0)
    j = lax.broadcasted_iota(jnp.int32, (TILE, n_k), 1) - n_prev
    lo = t - window // 2
    return jnp.where((j >= lo) & (j < lo + window), 1.0, 0.0).astype(BF16)


def _key_row0(tile, n_rows):
    return jnp.clip(tile * TILE_ROWS - WIN_H // 2, 0, n_rows - KEY_ROWS)


def _fill_local_bias(bias_ref, toep_ref, tile, n_rows):
    key_row0 = int(np.clip(tile * TILE_ROWS - WIN_H // 2, 0, n_rows - KEY_ROWS))
    for a in range(TILE_ROWS):
        r = tile * TILE_ROWS + a
        row_start = int(np.clip(r - WIN_H // 2, 0, n_rows - WIN_H))
        for b in range(KEY_ROWS):
            kr = key_row0 + b
            dr = kr - r + (WIN_H - 1) if row_start <= kr < row_start + WIN_H else 2 * WIN_H - 1
            half = slice((b % 2) * GRID_W, (b % 2 + 1) * GRID_W)
            bias_ref[:, a * GRID_W:(a + 1) * GRID_W, b * GRID_W:(b + 1) * GRID_W] = (
                toep_ref[:, dr, :, half])


def _mixer_out_kernel(*refs, latent, tiles_per_seq):
    if latent:
        (x_ref, mod_ref, p_ref, pprev_ref, pnext_ref, q_ref, k_ref, v_ref, gp_ref, gn_ref,
         ck_ref, cv_ref, toep_ref, wpool_ref, spool_ref, wpp_ref, wnp_ref, wo_ref,
         o_ref, na_ref, bias_ref) = refs
    else:
        (x_ref, mod_ref, p_ref, q_ref, k_ref, v_ref, gp_ref, gn_ref,
         wpool_ref, spool_ref, wpp_ref, wnp_ref, wo_ref, o_ref, na_ref) = refs
    i = pl.program_id(0)
    t_in_seq = i % tiles_per_seq
    seq_len = TILE * tiles_per_seq

    if latent:
        n_rows = seq_len // GRID_W
        key0 = pl.multiple_of(_key_row0(t_in_seq, n_rows) * GRID_W, TILE)
        for tile in (0, 1, tiles_per_seq - 1):
            pl.when(t_in_seq == tile)(
                functools.partial(_fill_local_bias, bias_ref, toep_ref, tile, n_rows))
    for hd in range(N_HEADS):
        sl = slice(hd * HEAD_DIM, (hd + 1) * HEAD_DIM)
        qh = q_ref[:, sl]
        if latent:
            s_loc = _dot_nt(qh, k_ref[0, pl.ds(key0, N_LOCAL), sl]) + bias_ref[hd]
            s_ctx = _dot_nt(qh, ck_ref[0, hd])
            m = jnp.maximum(jnp.max(s_loc, axis=-1, keepdims=True),
                            jnp.max(s_ctx, axis=-1, keepdims=True))
            e_loc = jnp.exp(s_loc - m)
            e_ctx = jnp.exp(s_ctx - m)
            denom = (jnp.sum(e_loc, axis=-1, keepdims=True)
                     + jnp.sum(e_ctx, axis=-1, keepdims=True))
            o = (_dot(e_loc.astype(BF16), v_ref[0, pl.ds(key0, N_LOCAL), sl])
                 + _dot(e_ctx.astype(BF16), cv_ref[0, hd]))
        else:
            s = _dot_nt(qh, k_ref[0, :, sl])
            m = jnp.max(s, axis=-1, keepdims=True)
            e = jnp.exp(s - m)
            denom = jnp.sum(e, axis=-1, keepdims=True)
            o = _dot(e.astype(BF16), v_ref[0, :, sl])
        na_ref[:, sl] = o / denom

    p = p_ref[...]
    pos = t_in_seq * TILE + lax.broadcasted_iota(jnp.int32, (TILE, 1), 0)
    if latent:
        prev_ok = (t_in_seq > 0).astype(F32)
        next_ok = (t_in_seq < tiles_per_seq - 1).astype(F32)
        p_ext = jnp.concatenate([pprev_ref[...] * prev_ok, p, pnext_ref[...] * next_ok], axis=0)
        n_prev = POOL_HALO
    else:
        p_ext = p
        n_prev = 0
    p_hi, p_lo = _split_bf16(p_ext)
    mixed = []
    for gi, w in enumerate(POOL_WINDOWS):
        gs = slice(gi * POOL_GROUP, (gi + 1) * POOL_GROUP)
        band = _pool_band(w, n_prev)
        sums = _dot(band, p_hi[:, gs]) + _dot(band, p_lo[:, gs])
        lo = jnp.maximum(pos - w // 2, 0)
        hi = jnp.minimum(pos + (w - w // 2), seq_len)
        pooled = sums / (hi - lo).astype(F32) - p[:, gs]
        mixed.append(_dot(pooled.astype(BF16), wpool_ref[gi]))
    pool_out = jnp.concatenate(mixed, axis=-1) * spool_ref[...]

    a = _dot(pool_out.astype(BF16), wpp_ref[...])
    b = _dot(na_ref[...].astype(BF16), wnp_ref[...])
    merged = _sigmoid(gp_ref[...]) * a + _sigmoid(gn_ref[...]) * b
    y = _dot(merged.astype(BF16), wo_ref[...])
    gate = mod_ref[0][:, 2 * D_MODEL:3 * D_MODEL]
    o_ref[...] = x_ref[...] + gate * y


def _mixer_out(x, mod, mod_index, p, q, k, v, gp, gn, w_pool, s_pool, w_pp, w_np, w_o,
               *, seq_len, cache=None):
    n_tok = x.shape[0]
    n_tiles = n_tok // TILE
    tiles_per_seq = seq_len // TILE
    latent = cache is not None
    tok = lambda i: (i, 0)
    const2 = lambda i: (0, 0)
    seq = lambda i: (i // tiles_per_seq, 0, 0)
    k = k.reshape(n_tok // seq_len, seq_len, D_NA)
    v = v.reshape(n_tok // seq_len, seq_len, D_NA)

    in_specs = [
        pl.BlockSpec((TILE, D_MODEL), tok),
        pl.BlockSpec((1, 1, 6 * D_MODEL), lambda i: (mod_index(i), 0, 0)),
        pl.BlockSpec((TILE, D_POOL), tok),
    ]
    args = [x, mod, p]
    scratch_shapes = [pltpu.VMEM((TILE, D_NA), F32)]
    if latent:
        hb = TILE // POOL_HALO
        n_hb = n_tok // POOL_HALO
        in_specs += [
            pl.BlockSpec((POOL_HALO, D_POOL), lambda i: (jnp.maximum(i * hb - 1, 0), 0)),
            pl.BlockSpec((POOL_HALO, D_POOL), lambda i: (jnp.minimum((i + 1) * hb, n_hb - 1), 0)),
        ]
        args += [p, p]
    in_specs += [
        pl.BlockSpec((TILE, D_NA), tok),
        pl.BlockSpec((1, seq_len, D_NA), seq),
        pl.BlockSpec((1, seq_len, D_NA), seq),
        pl.BlockSpec((TILE, D_MODEL), tok),
        pl.BlockSpec((TILE, D_MODEL), tok),
    ]
    args += [q, k, v, gp, gn]
    if latent:
        cache_k, cache_v, toeplitz = cache
        past = cache_k.shape[2]
        in_specs += [
            pl.BlockSpec((1, N_HEADS, past, HEAD_DIM), lambda i: (i // tiles_per_seq, 0, 0, 0)),
            pl.BlockSpec((1, N_HEADS, past, HEAD_DIM), lambda i: (i // tiles_per_seq, 0, 0, 0)),
            pl.BlockSpec(toeplitz.shape, lambda i: (0, 0, 0, 0)),
        ]
        args += [cache_k, cache_v, toeplitz]
        scratch_shapes.append(pltpu.VMEM((N_HEADS, TILE, N_LOCAL), F32))
        assert tiles_per_seq >= 3 and seq_len // GRID_W >= KEY_ROWS
    in_specs += [
        pl.BlockSpec((len(POOL_WINDOWS), POOL_GROUP, POOL_GROUP), lambda i: (0, 0, 0)),
        pl.BlockSpec((1, D_POOL), const2),
        pl.BlockSpec((D_POOL, D_MODEL), const2),
        pl.BlockSpec((D_NA, D_MODEL), const2),
        pl.BlockSpec((D_MODEL, D_MODEL), const2),
    ]
    args += [w_pool, s_pool, w_pp, w_np, w_o]
    return pl.pallas_call(
        functools.partial(_mixer_out_kernel, latent=latent, tiles_per_seq=tiles_per_seq),
        grid=(n_tiles,),
        in_specs=in_specs,
        out_specs=pl.BlockSpec((TILE, D_MODEL), tok),
        out_shape=jax.ShapeDtypeStruct((n_tok, D_MODEL), F32),
        scratch_shapes=scratch_shapes,
        compiler_params=pltpu.CompilerParams(
            dimension_semantics=("arbitrary",), vmem_limit_bytes=VMEM_LIMIT),
        name="mixer_out_lat" if latent else "mixer_out_ctx",
    )(*args)


def _conv_ffn_kernel(*refs, halo, tiles_per_seq):
    if halo:
        (x_ref, xprev_ref, xnext_ref, mod_ref, g_ref, wup_ref, cw_ref, cb_ref, wdn_ref,
         o_ref, acc_ref) = refs
    else:
        x_ref, mod_ref, g_ref, wup_ref, cw_ref, cb_ref, wdn_ref, o_ref, acc_ref = refs
    i = pl.program_id(0)
    t_in_seq = i % tiles_per_seq
    mod = mod_ref[0]
    shift = mod[:, 3 * D_MODEL:4 * D_MODEL]
    scale = mod[:, 4 * D_MODEL:5 * D_MODEL]
    gate = mod[:, 5 * D_MODEL:6 * D_MODEL]
    x = x_ref[...]
    if halo:
        x_ext = jnp.concatenate([xprev_ref[...], x, xnext_ref[...]], axis=0)
        pad = CONV_HALO
    else:
        x_ext = x
        pad = 0
    n_ext = TILE + 2 * pad
    h = _modulated_norm(x_ext, g_ref[...], shift, scale).astype(BF16)

    row = lax.broadcasted_iota(jnp.int32, (TILE, 1), 0)
    keep_prev = jnp.where((row == 0) & (t_in_seq == 0), 0.0, 1.0)
    keep_next = jnp.where((row == TILE - 1) & (t_in_seq == tiles_per_seq - 1), 0.0, 1.0)

    def conv(u, lo, hi):
        u_prev = pltpu.roll(u, 1, axis=0)[pad:pad + TILE] * keep_prev
        u_next = pltpu.roll(u, n_ext - 1, axis=0)[pad:pad + TILE] * keep_next
        return (u_prev * cw_ref[0:1, lo:hi] + u[pad:pad + TILE] * cw_ref[1:2, lo:hi]
                + u_next * cw_ref[2:3, lo:hi] + cb_ref[:, lo:hi])

    start = 0
    for ci, width in enumerate(FF_CHUNKS):
        lo, hi = start, start + width
        a = conv(_dot(h, wup_ref[:, lo:hi]), lo, hi)
        g = conv(_dot(h, wup_ref[:, D_FF + lo:D_FF + hi]), D_FF + lo, D_FF + hi)
        act = (a * _sigmoid(a) * g).astype(BF16)
        part = _dot(act, wdn_ref[lo:hi, :])
        if ci == 0:
            acc_ref[...] = part
        else:
            acc_ref[...] += part
        start = hi
    o_ref[...] = x + gate * acc_ref[...]


def _conv_ffn(x, mod, mod_index, norm_g, w_up, conv_w, conv_b, w_down, *, seq_len):
    n_tok = x.shape[0]
    n_tiles = n_tok // TILE
    tiles_per_seq = seq_len // TILE
    halo = tiles_per_seq > 1
    tok = lambda i: (i, 0)
    const = lambda i: (0, 0)
    in_specs = [pl.BlockSpec((TILE, D_MODEL), tok)]
    args = [x]
    if halo:
        hb = TILE // CONV_HALO
        n_hb = n_tok // CONV_HALO
        in_specs += [
            pl.BlockSpec((CONV_HALO, D_MODEL), lambda i: (jnp.maximum(i * hb - 1, 0), 0)),
            pl.BlockSpec((CONV_HALO, D_MODEL), lambda i: (jnp.minimum((i + 1) * hb, n_hb - 1), 0)),
        ]
        args += [x, x]
    in_specs += [
        pl.BlockSpec((1, 1, 6 * D_MODEL), lambda i: (mod_index(i), 0, 0)),
        pl.BlockSpec((1, D_MODEL), const),
        pl.BlockSpec((D_MODEL, 2 * D_FF), const),
        pl.BlockSpec((3, 2 * D_FF), const),
        pl.BlockSpec((1, 2 * D_FF), const),
        pl.BlockSpec((D_FF, D_MODEL), const),
    ]
    args += [mod, norm_g, w_up, conv_w, conv_b, w_down]
    return pl.pallas_call(
        functools.partial(_conv_ffn_kernel, halo=halo, tiles_per_seq=tiles_per_seq),
        grid=(n_tiles,),
        in_specs=in_specs,
        out_specs=pl.BlockSpec((TILE, D_MODEL), tok),
        out_shape=jax.ShapeDtypeStruct((n_tok, D_MODEL), F32),
        scratch_shapes=[pltpu.VMEM((TILE, D_MODEL), F32)],
        compiler_params=pltpu.CompilerParams(
            dimension_semantics=("arbitrary",), vmem_limit_bytes=VMEM_LIMIT),
        name="conv_ffn_lat" if halo else "conv_ffn_ctx",
    )(*args)


def _toeplitz_bias(rpb):
    n_dc = 2 * WIN_W - 1
    col = np.arange(GRID_W)
    dc = col[None, :] - col[:, None] + (WIN_W - 1)
    win_start = np.clip(col - WIN_W // 2, 0, GRID_W - WIN_W)
    col_ok = (col[None, :] >= win_start[:, None]) & (col[None, :] < win_start[:, None] + WIN_W)
    col_sel = (dc[None] == np.arange(n_dc)[:, None, None]).astype(np.float32)
    blocks = jnp.einsum("hdc,cqk->hdqk", rpb, col_sel, precision=lax.Precision.HIGHEST)
    blocks = jnp.where(col_ok[None, None], blocks, NEG_INF)
    blocks = jnp.pad(blocks, ((0, 0), (0, 1), (0, 0), (0, 0)), constant_values=NEG_INF)
    return jnp.concatenate([blocks, blocks], axis=-1)


def kernel(x_prompt, x_sample, cache_k, cache_v, c, c_ctx, norm_mix_g, norm_ffn_g, w_mod, b_mod,
           w_in, q_norm_g, k_norm_g, pool_w, pool_scale, na_rpb, w_pool_proj, w_na_proj, w_o,
           w_up, ffn_conv_w, ffn_conv_b, w_down):
    depth = w_in.shape[0]
    assert depth == 1
    batch, seq, _ = x_prompt.shape
    dec_batch, dec_seq, _ = x_sample.shape
    assert seq == TILE and dec_seq % TILE == 0 and dec_batch + 1 <= MOD_ROWS
    l = 0

    cond = jnp.zeros((MOD_ROWS, D_MODEL), F32).at[0].set(c_ctx).at[1:1 + dec_batch].set(c)
    mod = _modulation(cond, w_mod[l], b_mod[l]).reshape(MOD_ROWS, 1, 6 * D_MODEL)

    w_in_b = w_in[l].astype(BF16)
    w_pool_b = pool_w[l].astype(BF16)
    w_pp_b = w_pool_proj[l].astype(BF16)
    w_np_b = w_na_proj[l].astype(BF16)
    w_o_b = w_o[l].astype(BF16)
    w_up_b = w_up[l].astype(BF16)
    w_down_b = w_down[l].astype(BF16)
    g_mix = norm_mix_g[l].reshape(1, D_MODEL)
    g_ffn = norm_ffn_g[l].reshape(1, D_MODEL)
    q_g = jnp.tile(q_norm_g[l], N_HEADS).reshape(1, D_NA)
    k_g = jnp.tile(k_norm_g[l], N_HEADS).reshape(1, D_NA)
    s_pool = pool_scale[l].reshape(1, D_POOL)
    conv_b = ffn_conv_b[l].reshape(1, 2 * D_FF)
    head_of = np.arange(D_NA) // HEAD_DIM
    hsum = jnp.asarray(head_of[:, None] == head_of[None, :], BF16)

    ctx_mod = lambda i: 0
    xc = x_prompt.reshape(batch * seq, D_MODEL)
    p, q, k, v, gp, gn, new_k, new_v = _mixer_in(
        xc, mod, ctx_mod, g_mix, w_in_b, q_g, k_g, hsum, emit_kv=True)
    xc = _mixer_out(xc, mod, ctx_mod, p, q, k, v, gp, gn, w_pool_b, s_pool, w_pp_b, w_np_b, w_o_b,
                    seq_len=seq)
    xc = _conv_ffn(xc, mod, ctx_mod, g_ffn, w_up_b, ffn_conv_w[l], conv_b, w_down_b, seq_len=seq)
    y_prompt = xc.reshape(batch, seq, D_MODEL)

    tiles_per_seq = dec_seq // TILE
    lat_mod = lambda i: 1 + i // tiles_per_seq
    cache = (cache_k[:, l].astype(BF16), cache_v[:, l].astype(BF16), _toeplitz_bias(na_rpb[l]))
    xs = x_sample.reshape(dec_batch * dec_seq, D_MODEL)
    p, q, k, v, gp, gn = _mixer_in(xs, mod, lat_mod, g_mix, w_in_b, q_g, k_g, hsum, emit_kv=False)
    xs = _mixer_out(xs, mod, lat_mod, p, q, k, v, gp, gn, w_pool_b, s_pool, w_pp_b, w_np_b, w_o_b,
                    seq_len=dec_seq, cache=cache)
    xs = _conv_ffn(xs, mod, lat_mod, g_ffn, w_up_b, ffn_conv_w[l], conv_b, w_down_b,
                   seq_len=dec_seq)
    y_sample = xs.reshape(dec_batch, dec_seq, D_MODEL)
    return (y_prompt, y_sample, new_k, new_v)
```

```python
import functools

import jax
import jax.numpy as jnp
import numpy as np
from jax import lax
from jax.experimental import pallas as pl
from jax.experimental.pallas import tpu as pltpu

D_MODEL = 1024
GRID_W = 64
N_HEADS = 8
HEAD_DIM = 64
D_NA = N_HEADS * HEAD_DIM
D_POOL = 512
POOL_WINDOWS = (2, 4, 8, 16)
POOL_GROUP = D_POOL // len(POOL_WINDOWS)
WIN_H = 8
WIN_W = 16
D_FF = 2816
Q_BLOCK = 128
EPS = 1e-6
NEG_INF = -1e30
OFF_Q = D_POOL
OFF_K = D_POOL + D_NA
OFF_V = D_POOL + 2 * D_NA
OFF_GP = D_POOL + 3 * D_NA
OFF_GN = OFF_GP + D_MODEL
D_IN = OFF_GN + D_MODEL

TILE = 256
TILE_ROWS = TILE // GRID_W
KEY_ROWS = 12
N_LOCAL = KEY_ROWS * GRID_W
POOL_HALO = 128
SUBLANES = 8
LANES = 128
CONV_HALO = SUBLANES
FF_CHUNK = 256
N_FF_CHUNKS = D_FF // FF_CHUNK
PIPE_DEPTH = 3
MOD_ROWS = 8
MOD_TILE = 1536
VMEM_LIMIT = 56 * 1024 * 1024

BF16 = jnp.bfloat16
F32 = jnp.float32


def _dot(a, b):
    return jnp.dot(a, b, preferred_element_type=F32)


def _dot_nt(a, b):
    return lax.dot_general(a, b, (((1,), (1,)), ((), ())), preferred_element_type=F32)


def _modulated_norm(x, g, shift, scale):
    ms = jnp.mean(x * x, axis=-1, keepdims=True)
    return (x * lax.rsqrt(ms + EPS) * g) * (1.0 + scale) + shift


def _sigmoid(x):
    return 1.0 / (1.0 + jnp.exp(-x))


def _split_bf16(x):
    hi = x.astype(BF16)
    lo = (x - hi.astype(F32)).astype(BF16)
    return hi, lo


def _mod_kernel(cond_ref, w_ref, b_ref, o_ref):
    cnd = cond_ref[...]
    s = (cnd * _sigmoid(cnd)).astype(BF16)
    o_ref[...] = _dot(s, w_ref[...].astype(BF16)) + b_ref[...]


def _modulation(cond, w_mod, b_mod):
    n = w_mod.shape[1]
    return pl.pallas_call(
        _mod_kernel,
        grid=(n // MOD_TILE,),
        in_specs=[
            pl.BlockSpec((MOD_ROWS, D_MODEL), lambda j: (0, 0)),
            pl.BlockSpec((D_MODEL, MOD_TILE), lambda j: (0, j)),
            pl.BlockSpec((1, MOD_TILE), lambda j: (0, j)),
        ],
        out_specs=pl.BlockSpec((MOD_ROWS, MOD_TILE), lambda j: (0, j)),
        out_shape=jax.ShapeDtypeStruct((MOD_ROWS, n), F32),
        compiler_params=pltpu.CompilerParams(
            dimension_semantics=("arbitrary",), vmem_limit_bytes=VMEM_LIMIT),
        name="modulation",
    )(cond, w_mod, b_mod.reshape(1, n))


def _mixer_in_kernel(x_ref, mod_ref, g_ref, w_ref, qg_ref, kg_ref, *out_refs, emit_kv):
    p_ref, q_ref, k_ref, v_ref, gp_ref, gn_ref = out_refs[:6]
    mod = mod_ref[0]
    h = _modulated_norm(x_ref[...], g_ref[...], mod[:, 0:D_MODEL], mod[:, D_MODEL:2 * D_MODEL])
    h = h.astype(BF16)

    def proj(lo, hi):
        return _dot(h, w_ref[:, lo:hi])

    first_head = lax.broadcasted_iota(jnp.int32, (1, LANES), 1) < HEAD_DIM

    def head_norm(t, g):
        assert 2 * HEAD_DIM == LANES
        tt = t * t
        blocks = []
        for j in range(D_NA // LANES):
            blk = tt[:, j * LANES:(j + 1) * LANES]
            both = jnp.sum(blk, axis=-1, keepdims=True)
            one = jnp.sum(jnp.where(first_head, blk, 0.0), axis=-1, keepdims=True)
            blocks.append(jnp.where(first_head, one, both - one))
        ss = jnp.concatenate(blocks, axis=-1)
        return t * lax.rsqrt(ss * (1.0 / HEAD_DIM) + EPS) * g

    q = proj(OFF_Q, OFF_K)
    k = proj(OFF_K, OFF_V)
    p_ref[...] = proj(0, OFF_Q)
    v = proj(OFF_V, OFF_GP)
    v_ref[...] = v.astype(BF16)
    q = (head_norm(q, qg_ref[...]) * (HEAD_DIM ** -0.5)).astype(BF16)
    gp_ref[...] = proj(OFF_GP, OFF_GN)
    k = head_norm(k, kg_ref[...])
    k_ref[...] = k.astype(BF16)
    gn_ref[...] = proj(OFF_GN, D_IN)
    if emit_kv:
        t = lax.broadcasted_iota(jnp.int32, (TILE, TILE), 0)
        j = lax.broadcasted_iota(jnp.int32, (TILE, TILE), 1)
        n_blocks = TILE // Q_BLOCK
        perm = jnp.where(j == (t % n_blocks) * Q_BLOCK + t // n_blocks, 1.0, 0.0).astype(BF16)
        q = _dot(perm, q).astype(BF16)
    q_ref[...] = q
    if emit_kv:
        nk_ref, nv_ref = out_refs[6:]
        for hd in range(N_HEADS):
            sl = slice(hd * HEAD_DIM, (hd + 1) * HEAD_DIM)
            nk_ref[0, 0, hd] = k[:, sl]
            nv_ref[0, 0, hd] = v[:, sl]


def _mixer_in(x, mod, mod_index, norm_g, w_in, q_g, k_g, *, emit_kv):
    n_tok = x.shape[0]
    n_tiles = n_tok // TILE
    tok = lambda i: (i, 0)
    const = lambda i: (0, 0)
    out_shape = [
        jax.ShapeDtypeStruct((n_tok, D_POOL), F32),
        jax.ShapeDtypeStruct((n_tok, D_NA), BF16),
        jax.ShapeDtypeStruct((n_tok, D_NA), BF16),
        jax.ShapeDtypeStruct((n_tok, D_NA), BF16),
        jax.ShapeDtypeStruct((n_tok, D_MODEL), F32),
        jax.ShapeDtypeStruct((n_tok, D_MODEL), F32),
    ]
    out_specs = [
        pl.BlockSpec((TILE, D_POOL), tok),
        pl.BlockSpec((TILE, D_NA), tok),
        pl.BlockSpec((TILE, D_NA), tok),
        pl.BlockSpec((TILE, D_NA), tok),
        pl.BlockSpec((TILE, D_MODEL), tok),
        pl.BlockSpec((TILE, D_MODEL), tok),
    ]
    if emit_kv:
        kv_shape = jax.ShapeDtypeStruct((n_tiles, 1, N_HEADS, TILE, HEAD_DIM), F32)
        kv_spec = pl.BlockSpec((1, 1, N_HEADS, TILE, HEAD_DIM), lambda i: (i, 0, 0, 0, 0))
        out_shape += [kv_shape, kv_shape]
        out_specs += [kv_spec, kv_spec]
    return pl.pallas_call(
        functools.partial(_mixer_in_kernel, emit_kv=emit_kv),
        grid=(n_tiles,),
        in_specs=[
            pl.BlockSpec((TILE, D_MODEL), tok),
            pl.BlockSpec((1, 1, 6 * D_MODEL), lambda i: (mod_index(i), 0, 0)),
            pl.BlockSpec((1, D_MODEL), const),
            pl.BlockSpec((D_MODEL, D_IN), const),
            pl.BlockSpec((1, D_NA), const),
            pl.BlockSpec((1, D_NA), const),
        ],
        out_specs=out_specs,
        out_shape=out_shape,
        compiler_params=pltpu.CompilerParams(
            dimension_semantics=("arbitrary",), vmem_limit_bytes=VMEM_LIMIT),
        name="mixer_in_ctx" if emit_kv else "mixer_in_lat",
    )(x, mod, norm_g, w_in, q_g, k_g)


def _pool_band(window, n_prev):
    n_k = TILE + 2 * n_prev
    t = lax.broadcasted_iota(jnp.int32, (TILE, n_k), 0)
    j = lax.broadcasted_iota(jnp.int32, (TILE, n_k), 1) - n_prev
    lo = t - window // 2
    return jnp.where((j >= lo) & (j < lo + window), 1.0, 0.0).astype(BF16)


def _key_row0(tile, n_rows):
    return jnp.clip(tile * TILE_ROWS - WIN_H // 2, 0, n_rows - KEY_ROWS)


def _fill_local_bias(bias_ref, toep_ref, tile, n_rows):
    key_row0 = int(np.clip(tile * TILE_ROWS - WIN_H // 2, 0, n_rows - KEY_ROWS))
    for a in range(TILE_ROWS):
        r = tile * TILE_ROWS + a
        row_start = int(np.clip(r - WIN_H // 2, 0, n_rows - WIN_H))
        for b in range(KEY_ROWS):
            kr = key_row0 + b
            dr = kr - r + (WIN_H - 1) if row_start <= kr < row_start + WIN_H else 2 * WIN_H - 1
            half = slice((b % 2) * GRID_W, (b % 2 + 1) * GRID_W)
            bias_ref[:, a * GRID_W:(a + 1) * GRID_W, b * GRID_W:(b + 1) * GRID_W] = (
                toep_ref[:, dr, :, half])


def _mixer_out_kernel(*refs, latent, tiles_per_seq):
    if latent:
        (x_ref, mod_ref, p_ref, pprev_ref, pnext_ref, q_ref, k_ref, v_ref, gp_ref, gn_ref,
         ck_ref, cv_ref, toep_ref, wpool_ref, spool_ref, wpp_ref, wnp_ref, wo_ref,
         o_ref, na_ref, bias_ref) = refs
    else:
        (x_ref, mod_ref, p_ref, q_ref, k_ref, v_ref, gp_ref, gn_ref,
         wpool_ref, spool_ref, wpp_ref, wnp_ref, wo_ref, o_ref, na_ref) = refs
    i = pl.program_id(0)
    t_in_seq = i % tiles_per_seq
    seq_len = TILE * tiles_per_seq

    if latent:
        n_rows = seq_len // GRID_W
        key0 = pl.multiple_of(_key_row0(t_in_seq, n_rows) * GRID_W, TILE)
        for tile in (0, 1, tiles_per_seq - 1):
            pl.when(t_in_seq == tile)(
                functools.partial(_fill_local_bias, bias_ref, toep_ref, tile, n_rows))
    for hd in range(N_HEADS):
        sl = slice(hd * HEAD_DIM, (hd + 1) * HEAD_DIM)
        qh = q_ref[:, sl]
        if latent:
            s_loc = _dot_nt(qh, k_ref[0, pl.ds(key0, N_LOCAL), sl]) + bias_ref[hd]
            s_ctx = _dot_nt(qh, ck_ref[0, hd])
            m = jnp.maximum(jnp.max(s_loc, axis=-1, keepdims=True),
                            jnp.max(s_ctx, axis=-1, keepdims=True))
            e_loc = jnp.exp(s_loc - m)
            e_ctx = jnp.exp(s_ctx - m)
            denom = (jnp.sum(e_loc, axis=-1, keepdims=True)
                     + jnp.sum(e_ctx, axis=-1, keepdims=True))
            o = (_dot(e_loc.astype(BF16), v_ref[0, pl.ds(key0, N_LOCAL), sl])
                 + _dot(e_ctx.astype(BF16), cv_ref[0, hd]))
        else:
            s = _dot_nt(qh, k_ref[0, :, sl])
            m = jnp.max(s, axis=-1, keepdims=True)
            e = jnp.exp(s - m)
            denom = jnp.sum(e, axis=-1, keepdims=True)
            o = _dot(e.astype(BF16), v_ref[0, :, sl])
        na_ref[:, sl] = o / denom

    p = p_ref[...]
    pos = t_in_seq * TILE + lax.broadcasted_iota(jnp.int32, (TILE, 1), 0)
    if latent:
        prev_ok = (t_in_seq > 0).astype(F32)
        next_ok = (t_in_seq < tiles_per_seq - 1).astype(F32)
        p_ext = jnp.concatenate([pprev_ref[...] * prev_ok, p, pnext_ref[...] * next_ok], axis=0)
        n_prev = POOL_HALO
    else:
        p_ext = p
        n_prev = 0
    p_hi, p_lo = _split_bf16(p_ext)
    mixed = []
    for gi, w in enumerate(POOL_WINDOWS):
        gs = slice(gi * POOL_GROUP, (gi + 1) * POOL_GROUP)
        band = _pool_band(w, n_prev)
        sums = _dot(band, p_hi[:, gs]) + _dot(band, p_lo[:, gs])
        lo = jnp.maximum(pos - w // 2, 0)
        hi = jnp.minimum(pos + (w - w // 2), seq_len)
        pooled = sums / (hi - lo).astype(F32) - p[:, gs]
        mixed.append(_dot(pooled.astype(BF16), wpool_ref[gi]))
    pool_out = jnp.concatenate(mixed, axis=-1) * spool_ref[...]

    a = _dot(pool_out.astype(BF16), wpp_ref[...])
    b = _dot(na_ref[...].astype(BF16), wnp_ref[...])
    merged = _sigmoid(gp_ref[...]) * a + _sigmoid(gn_ref[...]) * b
    y = _dot(merged.astype(BF16), wo_ref[...])
    gate = mod_ref[0][:, 2 * D_MODEL:3 * D_MODEL]
    o_ref[...] = x_ref[...] + gate * y


def _mixer_out(x, mod, mod_index, p, q, k, v, gp, gn, w_pool, s_pool, w_pp, w_np, w_o,
               *, seq_len, cache=None):
    n_tok = x.shape[0]
    n_tiles = n_tok // TILE
    tiles_per_seq = seq_len // TILE
    latent = cache is not None
    tok = lambda i: (i, 0)
    const2 = lambda i: (0, 0)
    seq = lambda i: (i // tiles_per_seq, 0, 0)
    k = k.reshape(n_tok // seq_len, seq_len, D_NA)
    v = v.reshape(n_tok // seq_len, seq_len, D_NA)

    in_specs = [
        pl.BlockSpec((TILE, D_MODEL), tok),
        pl.BlockSpec((1, 1, 6 * D_MODEL), lambda i: (mod_index(i), 0, 0)),
        pl.BlockSpec((TILE, D_POOL), tok),
    ]
    args = [x, mod, p]
    scratch_shapes = [pltpu.VMEM((TILE, D_NA), F32)]
    if latent:
        hb = TILE // POOL_HALO
        n_hb = n_tok // POOL_HALO
        in_specs += [
            pl.BlockSpec((POOL_HALO, D_POOL), lambda i: (jnp.maximum(i * hb - 1, 0), 0)),
            pl.BlockSpec((POOL_HALO, D_POOL), lambda i: (jnp.minimum((i + 1) * hb, n_hb - 1), 0)),
        ]
        args += [p, p]
    in_specs += [
        pl.BlockSpec((TILE, D_NA), tok),
        pl.BlockSpec((1, seq_len, D_NA), seq),
        pl.BlockSpec((1, seq_len, D_NA), seq),
        pl.BlockSpec((TILE, D_MODEL), tok),
        pl.BlockSpec((TILE, D_MODEL), tok),
    ]
    args += [q, k, v, gp, gn]
    if latent:
        cache_k, cache_v, toeplitz = cache
        past = cache_k.shape[2]
        in_specs += [
            pl.BlockSpec((1, N_HEADS, past, HEAD_DIM), lambda i: (i // tiles_per_seq, 0, 0, 0)),
            pl.BlockSpec((1, N_HEADS, past, HEAD_DIM), lambda i: (i // tiles_per_seq, 0, 0, 0)),
            pl.BlockSpec(toeplitz.shape, lambda i: (0, 0, 0, 0)),
        ]
        args += [cache_k, cache_v, toeplitz]
        scratch_shapes.append(pltpu.VMEM((N_HEADS, TILE, N_LOCAL), F32))
        assert tiles_per_seq >= 3 and seq_len // GRID_W >= KEY_ROWS
    in_specs += [
        pl.BlockSpec((len(POOL_WINDOWS), POOL_GROUP, POOL_GROUP), lambda i: (0, 0, 0)),
        pl.BlockSpec((1, D_POOL), const2),
        pl.BlockSpec((D_POOL, D_MODEL), const2),
        pl.BlockSpec((D_NA, D_MODEL), const2),
        pl.BlockSpec((D_MODEL, D_MODEL), const2),
    ]
    args += [w_pool, s_pool, w_pp, w_np, w_o]
    return pl.pallas_call(
        functools.partial(_mixer_out_kernel, latent=latent, tiles_per_seq=tiles_per_seq),
        grid=(n_tiles,),
        in_specs=in_specs,
        out_specs=pl.BlockSpec((TILE, D_MODEL), tok),
        out_shape=jax.ShapeDtypeStruct((n_tok, D_MODEL), F32),
        scratch_shapes=scratch_shapes,
        compiler_params=pltpu.CompilerParams(
            dimension_semantics=("arbitrary",), vmem_limit_bytes=VMEM_LIMIT),
        name="mixer_out_lat" if latent else "mixer_out_ctx",
    )(*args)


def _conv_ffn_kernel(*refs, halo, tiles_per_seq):
    if halo:
        (x_ref, xprev_ref, xnext_ref, mod_ref, g_ref, wup_ref, cw_ref, cb_ref, wdn_ref,
         o_ref, h_ref, u_ref, acc_ref) = refs
    else:
        (x_ref, mod_ref, g_ref, wup_ref, cw_ref, cb_ref, wdn_ref,
         o_ref, h_ref, u_ref, acc_ref) = refs
    i = pl.program_id(0)
    t_in_seq = i % tiles_per_seq
    mod = mod_ref[0]
    shift = mod[:, 3 * D_MODEL:4 * D_MODEL]
    scale = mod[:, 4 * D_MODEL:5 * D_MODEL]

    if halo:
        x_ext = jnp.concatenate([xprev_ref[...], x_ref[...], xnext_ref[...]], axis=0)
        pad = CONV_HALO
    else:
        x_ext = x_ref[...]
        pad = 0
    n_ext = TILE + 2 * pad
    h_ref[...] = _modulated_norm(x_ext, g_ref[...], shift, scale).astype(BF16)

    row = lax.broadcasted_iota(jnp.int32, (SUBLANES, 1), 0)
    keep_first = jnp.where((row == 0) & (t_in_seq == 0), 0.0, 1.0)
    keep_last = jnp.where((row == SUBLANES - 1) & (t_in_seq == tiles_per_seq - 1), 0.0, 1.0)

    def up_proj(c):
        u_ref[c] = _dot(h_ref[...], wup_ref[c])

    def gated_down(c):
        u = u_ref[c]
        u_prev = pltpu.roll(u, 1, axis=0)[pad:pad + TILE]
        u_prev = jnp.concatenate([u_prev[:SUBLANES] * keep_first, u_prev[SUBLANES:]], axis=0)
        u_next = pltpu.roll(u, n_ext - 1, axis=0)[pad:pad + TILE]
        u_next = jnp.concatenate([u_next[:-SUBLANES], u_next[-SUBLANES:] * keep_last], axis=0)
        cw = cw_ref[c]
        y = u_prev * cw[0:1] + u[pad:pad + TILE] * cw[1:2] + u_next * cw[2:3] + cb_ref[c]
        a, g = y[:, :FF_CHUNK], y[:, FF_CHUNK:]
        act = (a * _sigmoid(a) * g).astype(BF16)
        return _dot(act, wdn_ref[c])

    for c in range(min(PIPE_DEPTH, N_FF_CHUNKS)):
        up_proj(c)
    for c in range(N_FF_CHUNKS):
        if c + PIPE_DEPTH < N_FF_CHUNKS:
            up_proj(c + PIPE_DEPTH)
        part = gated_down(c)
        if c == 0:
            acc_ref[...] = part
        elif c + 1 < N_FF_CHUNKS:
            acc_ref[...] += part
        else:
            o_ref[...] = x_ref[...] + mod[:, 5 * D_MODEL:6 * D_MODEL] * (acc_ref[...] + part)


def _conv_ffn(x, mod, mod_index, norm_g, w_up, conv_w, conv_b, w_down, *, seq_len):
    n_tok = x.shape[0]
    n_tiles = n_tok // TILE
    tiles_per_seq = seq_len // TILE
    halo = tiles_per_seq > 1
    n_ext = TILE + 2 * CONV_HALO if halo else TILE
    tok = lambda i: (i, 0)
    const2 = lambda i: (0, 0)
    const3 = lambda i: (0, 0, 0)
    in_specs = [pl.BlockSpec((TILE, D_MODEL), tok)]
    args = [x]
    if halo:
        hb = TILE // CONV_HALO
        n_hb = n_tok // CONV_HALO
        in_specs += [
            pl.BlockSpec((CONV_HALO, D_MODEL), lambda i: (jnp.maximum(i * hb - 1, 0), 0)),
            pl.BlockSpec((CONV_HALO, D_MODEL), lambda i: (jnp.minimum((i + 1) * hb, n_hb - 1), 0)),
        ]
        args += [x, x]
    in_specs += [
        pl.BlockSpec((1, 1, 6 * D_MODEL), lambda i: (mod_index(i), 0, 0)),
        pl.BlockSpec((1, D_MODEL), const2),
        pl.BlockSpec(w_up.shape, const3),
        pl.BlockSpec(conv_w.shape, const3),
        pl.BlockSpec(conv_b.shape, const3),
        pl.BlockSpec(w_down.shape, const3),
    ]
    args += [mod, norm_g, w_up, conv_w, conv_b, w_down]
    return pl.pallas_call(
        functools.partial(_conv_ffn_kernel, halo=halo, tiles_per_seq=tiles_per_seq),
        grid=(n_tiles,),
        in_specs=in_specs,
        out_specs=pl.BlockSpec((TILE, D_MODEL), tok),
        out_shape=jax.ShapeDtypeStruct((n_tok, D_MODEL), F32),
        scratch_shapes=[
            pltpu.VMEM((n_ext, D_MODEL), BF16),
            pltpu.VMEM((N_FF_CHUNKS, n_ext, 2 * FF_CHUNK), F32),
            pltpu.VMEM((TILE, D_MODEL), F32),
        ],
        compiler_params=pltpu.CompilerParams(
            dimension_semantics=("arbitrary",), vmem_limit_bytes=VMEM_LIMIT),
        name="conv_ffn_lat" if halo else "conv_ffn_ctx",
    )(*args)


def _chunk_ff(t):
    lead = t.shape[:-1]
    t = t.reshape(lead + (2, N_FF_CHUNKS, FF_CHUNK))
    t = jnp.moveaxis(t, -2, 0)
    return t.reshape((N_FF_CHUNKS,) + lead + (2 * FF_CHUNK,))


def _toeplitz_bias(rpb):
    n_dc = 2 * WIN_W - 1
    col = np.arange(GRID_W)
    dc = col[None, :] - col[:, None] + (WIN_W - 1)
    win_start = np.clip(col - WIN_W // 2, 0, GRID_W - WIN_W)
    col_ok = (col[None, :] >= win_start[:, None]) & (col[None, :] < win_start[:, None] + WIN_W)
    col_sel = (dc[None] == np.arange(n_dc)[:, None, None]).astype(np.float32)
    blocks = jnp.einsum("hdc,cqk->hdqk", rpb, col_sel, precision=lax.Precision.HIGHEST)
    blocks = jnp.where(col_ok[None, None], blocks, NEG_INF)
    blocks = jnp.pad(blocks, ((0, 0), (0, 1), (0, 0), (0, 0)), constant_values=NEG_INF)
    return jnp.concatenate([blocks, blocks], axis=-1)


def kernel(x_prompt, x_sample, cache_k, cache_v, c, c_ctx, norm_mix_g, norm_ffn_g, w_mod, b_mod,
           w_in, q_norm_g, k_norm_g, pool_w, pool_scale, na_rpb, w_pool_proj, w_na_proj, w_o,
           w_up, ffn_conv_w, ffn_conv_b, w_down):
    depth = w_in.shape[0]
    assert depth == 1
    batch, seq, _ = x_prompt.shape
    dec_batch, dec_seq, _ = x_sample.shape
    assert seq == TILE and dec_seq % TILE == 0 and dec_batch + 1 <= MOD_ROWS
    l = 0

    cond = jnp.zeros((MOD_ROWS, D_MODEL), F32).at[0].set(c_ctx).at[1:1 + dec_batch].set(c)
    mod = _modulation(cond, w_mod[l], b_mod[l]).reshape(MOD_ROWS, 1, 6 * D_MODEL)

    w_in_b = w_in[l].astype(BF16)
    w_pool_b = pool_w[l].astype(BF16)
    w_pp_b = w_pool_proj[l].astype(BF16)
    w_np_b = w_na_proj[l].astype(BF16)
    w_o_b = w_o[l].astype(BF16)
    w_up_b = _chunk_ff(w_up[l]).astype(BF16)
    w_down_b = w_down[l].astype(BF16).reshape(N_FF_CHUNKS, FF_CHUNK, D_MODEL)
    conv_w = _chunk_ff(ffn_conv_w[l])
    g_mix = norm_mix_g[l].reshape(1, D_MODEL)
    g_ffn = norm_ffn_g[l].reshape(1, D_MODEL)
    q_g = jnp.tile(q_norm_g[l], N_HEADS).reshape(1, D_NA)
    k_g = jnp.tile(k_norm_g[l], N_HEADS).reshape(1, D_NA)
    s_pool = pool_scale[l].reshape(1, D_POOL)
    conv_b = _chunk_ff(ffn_conv_b[l].reshape(1, 2 * D_FF))

    ctx_mod = lambda i: 0
    xc = x_prompt.reshape(batch * seq, D_MODEL)
    p, q, k, v, gp, gn, new_k, new_v = _mixer_in(
        xc, mod, ctx_mod, g_mix, w_in_b, q_g, k_g, emit_kv=True)
    xc = _mixer_out(xc, mod, ctx_mod, p, q, k, v, gp, gn, w_pool_b, s_pool, w_pp_b, w_np_b, w_o_b,
                    seq_len=seq)
    xc = _conv_ffn(xc, mod, ctx_mod, g_ffn, w_up_b, conv_w, conv_b, w_down_b, seq_len=seq)
    y_prompt = xc.reshape(batch, seq, D_MODEL)

    tiles_per_seq = dec_seq // TILE
    lat_mod = lambda i: 1 + i // tiles_per_seq
    cache = (cache_k[:, l].astype(BF16), cache_v[:, l].astype(BF16), _toeplitz_bias(na_rpb[l]))
    xs = x_sample.reshape(dec_batch * dec_seq, D_MODEL)
    p, q, k, v, gp, gn = _mixer_in(xs, mod, lat_mod, g_mix, w_in_b, q_g, k_g, emit_kv=False)
    xs = _mixer_out(xs, mod, lat_mod, p, q, k, v, gp, gn, w_pool_b, s_pool, w_pp_b, w_np_b, w_o_b,
                    seq_len=dec_seq, cache=cache)
    xs = _conv_ffn(xs, mod, lat_mod, g_ffn, w_up_b, conv_w, conv_b, w_down_b,
                   seq_len=dec_seq)
    y_sample = xs.reshape(dec_batch, dec_seq, D_MODEL)
    return (y_prompt, y_sample, new_k, new_v)
```

```python
import functools

import jax
import jax.numpy as jnp
import numpy as np
from jax import lax
from jax.experimental import pallas as pl
from jax.experimental.pallas import tpu as pltpu

D_MODEL = 1024
GRID_W = 64
N_HEADS = 8
HEAD_DIM = 64
D_NA = N_HEADS * HEAD_DIM
D_POOL = 512
POOL_WINDOWS = (2, 4, 8, 16)
POOL_GROUP = D_POOL // len(POOL_WINDOWS)
WIN_H = 8
WIN_W = 16
D_FF = 2816
Q_BLOCK = 128
EPS = 1e-6
NEG_INF = -1e30
OFF_Q = D_POOL
OFF_K = D_POOL + D_NA
OFF_V = D_POOL + 2 * D_NA
OFF_GP = D_POOL + 3 * D_NA
OFF_GN = OFF_GP + D_MODEL
D_IN = OFF_GN + D_MODEL

TILE = 256
TILE_ROWS = TILE // GRID_W
KEY_ROWS = 12
N_LOCAL = KEY_ROWS * GRID_W
POOL_HALO = 128
SUBLANES = 8
LANES = 128
CONV_HALO = SUBLANES
FF_CHUNK = 256
N_FF_CHUNKS = D_FF // FF_CHUNK
PIPE_DEPTH = 3
ATTN_DEPTH = 1
MOD_ROWS = 8
MOD_TILE = 1536
VMEM_LIMIT = 56 * 1024 * 1024

BF16 = jnp.bfloat16
F32 = jnp.float32


def _dot(a, b):
    return jnp.dot(a, b, preferred_element_type=F32)


def _dot_nt(a, b):
    return lax.dot_general(a, b, (((1,), (1,)), ((), ())), preferred_element_type=F32)


def _modulated_norm(x, g, shift, scale):
    ms = jnp.mean(x * x, axis=-1, keepdims=True)
    return (x * lax.rsqrt(ms + EPS) * g) * (1.0 + scale) + shift


def _sigmoid(x):
    return 1.0 / (1.0 + jnp.exp(-x))


def _split_bf16(x):
    hi = x.astype(BF16)
    lo = (x - hi.astype(F32)).astype(BF16)
    return hi, lo


def _mod_kernel(cond_ref, w_ref, b_ref, o_ref):
    cnd = cond_ref[...]
    s = (cnd * _sigmoid(cnd)).astype(BF16)
    o_ref[...] = _dot(s, w_ref[...].astype(BF16)) + b_ref[...]


def _modulation(cond, w_mod, b_mod):
    n = w_mod.shape[1]
    return pl.pallas_call(
        _mod_kernel,
        grid=(n // MOD_TILE,),
        in_specs=[
            pl.BlockSpec((MOD_ROWS, D_MODEL), lambda j: (0, 0)),
            pl.BlockSpec((D_MODEL, MOD_TILE), lambda j: (0, j)),
            pl.BlockSpec((1, MOD_TILE), lambda j: (0, j)),
        ],
        out_specs=pl.BlockSpec((MOD_ROWS, MOD_TILE), lambda j: (0, j)),
        out_shape=jax.ShapeDtypeStruct((MOD_ROWS, n), F32),
        compiler_params=pltpu.CompilerParams(
            dimension_semantics=("arbitrary",), vmem_limit_bytes=VMEM_LIMIT),
        name="modulation",
    )(cond, w_mod, b_mod.reshape(1, n))


def _mixer_in_kernel(x_ref, mod_ref, g_ref, w_ref, qg_ref, kg_ref, *out_refs, emit_kv):
    p_ref, q_ref, k_ref, v_ref, gp_ref, gn_ref = out_refs[:6]
    mod = mod_ref[0]
    h = _modulated_norm(x_ref[...], g_ref[...], mod[:, 0:D_MODEL], mod[:, D_MODEL:2 * D_MODEL])
    h = h.astype(BF16)

    def proj(lo, hi):
        return _dot(h, w_ref[:, lo:hi])

    first_head = lax.broadcasted_iota(jnp.int32, (1, LANES), 1) < HEAD_DIM

    def head_norm(t, g):
        assert 2 * HEAD_DIM == LANES
        tt = t * t
        blocks = []
        for j in range(D_NA // LANES):
            blk = tt[:, j * LANES:(j + 1) * LANES]
            both = jnp.sum(blk, axis=-1, keepdims=True)
            one = jnp.sum(jnp.where(first_head, blk, 0.0), axis=-1, keepdims=True)
            blocks.append(jnp.where(first_head, one, both - one))
        ss = jnp.concatenate(blocks, axis=-1)
        return t * lax.rsqrt(ss * (1.0 / HEAD_DIM) + EPS) * g

    q = proj(OFF_Q, OFF_K)
    k = proj(OFF_K, OFF_V)
    p_ref[...] = proj(0, OFF_Q)
    v = proj(OFF_V, OFF_GP)
    v_ref[...] = v.astype(BF16)
    q = (head_norm(q, qg_ref[...]) * (HEAD_DIM ** -0.5)).astype(BF16)
    gp_ref[...] = proj(OFF_GP, OFF_GN)
    k = head_norm(k, kg_ref[...])
    k_ref[...] = k.astype(BF16)
    gn_ref[...] = proj(OFF_GN, D_IN)
    if emit_kv:
        t = lax.broadcasted_iota(jnp.int32, (TILE, TILE), 0)
        j = lax.broadcasted_iota(jnp.int32, (TILE, TILE), 1)
        n_blocks = TILE // Q_BLOCK
        perm = jnp.where(j == (t % n_blocks) * Q_BLOCK + t // n_blocks, 1.0, 0.0).astype(BF16)
        q = _dot(perm, q).astype(BF16)
    q_ref[...] = q
    if emit_kv:
        nk_ref, nv_ref = out_refs[6:]
        for hd in range(N_HEADS):
            sl = slice(hd * HEAD_DIM, (hd + 1) * HEAD_DIM)
            nk_ref[0, 0, hd] = k[:, sl]
            nv_ref[0, 0, hd] = v[:, sl]


def _mixer_in(x, mod, mod_index, norm_g, w_in, q_g, k_g, *, emit_kv):
    n_tok = x.shape[0]
    n_tiles = n_tok // TILE
    tok = lambda i: (i, 0)
    const = lambda i: (0, 0)
    out_shape = [
        jax.ShapeDtypeStruct((n_tok, D_POOL), F32),
        jax.ShapeDtypeStruct((n_tok, D_NA), BF16),
        jax.ShapeDtypeStruct((n_tok, D_NA), BF16),
        jax.ShapeDtypeStruct((n_tok, D_NA), BF16),
        jax.ShapeDtypeStruct((n_tok, D_MODEL), F32),
        jax.ShapeDtypeStruct((n_tok, D_MODEL), F32),
    ]
    out_specs = [
        pl.BlockSpec((TILE, D_POOL), tok),
        pl.BlockSpec((TILE, D_NA), tok),
        pl.BlockSpec((TILE, D_NA), tok),
        pl.BlockSpec((TILE, D_NA), tok),
        pl.BlockSpec((TILE, D_MODEL), tok),
        pl.BlockSpec((TILE, D_MODEL), tok),
    ]
    if emit_kv:
        kv_shape = jax.ShapeDtypeStruct((n_tiles, 1, N_HEADS, TILE, HEAD_DIM), F32)
        kv_spec = pl.BlockSpec((1, 1, N_HEADS, TILE, HEAD_DIM), lambda i: (i, 0, 0, 0, 0))
        out_shape += [kv_shape, kv_shape]
        out_specs += [kv_spec, kv_spec]
    return pl.pallas_call(
        functools.partial(_mixer_in_kernel, emit_kv=emit_kv),
        grid=(n_tiles,),
        in_specs=[
            pl.BlockSpec((TILE, D_MODEL), tok),
            pl.BlockSpec((1, 1, 6 * D_MODEL), lambda i: (mod_index(i), 0, 0)),
            pl.BlockSpec((1, D_MODEL), const),
            pl.BlockSpec((D_MODEL, D_IN), const),
            pl.BlockSpec((1, D_NA), const),
            pl.BlockSpec((1, D_NA), const),
        ],
        out_specs=out_specs,
        out_shape=out_shape,
        compiler_params=pltpu.CompilerParams(
            dimension_semantics=("arbitrary",), vmem_limit_bytes=VMEM_LIMIT),
        name="mixer_in_ctx" if emit_kv else "mixer_in_lat",
    )(x, mod, norm_g, w_in, q_g, k_g)


def _pool_band(window, n_prev):
    n_k = TILE + 2 * n_prev
    t = lax.broadcasted_iota(jnp.int32, (TILE, n_k), 0)
    j = lax.broadcasted_iota(jnp.int32, (TILE, n_k), 1) - n_prev
    lo = t - window // 2
    return jnp.where((j >= lo) & (j < lo + window), 1.0, 0.0).astype(BF16)


def _key_row0(tile, n_rows):
    return jnp.clip(tile * TILE_ROWS - WIN_H // 2, 0, n_rows - KEY_ROWS)


def _fill_local_bias(bias_ref, toep_ref, tile, n_rows):
    key_row0 = int(np.clip(tile * TILE_ROWS - WIN_H // 2, 0, n_rows - KEY_ROWS))
    for a in range(TILE_ROWS):
        r = tile * TILE_ROWS + a
        row_start = int(np.clip(r - WIN_H // 2, 0, n_rows - WIN_H))
        for b in range(KEY_ROWS):
            kr = key_row0 + b
            dr = kr - r + (WIN_H - 1) if row_start <= kr < row_start + WIN_H else 2 * WIN_H - 1
            half = slice((b % 2) * GRID_W, (b % 2 + 1) * GRID_W)
            bias_ref[:, a * GRID_W:(a + 1) * GRID_W, b * GRID_W:(b + 1) * GRID_W] = (
                toep_ref[:, dr, :, half])


def _mixer_out_kernel(*refs, latent, tiles_per_seq):
    if latent:
        (x_ref, mod_ref, p_ref, pprev_ref, pnext_ref, q_ref, k_ref, v_ref, gp_ref, gn_ref,
         ck_ref, cv_ref, toep_ref, wpool_ref, spool_ref, wpp_ref, wnp_ref, wo_ref,
         o_ref, na_ref, bias_ref) = refs
    else:
        (x_ref, mod_ref, p_ref, q_ref, k_ref, v_ref, gp_ref, gn_ref,
         wpool_ref, spool_ref, wpp_ref, wnp_ref, wo_ref, o_ref, na_ref) = refs
    i = pl.program_id(0)
    t_in_seq = i % tiles_per_seq
    seq_len = TILE * tiles_per_seq

    p = p_ref[...]
    pos = t_in_seq * TILE + lax.broadcasted_iota(jnp.int32, (TILE, 1), 0)
    if latent:
        prev_ok = (t_in_seq > 0).astype(F32)
        next_ok = (t_in_seq < tiles_per_seq - 1).astype(F32)
        p_ext = jnp.concatenate([pprev_ref[...] * prev_ok, p, pnext_ref[...] * next_ok], axis=0)
        n_prev = POOL_HALO
    else:
        p_ext = p
        n_prev = 0
    p_hi, p_lo = _split_bf16(p_ext)
    mixed = []
    for gi, w in enumerate(POOL_WINDOWS):
        gs = slice(gi * POOL_GROUP, (gi + 1) * POOL_GROUP)
        band = _pool_band(w, n_prev)
        sums = _dot(band, p_hi[:, gs]) + _dot(band, p_lo[:, gs])
        lo = jnp.maximum(pos - w // 2, 0)
        hi = jnp.minimum(pos + (w - w // 2), seq_len)
        pooled = sums / (hi - lo).astype(F32) - p[:, gs]
        mixed.append(_dot(pooled.astype(BF16), wpool_ref[gi]))
    pool_out = jnp.concatenate(mixed, axis=-1) * spool_ref[...]
    gated_pool = _sigmoid(gp_ref[...]) * _dot(pool_out.astype(BF16), wpp_ref[...])

    if latent:
        n_rows = seq_len // GRID_W
        key0 = pl.multiple_of(_key_row0(t_in_seq, n_rows) * GRID_W, TILE)
        for tile in (0, 1, tiles_per_seq - 1):
            pl.when(t_in_seq == tile)(
                functools.partial(_fill_local_bias, bias_ref, toep_ref, tile, n_rows))

    assert 2 * HEAD_DIM == LANES
    lane = lax.broadcasted_iota(jnp.int32, (1, LANES), 1)
    only = [jnp.where(lane < HEAD_DIM, 1.0, 0.0).astype(BF16),
            jnp.where(lane < HEAD_DIM, 0.0, 1.0).astype(BF16)]

    def pair(j):
        return slice(j * LANES, (j + 1) * LANES)

    def scores(j):
        qp = q_ref[:, pair(j)]
        out = []
        for half in range(2):
            qh = qp * only[half]
            if latent:
                s_loc = (_dot_nt(qh, k_ref[0, pl.ds(key0, N_LOCAL), pair(j)])
                         + bias_ref[2 * j + half])
                out.append((s_loc, _dot_nt(qh, ck_ref[0, j])))
            else:
                out.append((_dot_nt(qh, k_ref[0, :, pair(j)]),))
        return out

    def attend(j, halves):
        if latent:
            vals = (v_ref[0, pl.ds(key0, N_LOCAL), pair(j)], cv_ref[0, j])
        else:
            vals = (v_ref[0, :, pair(j)],)
        o = None
        for half, parts in enumerate(halves):
            m = functools.reduce(jnp.maximum, [jnp.max(s, axis=-1, keepdims=True) for s in parts])
            es = [jnp.exp(s - m) for s in parts]
            denom = sum(jnp.sum(e, axis=-1, keepdims=True) for e in es)
            oh = sum(_dot(e.astype(BF16), v * only[half]) for e, v in zip(es, vals)) / denom
            o = oh if o is None else o + oh
        na_ref[:, pair(j)] = o.astype(BF16)

    n_pairs = N_HEADS // 2
    pending = {j: scores(j) for j in range(min(ATTN_DEPTH, n_pairs))}
    for j in range(n_pairs):
        if j + ATTN_DEPTH < n_pairs:
            pending[j + ATTN_DEPTH] = scores(j + ATTN_DEPTH)
        attend(j, pending.pop(j))

    b = _dot(na_ref[...], wnp_ref[...])
    merged = gated_pool + _sigmoid(gn_ref[...]) * b
    y = _dot(merged.astype(BF16), wo_ref[...])
    gate = mod_ref[0][:, 2 * D_MODEL:3 * D_MODEL]
    o_ref[...] = x_ref[...] + gate * y


def _mixer_out(x, mod, mod_index, p, q, k, v, gp, gn, w_pool, s_pool, w_pp, w_np, w_o,
               *, seq_len, cache=None):
    n_tok = x.shape[0]
    n_tiles = n_tok // TILE
    tiles_per_seq = seq_len // TILE
    latent = cache is not None
    tok = lambda i: (i, 0)
    const2 = lambda i: (0, 0)
    seq = lambda i: (i // tiles_per_seq, 0, 0)
    k = k.reshape(n_tok // seq_len, seq_len, D_NA)
    v = v.reshape(n_tok // seq_len, seq_len, D_NA)

    in_specs = [
        pl.BlockSpec((TILE, D_MODEL), tok),
        pl.BlockSpec((1, 1, 6 * D_MODEL), lambda i: (mod_index(i), 0, 0)),
        pl.BlockSpec((TILE, D_POOL), tok),
    ]
    args = [x, mod, p]
    scratch_shapes = [pltpu.VMEM((TILE, D_NA), BF16)]
    if latent:
        hb = TILE // POOL_HALO
        n_hb = n_tok // POOL_HALO
        in_specs += [
            pl.BlockSpec((POOL_HALO, D_POOL), lambda i: (jnp.maximum(i * hb - 1, 0), 0)),
            pl.BlockSpec((POOL_HALO, D_POOL), lambda i: (jnp.minimum((i + 1) * hb, n_hb - 1), 0)),
        ]
        args += [p, p]
    in_specs += [
        pl.BlockSpec((TILE, D_NA), tok),
        pl.BlockSpec((1, seq_len, D_NA), seq),
        pl.BlockSpec((1, seq_len, D_NA), seq),
        pl.BlockSpec((TILE, D_MODEL), tok),
        pl.BlockSpec((TILE, D_MODEL), tok),
    ]
    args += [q, k, v, gp, gn]
    if latent:
        cache_k, cache_v, toeplitz = cache
        in_specs += [
            pl.BlockSpec((1,) + cache_k.shape[1:], lambda i: (i // tiles_per_seq, 0, 0, 0)),
            pl.BlockSpec((1,) + cache_v.shape[1:], lambda i: (i // tiles_per_seq, 0, 0, 0)),
            pl.BlockSpec(toeplitz.shape, lambda i: (0, 0, 0, 0)),
        ]
        args += [cache_k, cache_v, toeplitz]
        scratch_shapes.append(pltpu.VMEM((N_HEADS, TILE, N_LOCAL), F32))
        assert tiles_per_seq >= 3 and seq_len // GRID_W >= KEY_ROWS
    in_specs += [
        pl.BlockSpec((len(POOL_WINDOWS), POOL_GROUP, POOL_GROUP), lambda i: (0, 0, 0)),
        pl.BlockSpec((1, D_POOL), const2),
        pl.BlockSpec((D_POOL, D_MODEL), const2),
        pl.BlockSpec((D_NA, D_MODEL), const2),
        pl.BlockSpec((D_MODEL, D_MODEL), const2),
    ]
    args += [w_pool, s_pool, w_pp, w_np, w_o]
    return pl.pallas_call(
        functools.partial(_mixer_out_kernel, latent=latent, tiles_per_seq=tiles_per_seq),
        grid=(n_tiles,),
        in_specs=in_specs,
        out_specs=pl.BlockSpec((TILE, D_MODEL), tok),
        out_shape=jax.ShapeDtypeStruct((n_tok, D_MODEL), F32),
        scratch_shapes=scratch_shapes,
        compiler_params=pltpu.CompilerParams(
            dimension_semantics=("arbitrary",), vmem_limit_bytes=VMEM_LIMIT),
        name="mixer_out_lat" if latent else "mixer_out_ctx",
    )(*args)


def _conv_ffn_kernel(*refs, halo, tiles_per_seq):
    if halo:
        (x_ref, xprev_ref, xnext_ref, mod_ref, g_ref, wup_ref, cw_ref, cb_ref, wdn_ref,
         o_ref, h_ref, u_ref, acc_ref) = refs
    else:
        (x_ref, mod_ref, g_ref, wup_ref, cw_ref, cb_ref, wdn_ref,
         o_ref, h_ref, u_ref, acc_ref) = refs
    i = pl.program_id(0)
    t_in_seq = i % tiles_per_seq
    mod = mod_ref[0]
    shift = mod[:, 3 * D_MODEL:4 * D_MODEL]
    scale = mod[:, 4 * D_MODEL:5 * D_MODEL]

    if halo:
        x_ext = jnp.concatenate([xprev_ref[...], x_ref[...], xnext_ref[...]], axis=0)
        pad = CONV_HALO
    else:
        x_ext = x_ref[...]
        pad = 0
    n_ext = TILE + 2 * pad
    h_ref[...] = _modulated_norm(x_ext, g_ref[...], shift, scale).astype(BF16)

    row = lax.broadcasted_iota(jnp.int32, (SUBLANES, 1), 0)
    keep_first = jnp.where((row == 0) & (t_in_seq == 0), 0.0, 1.0)
    keep_last = jnp.where((row == SUBLANES - 1) & (t_in_seq == tiles_per_seq - 1), 0.0, 1.0)

    def cols(c, part):
        lo = part * D_FF + c * FF_CHUNK
        return slice(lo, lo + FF_CHUNK)

    def up_proj(c):
        for part in range(2):
            u_ref[c, part] = _dot(h_ref[...], wup_ref[:, cols(c, part)])

    def conv(c, part):
        u = u_ref[c, part]
        sl = cols(c, part)
        u_prev = pltpu.roll(u, 1, axis=0)[pad:pad + TILE]
        u_prev = jnp.concatenate([u_prev[:SUBLANES] * keep_first, u_prev[SUBLANES:]], axis=0)
        u_next = pltpu.roll(u, n_ext - 1, axis=0)[pad:pad + TILE]
        u_next = jnp.concatenate([u_next[:-SUBLANES], u_next[-SUBLANES:] * keep_last], axis=0)
        return (u_prev * cw_ref[0:1, sl] + u[pad:pad + TILE] * cw_ref[1:2, sl]
                + u_next * cw_ref[2:3, sl] + cb_ref[:, sl])

    def gated_down(c):
        a = conv(c, 0)
        act = (a * _sigmoid(a) * conv(c, 1)).astype(BF16)
        return _dot(act, wdn_ref[c * FF_CHUNK:(c + 1) * FF_CHUNK, :])

    for c in range(min(PIPE_DEPTH, N_FF_CHUNKS)):
        up_proj(c)
    for c in range(N_FF_CHUNKS):
        if c + PIPE_DEPTH < N_FF_CHUNKS:
            up_proj(c + PIPE_DEPTH)
        part = gated_down(c)
        if c == 0:
            acc_ref[...] = part
        elif c + 1 < N_FF_CHUNKS:
            acc_ref[...] += part
        else:
            o_ref[...] = x_ref[...] + mod[:, 5 * D_MODEL:6 * D_MODEL] * (acc_ref[...] + part)


def _conv_ffn(x, mod, mod_index, norm_g, w_up, conv_w, conv_b, w_down, *, seq_len):
    n_tok = x.shape[0]
    n_tiles = n_tok // TILE
    tiles_per_seq = seq_len // TILE
    halo = tiles_per_seq > 1
    n_ext = TILE + 2 * CONV_HALO if halo else TILE
    tok = lambda i: (i, 0)
    const2 = lambda i: (0, 0)
    in_specs = [pl.BlockSpec((TILE, D_MODEL), tok)]
    args = [x]
    if halo:
        hb = TILE // CONV_HALO
        n_hb = n_tok // CONV_HALO
        in_specs += [
            pl.BlockSpec((CONV_HALO, D_MODEL), lambda i: (jnp.maximum(i * hb - 1, 0), 0)),
            pl.BlockSpec((CONV_HALO, D_MODEL), lambda i: (jnp.minimum((i + 1) * hb, n_hb - 1), 0)),
        ]
        args += [x, x]
    in_specs += [
        pl.BlockSpec((1, 1, 6 * D_MODEL), lambda i: (mod_index(i), 0, 0)),
        pl.BlockSpec((1, D_MODEL), const2),
        pl.BlockSpec((D_MODEL, 2 * D_FF), const2),
        pl.BlockSpec((3, 2 * D_FF), const2),
        pl.BlockSpec((1, 2 * D_FF), const2),
        pl.BlockSpec((D_FF, D_MODEL), const2),
    ]
    args += [mod, norm_g, w_up, conv_w, conv_b, w_down]
    return pl.pallas_call(
        functools.partial(_conv_ffn_kernel, halo=halo, tiles_per_seq=tiles_per_seq),
        grid=(n_tiles,),
        in_specs=in_specs,
        out_specs=pl.BlockSpec((TILE, D_MODEL), tok),
        out_shape=jax.ShapeDtypeStruct((n_tok, D_MODEL), F32),
        scratch_shapes=[
            pltpu.VMEM((n_ext, D_MODEL), BF16),
            pltpu.VMEM((N_FF_CHUNKS, 2, n_ext, FF_CHUNK), F32),
            pltpu.VMEM((TILE, D_MODEL), F32),
        ],
        compiler_params=pltpu.CompilerParams(
            dimension_semantics=("arbitrary",), vmem_limit_bytes=VMEM_LIMIT),
        name="conv_ffn_lat" if halo else "conv_ffn_ctx",
    )(*args)


def _toeplitz_bias(rpb):
    n_dc = 2 * WIN_W - 1
    col = np.arange(GRID_W)
    dc = col[None, :] - col[:, None] + (WIN_W - 1)
    win_start = np.clip(col - WIN_W // 2, 0, GRID_W - WIN_W)
    col_ok = (col[None, :] >= win_start[:, None]) & (col[None, :] < win_start[:, None] + WIN_W)
    col_sel = (dc[None] == np.arange(n_dc)[:, None, None]).astype(np.float32)
    blocks = jnp.einsum("hdc,cqk->hdqk", rpb, col_sel, precision=lax.Precision.HIGHEST)
    blocks = jnp.where(col_ok[None, None], blocks, NEG_INF)
    blocks = jnp.pad(blocks, ((0, 0), (0, 1), (0, 0), (0, 0)), constant_values=NEG_INF)
    return jnp.concatenate([blocks, blocks], axis=-1)


def _pair_heads(t):
    b, _, n, _ = t.shape
    t = t.astype(BF16).reshape(b, N_HEADS // 2, 2, n, HEAD_DIM)
    return t.transpose(0, 1, 3, 2, 4).reshape(b, N_HEADS // 2, n, 2 * HEAD_DIM)


def kernel(x_prompt, x_sample, cache_k, cache_v, c, c_ctx, norm_mix_g, norm_ffn_g, w_mod, b_mod,
           w_in, q_norm_g, k_norm_g, pool_w, pool_scale, na_rpb, w_pool_proj, w_na_proj, w_o,
           w_up, ffn_conv_w, ffn_conv_b, w_down):
    depth = w_in.shape[0]
    assert depth == 1
    batch, seq, _ = x_prompt.shape
    dec_batch, dec_seq, _ = x_sample.shape
    assert seq == TILE and dec_seq % TILE == 0 and dec_batch + 1 <= MOD_ROWS
    l = 0

    cond = jnp.zeros((MOD_ROWS, D_MODEL), F32).at[0].set(c_ctx).at[1:1 + dec_batch].set(c)
    mod = _modulation(cond, w_mod[l], b_mod[l]).reshape(MOD_ROWS, 1, 6 * D_MODEL)

    w_in_b = w_in[l].astype(BF16)
    w_pool_b = pool_w[l].astype(BF16)
    w_pp_b = w_pool_proj[l].astype(BF16)
    w_np_b = w_na_proj[l].astype(BF16)
    w_o_b = w_o[l].astype(BF16)
    w_up_b = w_up[l].astype(BF16)
    w_down_b = w_down[l].astype(BF16)
    conv_w = ffn_conv_w[l]
    g_mix = norm_mix_g[l].reshape(1, D_MODEL)
    g_ffn = norm_ffn_g[l].reshape(1, D_MODEL)
    q_g = jnp.tile(q_norm_g[l], N_HEADS).reshape(1, D_NA)
    k_g = jnp.tile(k_norm_g[l], N_HEADS).reshape(1, D_NA)
    s_pool = pool_scale[l].reshape(1, D_POOL)
    conv_b = ffn_conv_b[l].reshape(1, 2 * D_FF)

    ctx_mod = lambda i: 0
    xc = x_prompt.reshape(batch * seq, D_MODEL)
    p, q, k, v, gp, gn, new_k, new_v = _mixer_in(
        xc, mod, ctx_mod, g_mix, w_in_b, q_g, k_g, emit_kv=True)
    xc = _mixer_out(xc, mod, ctx_mod, p, q, k, v, gp, gn, w_pool_b, s_pool, w_pp_b, w_np_b, w_o_b,
                    seq_len=seq)
    xc = _conv_ffn(xc, mod, ctx_mod, g_ffn, w_up_b, conv_w, conv_b, w_down_b, seq_len=seq)
    y_prompt = xc.reshape(batch, seq, D_MODEL)

    tiles_per_seq = dec_seq // TILE
    lat_mod = lambda i: 1 + i // tiles_per_seq
    cache = (_pair_heads(cache_k[:, l]), _pair_heads(cache_v[:, l]), _toeplitz_bias(na_rpb[l]))
    xs = x_sample.reshape(dec_batch * dec_seq, D_MODEL)
    p, q, k, v, gp, gn = _mixer_in(xs, mod, lat_mod, g_mix, w_in_b, q_g, k_g, emit_kv=False)
    xs = _mixer_out(xs, mod, lat_mod, p, q, k, v, gp, gn, w_pool_b, s_pool, w_pp_b, w_np_b, w_o_b,
                    seq_len=dec_seq, cache=cache)
    xs = _conv_ffn(xs, mod, lat_mod, g_ffn, w_up_b, conv_w, conv_b, w_down_b,
                   seq_len=dec_seq)
    y_sample = xs.reshape(dec_batch, dec_seq, D_MODEL)
    return (y_prompt, y_sample, new_k, new_v)
```

```python
import functools

import jax
import jax.numpy as jnp
import numpy as np
from jax import lax
from jax.experimental import pallas as pl
from jax.experimental.pallas import tpu as pltpu

D_MODEL = 1024
GRID_W = 64
N_HEADS = 8
HEAD_DIM = 64
D_NA = N_HEADS * HEAD_DIM
D_POOL = 512
POOL_WINDOWS = (2, 4, 8, 16)
POOL_GROUP = D_POOL // len(POOL_WINDOWS)
WIN_H = 8
WIN_W = 16
D_FF = 2816
Q_BLOCK = 128
EPS = 1e-6
NEG_INF = -1e30
OFF_Q = D_POOL
OFF_K = D_POOL + D_NA
OFF_V = D_POOL + 2 * D_NA
OFF_GP = D_POOL + 3 * D_NA
OFF_GN = OFF_GP + D_MODEL
D_IN = OFF_GN + D_MODEL

TILE = 256
TILE_ROWS = TILE // GRID_W
KEY_ROWS = 12
N_LOCAL = KEY_ROWS * GRID_W
POOL_HALO = 128
SUBLANES = 8
LANES = 128
CONV_HALO = SUBLANES
FF_CHUNK = 256
N_FF_CHUNKS = D_FF // FF_CHUNK
PIPE_DEPTH = 3
ATTN_DEPTH = 1
MOD_ROWS = 8
MOD_TILE = 1536
VMEM_LIMIT = 56 * 1024 * 1024

BF16 = jnp.bfloat16
F32 = jnp.float32


def _dot(a, b):
    return jnp.dot(a, b, preferred_element_type=F32)


def _dot_nt(a, b):
    return lax.dot_general(a, b, (((1,), (1,)), ((), ())), preferred_element_type=F32)


def _modulated_norm(x, g, shift, scale):
    ms = jnp.mean(x * x, axis=-1, keepdims=True)
    return (x * lax.rsqrt(ms + EPS) * g) * (1.0 + scale) + shift


def _sigmoid(x):
    return 1.0 / (1.0 + jnp.exp(-x))


def _split_bf16(x):
    hi = x.astype(BF16)
    lo = (x - hi.astype(F32)).astype(BF16)
    return hi, lo


def _mod_kernel(cond_ref, w_ref, b_ref, o_ref):
    cnd = cond_ref[...]
    s = (cnd * _sigmoid(cnd)).astype(BF16)
    o_ref[...] = _dot(s, w_ref[...].astype(BF16)) + b_ref[...]


def _modulation(cond, w_mod, b_mod):
    n = w_mod.shape[1]
    return pl.pallas_call(
        _mod_kernel,
        grid=(n // MOD_TILE,),
        in_specs=[
            pl.BlockSpec((MOD_ROWS, D_MODEL), lambda j: (0, 0)),
            pl.BlockSpec((D_MODEL, MOD_TILE), lambda j: (0, j)),
            pl.BlockSpec((1, MOD_TILE), lambda j: (0, j)),
        ],
        out_specs=pl.BlockSpec((MOD_ROWS, MOD_TILE), lambda j: (0, j)),
        out_shape=jax.ShapeDtypeStruct((MOD_ROWS, n), F32),
        compiler_params=pltpu.CompilerParams(
            dimension_semantics=("arbitrary",), vmem_limit_bytes=VMEM_LIMIT),
        name="modulation",
    )(cond, w_mod, b_mod.reshape(1, n))


def _mixer_in_kernel(x_ref, mod_ref, g_ref, w_ref, qg_ref, kg_ref, *out_refs, emit_kv):
    p_ref, q_ref, k_ref, v_ref, gp_ref, gn_ref = out_refs[:6]
    mod = mod_ref[0]
    h = _modulated_norm(x_ref[...], g_ref[...], mod[:, 0:D_MODEL], mod[:, D_MODEL:2 * D_MODEL])
    h = h.astype(BF16)

    def proj(lo, hi):
        return _dot(h, w_ref[:, lo:hi])

    first_head = lax.broadcasted_iota(jnp.int32, (1, LANES), 1) < HEAD_DIM

    def head_norm(t, g):
        assert 2 * HEAD_DIM == LANES
        tt = t * t
        blocks = []
        for j in range(D_NA // LANES):
            blk = tt[:, j * LANES:(j + 1) * LANES]
            both = jnp.sum(blk, axis=-1, keepdims=True)
            one = jnp.sum(jnp.where(first_head, blk, 0.0), axis=-1, keepdims=True)
            blocks.append(jnp.where(first_head, one, both - one))
        ss = jnp.concatenate(blocks, axis=-1)
        return t * lax.rsqrt(ss * (1.0 / HEAD_DIM) + EPS) * g

    q = proj(OFF_Q, OFF_K)
    k = proj(OFF_K, OFF_V)
    p_ref[...] = proj(0, OFF_Q)
    v = proj(OFF_V, OFF_GP)
    v_ref[...] = v.astype(BF16)
    q = (head_norm(q, qg_ref[...]) * (HEAD_DIM ** -0.5)).astype(BF16)
    gp_ref[...] = proj(OFF_GP, OFF_GN)
    k = head_norm(k, kg_ref[...])
    k_ref[...] = k.astype(BF16)
    gn_ref[...] = proj(OFF_GN, D_IN)
    if emit_kv:
        t = lax.broadcasted_iota(jnp.int32, (TILE, TILE), 0)
        j = lax.broadcasted_iota(jnp.int32, (TILE, TILE), 1)
        n_blocks = TILE // Q_BLOCK
        perm = jnp.where(j == (t % n_blocks) * Q_BLOCK + t // n_blocks, 1.0, 0.0).astype(BF16)
        q = _dot(perm, q).astype(BF16)
    q_ref[...] = q
    if emit_kv:
        nk_ref, nv_ref = out_refs[6:]
        for hd in range(N_HEADS):
            sl = slice(hd * HEAD_DIM, (hd + 1) * HEAD_DIM)
            nk_ref[0, 0, hd] = k[:, sl]
            nv_ref[0, 0, hd] = v[:, sl]


def _mixer_in(x, mod, mod_index, norm_g, w_in, q_g, k_g, *, emit_kv):
    n_tok = x.shape[0]
    n_tiles = n_tok // TILE
    tok = lambda i: (i, 0)
    const = lambda i: (0, 0)
    out_shape = [
        jax.ShapeDtypeStruct((n_tok, D_POOL), F32),
        jax.ShapeDtypeStruct((n_tok, D_NA), BF16),
        jax.ShapeDtypeStruct((n_tok, D_NA), BF16),
        jax.ShapeDtypeStruct((n_tok, D_NA), BF16),
        jax.ShapeDtypeStruct((n_tok, D_MODEL), F32),
        jax.ShapeDtypeStruct((n_tok, D_MODEL), F32),
    ]
    out_specs = [
        pl.BlockSpec((TILE, D_POOL), tok),
        pl.BlockSpec((TILE, D_NA), tok),
        pl.BlockSpec((TILE, D_NA), tok),
        pl.BlockSpec((TILE, D_NA), tok),
        pl.BlockSpec((TILE, D_MODEL), tok),
        pl.BlockSpec((TILE, D_MODEL), tok),
    ]
    if emit_kv:
        kv_shape = jax.ShapeDtypeStruct((n_tiles, 1, N_HEADS, TILE, HEAD_DIM), F32)
        kv_spec = pl.BlockSpec((1, 1, N_HEADS, TILE, HEAD_DIM), lambda i: (i, 0, 0, 0, 0))
        out_shape += [kv_shape, kv_shape]
        out_specs += [kv_spec, kv_spec]
    return pl.pallas_call(
        functools.partial(_mixer_in_kernel, emit_kv=emit_kv),
        grid=(n_tiles,),
        in_specs=[
            pl.BlockSpec((TILE, D_MODEL), tok),
            pl.BlockSpec((1, 1, 6 * D_MODEL), lambda i: (mod_index(i), 0, 0)),
            pl.BlockSpec((1, D_MODEL), const),
            pl.BlockSpec((D_MODEL, D_IN), const),
            pl.BlockSpec((1, D_NA), const),
            pl.BlockSpec((1, D_NA), const),
        ],
        out_specs=out_specs,
        out_shape=out_shape,
        compiler_params=pltpu.CompilerParams(
            dimension_semantics=("arbitrary",), vmem_limit_bytes=VMEM_LIMIT),
        name="mixer_in_ctx" if emit_kv else "mixer_in_lat",
    )(x, mod, norm_g, w_in, q_g, k_g)


def _pool_band(window, n_prev):
    n_k = TILE + 2 * n_prev
    t = lax.broadcasted_iota(jnp.int32, (TILE, n_k), 0)
    j = lax.broadcasted_iota(jnp.int32, (TILE, n_k), 1) - n_prev
    lo = t - window // 2
    return jnp.where((j >= lo) & (j < lo + window), 1.0, 0.0).astype(BF16)


def _key_row0(tile, n_rows):
    return jnp.clip(tile * TILE_ROWS - WIN_H // 2, 0, n_rows - KEY_ROWS)


def _fill_local_bias(bias_ref, toep_ref, tile, n_rows):
    key_row0 = int(np.clip(tile * TILE_ROWS - WIN_H // 2, 0, n_rows - KEY_ROWS))
    for a in range(TILE_ROWS):
        r = tile * TILE_ROWS + a
        row_start = int(np.clip(r - WIN_H // 2, 0, n_rows - WIN_H))
        for b in range(KEY_ROWS):
            kr = key_row0 + b
            dr = kr - r + (WIN_H - 1) if row_start <= kr < row_start + WIN_H else 2 * WIN_H - 1
            half = slice((b % 2) * GRID_W, (b % 2 + 1) * GRID_W)
            bias_ref[:, a * GRID_W:(a + 1) * GRID_W, b * GRID_W:(b + 1) * GRID_W] = (
                toep_ref[:, dr, :, half])


def _mixer_out_kernel(*refs, latent, tiles_per_seq):
    if latent:
        (x_ref, mod_ref, p_ref, pprev_ref, pnext_ref, q_ref, k_ref, v_ref, gp_ref, gn_ref,
         ck_ref, cv_ref, toep_ref, wpool_ref, spool_ref, wpp_ref, wnp_ref, wo_ref,
         o_ref, na_ref, bias_ref) = refs
    else:
        (x_ref, mod_ref, p_ref, q_ref, k_ref, v_ref, gp_ref, gn_ref,
         wpool_ref, spool_ref, wpp_ref, wnp_ref, wo_ref, o_ref, na_ref) = refs
    i = pl.program_id(0)
    t_in_seq = i % tiles_per_seq
    seq_len = TILE * tiles_per_seq

    p = p_ref[...]
    pos = t_in_seq * TILE + lax.broadcasted_iota(jnp.int32, (TILE, 1), 0)
    if latent:
        prev_ok = (t_in_seq > 0).astype(F32)
        next_ok = (t_in_seq < tiles_per_seq - 1).astype(F32)
        p_ext = jnp.concatenate([pprev_ref[...] * prev_ok, p, pnext_ref[...] * next_ok], axis=0)
        n_prev = POOL_HALO
    else:
        p_ext = p
        n_prev = 0
    p_hi, p_lo = _split_bf16(p_ext)
    mixed = []
    for gi, w in enumerate(POOL_WINDOWS):
        gs = slice(gi * POOL_GROUP, (gi + 1) * POOL_GROUP)
        band = _pool_band(w, n_prev)
        sums = _dot(band, p_hi[:, gs]) + _dot(band, p_lo[:, gs])
        lo = jnp.maximum(pos - w // 2, 0)
        hi = jnp.minimum(pos + (w - w // 2), seq_len)
        pooled = sums / (hi - lo).astype(F32) - p[:, gs]
        mixed.append(_dot(pooled.astype(BF16), wpool_ref[gi]))
    pool_out = jnp.concatenate(mixed, axis=-1) * spool_ref[...]
    gated_pool = _sigmoid(gp_ref[...]) * _dot(pool_out.astype(BF16), wpp_ref[...])

    if latent:
        n_rows = seq_len // GRID_W
        key0 = pl.multiple_of(_key_row0(t_in_seq, n_rows) * GRID_W, TILE)
        for tile in (0, 1, tiles_per_seq - 1):
            pl.when(t_in_seq == tile)(
                functools.partial(_fill_local_bias, bias_ref, toep_ref, tile, n_rows))

    assert 2 * HEAD_DIM == LANES
    lane = lax.broadcasted_iota(jnp.int32, (1, LANES), 1)
    only = [jnp.where(lane < HEAD_DIM, 1.0, 0.0).astype(BF16),
            jnp.where(lane < HEAD_DIM, 0.0, 1.0).astype(BF16)]

    def pair(j):
        return slice(j * LANES, (j + 1) * LANES)

    def scores(j):
        qp = q_ref[:, pair(j)]
        out = []
        for half in range(2):
            qh = qp * only[half]
            if latent:
                s_loc = (_dot_nt(qh, k_ref[0, pl.ds(key0, N_LOCAL), pair(j)])
                         + bias_ref[2 * j + half])
                out.append((s_loc, _dot_nt(qh, ck_ref[0, j])))
            else:
                out.append((_dot_nt(qh, k_ref[0, :, pair(j)]),))
        return out

    def attend(j, halves):
        if latent:
            vals = (v_ref[0, pl.ds(key0, N_LOCAL), pair(j)], cv_ref[0, j])
        else:
            vals = (v_ref[0, :, pair(j)],)
        o = None
        for half, parts in enumerate(halves):
            m = functools.reduce(jnp.maximum, [jnp.max(s, axis=-1, keepdims=True) for s in parts])
            es = [jnp.exp(s - m) for s in parts]
            denom = sum(jnp.sum(e, axis=-1, keepdims=True) for e in es)
            oh = sum(_dot(e.astype(BF16), v * only[half]) for e, v in zip(es, vals)) / denom
            o = oh if o is None else o + oh
        na_ref[:, pair(j)] = o.astype(BF16)

    n_pairs = N_HEADS // 2
    pending = {j: scores(j) for j in range(min(ATTN_DEPTH, n_pairs))}
    for j in range(n_pairs):
        if j + ATTN_DEPTH < n_pairs:
            pending[j + ATTN_DEPTH] = scores(j + ATTN_DEPTH)
        attend(j, pending.pop(j))

    b = _dot(na_ref[...], wnp_ref[...])
    merged = gated_pool + _sigmoid(gn_ref[...]) * b
    y = _dot(merged.astype(BF16), wo_ref[...])
    gate = mod_ref[0][:, 2 * D_MODEL:3 * D_MODEL]
    o_ref[...] = x_ref[...] + gate * y


def _mixer_out(x, mod, mod_index, p, q, k, v, gp, gn, w_pool, s_pool, w_pp, w_np, w_o,
               *, seq_len, cache=None):
    n_tok = x.shape[0]
    n_tiles = n_tok // TILE
    tiles_per_seq = seq_len // TILE
    latent = cache is not None
    tok = lambda i: (i, 0)
    const2 = lambda i: (0, 0)
    seq = lambda i: (i // tiles_per_seq, 0, 0)
    k = k.reshape(n_tok // seq_len, seq_len, D_NA)
    v = v.reshape(n_tok // seq_len, seq_len, D_NA)

    in_specs = [
        pl.BlockSpec((TILE, D_MODEL), tok),
        pl.BlockSpec((1, 1, 6 * D_MODEL), lambda i: (mod_index(i), 0, 0)),
        pl.BlockSpec((TILE, D_POOL), tok),
    ]
    args = [x, mod, p]
    scratch_shapes = [pltpu.VMEM((TILE, D_NA), BF16)]
    if latent:
        hb = TILE // POOL_HALO
        n_hb = n_tok // POOL_HALO
        in_specs += [
            pl.BlockSpec((POOL_HALO, D_POOL), lambda i: (jnp.maximum(i * hb - 1, 0), 0)),
            pl.BlockSpec((POOL_HALO, D_POOL), lambda i: (jnp.minimum((i + 1) * hb, n_hb - 1), 0)),
        ]
        args += [p, p]
    in_specs += [
        pl.BlockSpec((TILE, D_NA), tok),
        pl.BlockSpec((1, seq_len, D_NA), seq),
        pl.BlockSpec((1, seq_len, D_NA), seq),
        pl.BlockSpec((TILE, D_MODEL), tok),
        pl.BlockSpec((TILE, D_MODEL), tok),
    ]
    args += [q, k, v, gp, gn]
    if latent:
        cache_k, cache_v, toeplitz = cache
        in_specs += [
            pl.BlockSpec((1,) + cache_k.shape[1:], lambda i: (i // tiles_per_seq, 0, 0, 0)),
            pl.BlockSpec((1,) + cache_v.shape[1:], lambda i: (i // tiles_per_seq, 0, 0, 0)),
            pl.BlockSpec(toeplitz.shape, lambda i: (0, 0, 0, 0)),
        ]
        args += [cache_k, cache_v, toeplitz]
        scratch_shapes.append(pltpu.VMEM((N_HEADS, TILE, N_LOCAL), F32))
        assert tiles_per_seq >= 3 and seq_len // GRID_W >= KEY_ROWS
    in_specs += [
        pl.BlockSpec((len(POOL_WINDOWS), POOL_GROUP, POOL_GROUP), lambda i: (0, 0, 0)),
        pl.BlockSpec((1, D_POOL), const2),
        pl.BlockSpec((D_POOL, D_MODEL), const2),
        pl.BlockSpec((D_NA, D_MODEL), const2),
        pl.BlockSpec((D_MODEL, D_MODEL), const2),
    ]
    args += [w_pool, s_pool, w_pp, w_np, w_o]
    return pl.pallas_call(
        functools.partial(_mixer_out_kernel, latent=latent, tiles_per_seq=tiles_per_seq),
        grid=(n_tiles,),
        in_specs=in_specs,
        out_specs=pl.BlockSpec((TILE, D_MODEL), tok),
        out_shape=jax.ShapeDtypeStruct((n_tok, D_MODEL), F32),
        scratch_shapes=scratch_shapes,
        compiler_params=pltpu.CompilerParams(
            dimension_semantics=("arbitrary",), vmem_limit_bytes=VMEM_LIMIT),
        name="mixer_out_lat" if latent else "mixer_out_ctx",
    )(*args)


def _mixer_seq_kernel(x_ref, mod_ref, g_ref, w_ref, qg_ref, kg_ref,
                      wpool_ref, spool_ref, wpp_ref, wnp_ref, wo_ref,
                      o_ref, nk_ref, nv_ref,
                      p_ref, q_ref, k_ref, v_ref, gp_ref, gn_ref, na_ref):
    _mixer_in_kernel(x_ref, mod_ref, g_ref, w_ref, qg_ref, kg_ref,
                     p_ref, q_ref, k_ref.at[0], v_ref.at[0], gp_ref, gn_ref, nk_ref, nv_ref,
                     emit_kv=True)
    _mixer_out_kernel(x_ref, mod_ref, p_ref, q_ref, k_ref, v_ref, gp_ref, gn_ref,
                      wpool_ref, spool_ref, wpp_ref, wnp_ref, wo_ref, o_ref, na_ref,
                      latent=False, tiles_per_seq=1)


def _mixer_seq(x, mod, mod_index, norm_g, w_in, q_g, k_g, w_pool, s_pool, w_pp, w_np, w_o):
    n_tok = x.shape[0]
    n_tiles = n_tok // TILE
    tok = lambda i: (i, 0)
    const2 = lambda i: (0, 0)
    kv_shape = jax.ShapeDtypeStruct((n_tiles, 1, N_HEADS, TILE, HEAD_DIM), F32)
    kv_spec = pl.BlockSpec((1, 1, N_HEADS, TILE, HEAD_DIM), lambda i: (i, 0, 0, 0, 0))
    return pl.pallas_call(
        _mixer_seq_kernel,
        grid=(n_tiles,),
        in_specs=[
            pl.BlockSpec((TILE, D_MODEL), tok),
            pl.BlockSpec((1, 1, 6 * D_MODEL), lambda i: (mod_index(i), 0, 0)),
            pl.BlockSpec((1, D_MODEL), const2),
            pl.BlockSpec((D_MODEL, D_IN), const2),
            pl.BlockSpec((1, D_NA), const2),
            pl.BlockSpec((1, D_NA), const2),
            pl.BlockSpec((len(POOL_WINDOWS), POOL_GROUP, POOL_GROUP), lambda i: (0, 0, 0)),
            pl.BlockSpec((1, D_POOL), const2),
            pl.BlockSpec((D_POOL, D_MODEL), const2),
            pl.BlockSpec((D_NA, D_MODEL), const2),
            pl.BlockSpec((D_MODEL, D_MODEL), const2),
        ],
        out_specs=[pl.BlockSpec((TILE, D_MODEL), tok), kv_spec, kv_spec],
        out_shape=[jax.ShapeDtypeStruct((n_tok, D_MODEL), F32), kv_shape, kv_shape],
        scratch_shapes=[
            pltpu.VMEM((TILE, D_POOL), F32),
            pltpu.VMEM((TILE, D_NA), BF16),
            pltpu.VMEM((1, TILE, D_NA), BF16),
            pltpu.VMEM((1, TILE, D_NA), BF16),
            pltpu.VMEM((TILE, D_MODEL), F32),
            pltpu.VMEM((TILE, D_MODEL), F32),
            pltpu.VMEM((TILE, D_NA), BF16),
        ],
        compiler_params=pltpu.CompilerParams(
            dimension_semantics=("arbitrary",), vmem_limit_bytes=VMEM_LIMIT),
        name="mixer_seq",
    )(x, mod, norm_g, w_in, q_g, k_g, w_pool, s_pool, w_pp, w_np, w_o)


def _conv_ffn_kernel(*refs, halo, tiles_per_seq):
    if halo:
        (x_ref, xprev_ref, xnext_ref, mod_ref, g_ref, wup_ref, cw_ref, cb_ref, wdn_ref,
         o_ref, h_ref, u_ref, acc_ref) = refs
    else:
        (x_ref, mod_ref, g_ref, wup_ref, cw_ref, cb_ref, wdn_ref,
         o_ref, h_ref, u_ref, acc_ref) = refs
    i = pl.program_id(0)
    t_in_seq = i % tiles_per_seq
    mod = mod_ref[0]
    shift = mod[:, 3 * D_MODEL:4 * D_MODEL]
    scale = mod[:, 4 * D_MODEL:5 * D_MODEL]

    if halo:
        x_ext = jnp.concatenate([xprev_ref[...], x_ref[...], xnext_ref[...]], axis=0)
        pad = CONV_HALO
    else:
        x_ext = x_ref[...]
        pad = 0
    n_ext = TILE + 2 * pad
    h_ref[...] = _modulated_norm(x_ext, g_ref[...], shift, scale).astype(BF16)

    row = lax.broadcasted_iota(jnp.int32, (SUBLANES, 1), 0)
    keep_first = jnp.where((row == 0) & (t_in_seq == 0), 0.0, 1.0)
    keep_last = jnp.where((row == SUBLANES - 1) & (t_in_seq == tiles_per_seq - 1), 0.0, 1.0)

    def cols(c, part):
        lo = part * D_FF + c * FF_CHUNK
        return slice(lo, lo + FF_CHUNK)

    def up_proj(c):
        for part in range(2):
            u_ref[c, part] = _dot(h_ref[...], wup_ref[:, cols(c, part)])

    def conv(c, part):
        u = u_ref[c, part]
        sl = cols(c, part)
        u_prev = pltpu.roll(u, 1, axis=0)[pad:pad + TILE]
        u_prev = jnp.concatenate([u_prev[:SUBLANES] * keep_first, u_prev[SUBLANES:]], axis=0)
        u_next = pltpu.roll(u, n_ext - 1, axis=0)[pad:pad + TILE]
        u_next = jnp.concatenate([u_next[:-SUBLANES], u_next[-SUBLANES:] * keep_last], axis=0)
        return (u_prev * cw_ref[0:1, sl] + u[pad:pad + TILE] * cw_ref[1:2, sl]
                + u_next * cw_ref[2:3, sl] + cb_ref[:, sl])

    def gated_down(c):
        a = conv(c, 0)
        act = (a * _sigmoid(a) * conv(c, 1)).astype(BF16)
        return _dot(act, wdn_ref[c * FF_CHUNK:(c + 1) * FF_CHUNK, :])

    for c in range(min(PIPE_DEPTH, N_FF_CHUNKS)):
        up_proj(c)
    for c in range(N_FF_CHUNKS):
        if c + PIPE_DEPTH < N_FF_CHUNKS:
            up_proj(c + PIPE_DEPTH)
        part = gated_down(c)
        if c == 0:
            acc_ref[...] = part
        elif c + 1 < N_FF_CHUNKS:
            acc_ref[...] += part
        else:
            o_ref[...] = x_ref[...] + mod[:, 5 * D_MODEL:6 * D_MODEL] * (acc_ref[...] + part)


def _conv_ffn(x, mod, mod_index, norm_g, w_up, conv_w, conv_b, w_down, *, seq_len):
    n_tok = x.shape[0]
    n_tiles = n_tok // TILE
    tiles_per_seq = seq_len // TILE
    halo = tiles_per_seq > 1
    n_ext = TILE + 2 * CONV_HALO if halo else TILE
    tok = lambda i: (i, 0)
    const2 = lambda i: (0, 0)
    in_specs = [pl.BlockSpec((TILE, D_MODEL), tok)]
    args = [x]
    if halo:
        hb = TILE // CONV_HALO
        n_hb = n_tok // CONV_HALO
        in_specs += [
            pl.BlockSpec((CONV_HALO, D_MODEL), lambda i: (jnp.maximum(i * hb - 1, 0), 0)),
            pl.BlockSpec((CONV_HALO, D_MODEL), lambda i: (jnp.minimum((i + 1) * hb, n_hb - 1), 0)),
        ]
        args += [x, x]
    in_specs += [
        pl.BlockSpec((1, 1, 6 * D_MODEL), lambda i: (mod_index(i), 0, 0)),
        pl.BlockSpec((1, D_MODEL), const2),
        pl.BlockSpec((D_MODEL, 2 * D_FF), const2),
        pl.BlockSpec((3, 2 * D_FF), const2),
        pl.BlockSpec((1, 2 * D_FF), const2),
        pl.BlockSpec((D_FF, D_MODEL), const2),
    ]
    args += [mod, norm_g, w_up, conv_w, conv_b, w_down]
    return pl.pallas_call(
        functools.partial(_conv_ffn_kernel, halo=halo, tiles_per_seq=tiles_per_seq),
        grid=(n_tiles,),
        in_specs=in_specs,
        out_specs=pl.BlockSpec((TILE, D_MODEL), tok),
        out_shape=jax.ShapeDtypeStruct((n_tok, D_MODEL), F32),
        scratch_shapes=[
            pltpu.VMEM((n_ext, D_MODEL), BF16),
            pltpu.VMEM((N_FF_CHUNKS, 2, n_ext, FF_CHUNK), F32),
            pltpu.VMEM((TILE, D_MODEL), F32),
        ],
        compiler_params=pltpu.CompilerParams(
            dimension_semantics=("arbitrary",), vmem_limit_bytes=VMEM_LIMIT),
        name="conv_ffn_lat" if halo else "conv_ffn_ctx",
    )(*args)


def _toeplitz_bias(rpb):
    n_dc = 2 * WIN_W - 1
    col = np.arange(GRID_W)
    dc = col[None, :] - col[:, None] + (WIN_W - 1)
    win_start = np.clip(col - WIN_W // 2, 0, GRID_W - WIN_W)
    col_ok = (col[None, :] >= win_start[:, None]) & (col[None, :] < win_start[:, None] + WIN_W)
    col_sel = (dc[None] == np.arange(n_dc)[:, None, None]).astype(np.float32)
    blocks = jnp.einsum("hdc,cqk->hdqk", rpb, col_sel, precision=lax.Precision.HIGHEST)
    blocks = jnp.where(col_ok[None, None], blocks, NEG_INF)
    blocks = jnp.pad(blocks, ((0, 0), (0, 1), (0, 0), (0, 0)), constant_values=NEG_INF)
    return jnp.concatenate([blocks, blocks], axis=-1)


def _pair_heads(t):
    b, _, n, _ = t.shape
    t = t.astype(BF16).reshape(b, N_HEADS // 2, 2, n, HEAD_DIM)
    return t.transpose(0, 1, 3, 2, 4).reshape(b, N_HEADS // 2, n, 2 * HEAD_DIM)


def kernel(x_prompt, x_sample, cache_k, cache_v, c, c_ctx, norm_mix_g, norm_ffn_g, w_mod, b_mod,
           w_in, q_norm_g, k_norm_g, pool_w, pool_scale, na_rpb, w_pool_proj, w_na_proj, w_o,
           w_up, ffn_conv_w, ffn_conv_b, w_down):
    depth = w_in.shape[0]
    assert depth == 1
    batch, seq, _ = x_prompt.shape
    dec_batch, dec_seq, _ = x_sample.shape
    assert seq == TILE and dec_seq % TILE == 0 and dec_batch + 1 <= MOD_ROWS
    l = 0

    cond = jnp.zeros((MOD_ROWS, D_MODEL), F32).at[0].set(c_ctx).at[1:1 + dec_batch].set(c)
    mod = _modulation(cond, w_mod[l], b_mod[l]).reshape(MOD_ROWS, 1, 6 * D_MODEL)

    w_in_b = w_in[l].astype(BF16)
    w_pool_b = pool_w[l].astype(BF16)
    w_pp_b = w_pool_proj[l].astype(BF16)
    w_np_b = w_na_proj[l].astype(BF16)
    w_o_b = w_o[l].astype(BF16)
    w_up_b = w_up[l].astype(BF16)
    w_down_b = w_down[l].astype(BF16)
    conv_w = ffn_conv_w[l]
    g_mix = norm_mix_g[l].reshape(1, D_MODEL)
    g_ffn = norm_ffn_g[l].reshape(1, D_MODEL)
    q_g = jnp.tile(q_norm_g[l], N_HEADS).reshape(1, D_NA)
    k_g = jnp.tile(k_norm_g[l], N_HEADS).reshape(1, D_NA)
    s_pool = pool_scale[l].reshape(1, D_POOL)
    conv_b = ffn_conv_b[l].reshape(1, 2 * D_FF)

    ctx_mod = lambda i: 0
    xc = x_prompt.reshape(batch * seq, D_MODEL)
    xc, new_k, new_v = _mixer_seq(xc, mod, ctx_mod, g_mix, w_in_b, q_g, k_g,
                                  w_pool_b, s_pool, w_pp_b, w_np_b, w_o_b)
    xc = _conv_ffn(xc, mod, ctx_mod, g_ffn, w_up_b, conv_w, conv_b, w_down_b, seq_len=seq)
    y_prompt = xc.reshape(batch, seq, D_MODEL)

    tiles_per_seq = dec_seq // TILE
    lat_mod = lambda i: 1 + i // tiles_per_seq
    cache = (_pair_heads(cache_k[:, l]), _pair_heads(cache_v[:, l]), _toeplitz_bias(na_rpb[l]))
    xs = x_sample.reshape(dec_batch * dec_seq, D_MODEL)
    p, q, k, v, gp, gn = _mixer_in(xs, mod, lat_mod, g_mix, w_in_b, q_g, k_g, emit_kv=False)
    xs = _mixer_out(xs, mod, lat_mod, p, q, k, v, gp, gn, w_pool_b, s_pool, w_pp_b, w_np_b, w_o_b,
                    seq_len=dec_seq, cache=cache)
    xs = _conv_ffn(xs, mod, lat_mod, g_ffn, w_up_b, conv_w, conv_b, w_down_b,
                   seq_len=dec_seq)
    y_sample = xs.reshape(dec_batch, dec_seq, D_MODEL)
    return (y_prompt, y_sample, new_k, new_v)
```

```python
import functools

import jax
import jax.numpy as jnp
import numpy as np
from jax import lax
from jax.experimental import pallas as pl
from jax.experimental.pallas import tpu as pltpu

D_MODEL = 1024
GRID_W = 64
N_HEADS = 8
HEAD_DIM = 64
D_NA = N_HEADS * HEAD_DIM
D_POOL = 512
POOL_WINDOWS = (2, 4, 8, 16)
POOL_GROUP = D_POOL // len(POOL_WINDOWS)
WIN_H = 8
WIN_W = 16
D_FF = 2816
Q_BLOCK = 128
EPS = 1e-6
NEG_INF = -1e30
OFF_Q = D_POOL
OFF_K = D_POOL + D_NA
OFF_V = D_POOL + 2 * D_NA
OFF_GP = D_POOL + 3 * D_NA
OFF_GN = OFF_GP + D_MODEL
D_IN = OFF_GN + D_MODEL

TILE = 256
TILE_ROWS = TILE // GRID_W
KEY_ROWS = 12
N_LOCAL = KEY_ROWS * GRID_W
POOL_HALO = 8
SUBLANES = 8
LANES = 128
CONV_HALO = SUBLANES
FF_CHUNK = 256
N_FF_CHUNKS = D_FF // FF_CHUNK
PIPE_DEPTH = 3
ATTN_DEPTH = 1
MOD_ROWS = 8
MOD_TILE = 1536
VMEM_LIMIT = 56 * 1024 * 1024

BF16 = jnp.bfloat16
F32 = jnp.float32


def _dot(a, b):
    return jnp.dot(a, b, preferred_element_type=F32)


def _dot_nt(a, b):
    return lax.dot_general(a, b, (((1,), (1,)), ((), ())), preferred_element_type=F32)


def _modulated_norm(x, g, shift, scale):
    ms = jnp.mean(x * x, axis=-1, keepdims=True)
    return (x * lax.rsqrt(ms + EPS) * g) * (1.0 + scale) + shift


def _sigmoid(x):
    return 1.0 / (1.0 + jnp.exp(-x))


def _mod_kernel(cond_ref, w_ref, b_ref, o_ref):
    cnd = cond_ref[...]
    s = (cnd * _sigmoid(cnd)).astype(BF16)
    o_ref[...] = _dot(s, w_ref[...].astype(BF16)) + b_ref[...]


def _modulation(cond, w_mod, b_mod):
    n = w_mod.shape[1]
    return pl.pallas_call(
        _mod_kernel,
        grid=(n // MOD_TILE,),
        in_specs=[
            pl.BlockSpec((MOD_ROWS, D_MODEL), lambda j: (0, 0)),
            pl.BlockSpec((D_MODEL, MOD_TILE), lambda j: (0, j)),
            pl.BlockSpec((1, MOD_TILE), lambda j: (0, j)),
        ],
        out_specs=pl.BlockSpec((MOD_ROWS, MOD_TILE), lambda j: (0, j)),
        out_shape=jax.ShapeDtypeStruct((MOD_ROWS, n), F32),
        compiler_params=pltpu.CompilerParams(
            dimension_semantics=("arbitrary",), vmem_limit_bytes=VMEM_LIMIT),
        name="modulation",
    )(cond, w_mod, b_mod.reshape(1, n))


def _mixer_in_kernel(x_ref, mod_ref, g_ref, w_ref, qg_ref, kg_ref, *out_refs, emit_kv):
    p_ref, q_ref, k_ref, v_ref, gp_ref, gn_ref = out_refs[:6]
    mod = mod_ref[0]
    h = _modulated_norm(x_ref[...], g_ref[...], mod[:, 0:D_MODEL], mod[:, D_MODEL:2 * D_MODEL])
    h = h.astype(BF16)

    def proj(lo, hi):
        return _dot(h, w_ref[:, lo:hi])

    first_head = lax.broadcasted_iota(jnp.int32, (1, LANES), 1) < HEAD_DIM

    def head_norm(t, g):
        assert 2 * HEAD_DIM == LANES
        tt = t * t
        blocks = []
        for j in range(D_NA // LANES):
            blk = tt[:, j * LANES:(j + 1) * LANES]
            both = jnp.sum(blk, axis=-1, keepdims=True)
            one = jnp.sum(jnp.where(first_head, blk, 0.0), axis=-1, keepdims=True)
            blocks.append(jnp.where(first_head, one, both - one))
        ss = jnp.concatenate(blocks, axis=-1)
        return t * lax.rsqrt(ss * (1.0 / HEAD_DIM) + EPS) * g

    q = proj(OFF_Q, OFF_K)
    k = proj(OFF_K, OFF_V)
    p_ref[...] = proj(0, OFF_Q)
    v = proj(OFF_V, OFF_GP)
    v_ref[...] = v.astype(BF16)
    q = (head_norm(q, qg_ref[...]) * (HEAD_DIM ** -0.5)).astype(BF16)
    gp_ref[...] = proj(OFF_GP, OFF_GN)
    k = head_norm(k, kg_ref[...])
    k_ref[...] = k.astype(BF16)
    gn_ref[...] = proj(OFF_GN, D_IN)
    if emit_kv:
        t = lax.broadcasted_iota(jnp.int32, (TILE, TILE), 0)
        j = lax.broadcasted_iota(jnp.int32, (TILE, TILE), 1)
        n_blocks = TILE // Q_BLOCK
        perm = jnp.where(j == (t % n_blocks) * Q_BLOCK + t // n_blocks, 1.0, 0.0).astype(BF16)
        q = _dot(perm, q).astype(BF16)
    q_ref[...] = q
    if emit_kv:
        nk_ref, nv_ref = out_refs[6:]
        for hd in range(N_HEADS):
            sl = slice(hd * HEAD_DIM, (hd + 1) * HEAD_DIM)
            nk_ref[0, 0, hd] = k[:, sl]
            nv_ref[0, 0, hd] = v[:, sl]


def _mixer_in(x, mod, mod_index, norm_g, w_in, q_g, k_g, *, emit_kv):
    n_tok = x.shape[0]
    n_tiles = n_tok // TILE
    tok = lambda i: (i, 0)
    const = lambda i: (0, 0)
    out_shape = [
        jax.ShapeDtypeStruct((n_tok, D_POOL), F32),
        jax.ShapeDtypeStruct((n_tok, D_NA), BF16),
        jax.ShapeDtypeStruct((n_tok, D_NA), BF16),
        jax.ShapeDtypeStruct((n_tok, D_NA), BF16),
        jax.ShapeDtypeStruct((n_tok, D_MODEL), F32),
        jax.ShapeDtypeStruct((n_tok, D_MODEL), F32),
    ]
    out_specs = [
        pl.BlockSpec((TILE, D_POOL), tok),
        pl.BlockSpec((TILE, D_NA), tok),
        pl.BlockSpec((TILE, D_NA), tok),
        pl.BlockSpec((TILE, D_NA), tok),
        pl.BlockSpec((TILE, D_MODEL), tok),
        pl.BlockSpec((TILE, D_MODEL), tok),
    ]
    if emit_kv:
        kv_shape = jax.ShapeDtypeStruct((n_tiles, 1, N_HEADS, TILE, HEAD_DIM), F32)
        kv_spec = pl.BlockSpec((1, 1, N_HEADS, TILE, HEAD_DIM), lambda i: (i, 0, 0, 0, 0))
        out_shape += [kv_shape, kv_shape]
        out_specs += [kv_spec, kv_spec]
    return pl.pallas_call(
        functools.partial(_mixer_in_kernel, emit_kv=emit_kv),
        grid=(n_tiles,),
        in_specs=[
            pl.BlockSpec((TILE, D_MODEL), tok),
            pl.BlockSpec((1, 1, 6 * D_MODEL), lambda i: (mod_index(i), 0, 0)),
            pl.BlockSpec((1, D_MODEL), const),
            pl.BlockSpec((D_MODEL, D_IN), const),
            pl.BlockSpec((1, D_NA), const),
            pl.BlockSpec((1, D_NA), const),
        ],
        out_specs=out_specs,
        out_shape=out_shape,
        compiler_params=pltpu.CompilerParams(
            dimension_semantics=("arbitrary",), vmem_limit_bytes=VMEM_LIMIT),
        name="mixer_in_ctx" if emit_kv else "mixer_in_lat",
    )(x, mod, norm_g, w_in, q_g, k_g)


_POOL_SLABS = pltpu.VMEM((len(POOL_WINDOWS), TILE + 2 * POOL_HALO, LANES), F32)


def _key_row0(tile, n_rows):
    return jnp.clip(tile * TILE_ROWS - WIN_H // 2, 0, n_rows - KEY_ROWS)


def _fill_local_bias(bias_ref, toep_ref, tile, n_rows):
    key_row0 = int(np.clip(tile * TILE_ROWS - WIN_H // 2, 0, n_rows - KEY_ROWS))
    for a in range(TILE_ROWS):
        r = tile * TILE_ROWS + a
        row_start = int(np.clip(r - WIN_H // 2, 0, n_rows - WIN_H))
        for b in range(KEY_ROWS):
            kr = key_row0 + b
            dr = kr - r + (WIN_H - 1) if row_start <= kr < row_start + WIN_H else 2 * WIN_H - 1
            half = slice((b % 2) * GRID_W, (b % 2 + 1) * GRID_W)
            bias_ref[:, a * GRID_W:(a + 1) * GRID_W, b * GRID_W:(b + 1) * GRID_W] = (
                toep_ref[:, dr, :, half])


def _mixer_out_kernel(*refs, latent, tiles_per_seq):
    if latent:
        (x_ref, mod_ref, p_ref, pprev_ref, pnext_ref, q_ref, k_ref, v_ref, gp_ref, gn_ref,
         ck_ref, cv_ref, toep_ref, wpool_ref, spool_ref, wpp_ref, wnp_ref, wo_ref,
         o_ref, na_ref, pslab_ref, bias_ref) = refs
    else:
        (x_ref, mod_ref, p_ref, q_ref, k_ref, v_ref, gp_ref, gn_ref,
         wpool_ref, spool_ref, wpp_ref, wnp_ref, wo_ref, o_ref, na_ref, pslab_ref) = refs
    i = pl.program_id(0)
    t_in_seq = i % tiles_per_seq
    seq_len = TILE * tiles_per_seq

    assert POOL_GROUP == LANES and POOL_HALO >= max(POOL_WINDOWS) // 2
    pos = t_in_seq * TILE + lax.broadcasted_iota(jnp.int32, (TILE, 1), 0)
    mixed = []
    for gi, w in enumerate(POOL_WINDOWS):
        gs = slice(gi * POOL_GROUP, (gi + 1) * POOL_GROUP)
        if latent:
            pslab_ref[gi, 0:POOL_HALO] = jnp.where(t_in_seq > 0, pprev_ref[:, gs], 0.0)
            pslab_ref[gi, POOL_HALO + TILE:] = jnp.where(
                t_in_seq < tiles_per_seq - 1, pnext_ref[:, gs], 0.0)
        else:
            zeros = jnp.zeros((POOL_HALO, LANES), F32)
            pslab_ref[gi, 0:POOL_HALO] = zeros
            pslab_ref[gi, POOL_HALO + TILE:] = zeros
        pslab_ref[gi, POOL_HALO:POOL_HALO + TILE] = p_ref[:, gs]
        sums = sum(pslab_ref[gi, POOL_HALO + d:POOL_HALO + d + TILE]
                   for d in range(-(w // 2), w - w // 2))
        lo = jnp.maximum(pos - w // 2, 0)
        hi = jnp.minimum(pos + (w - w // 2), seq_len)
        pooled = sums / (hi - lo).astype(F32) - p_ref[:, gs]
        mixed.append(_dot(pooled.astype(BF16), wpool_ref[gi]))
    pool_out = jnp.concatenate(mixed, axis=-1) * spool_ref[...]
    gated_pool = _sigmoid(gp_ref[...]) * _dot(pool_out.astype(BF16), wpp_ref[...])

    if latent:
        n_rows = seq_len // GRID_W
        key0 = pl.multiple_of(_key_row0(t_in_seq, n_rows) * GRID_W, TILE)
        for tile in (0, 1, tiles_per_seq - 1):
            pl.when(t_in_seq == tile)(
                functools.partial(_fill_local_bias, bias_ref, toep_ref, tile, n_rows))

    assert 2 * HEAD_DIM == LANES
    lane = lax.broadcasted_iota(jnp.int32, (1, LANES), 1)
    only = [jnp.where(lane < HEAD_DIM, 1.0, 0.0).astype(BF16),
            jnp.where(lane < HEAD_DIM, 0.0, 1.0).astype(BF16)]

    def pair(j):
        return slice(j * LANES, (j + 1) * LANES)

    def scores(j):
        qp = q_ref[:, pair(j)]
        out = []
        for half in range(2):
            qh = qp * only[half]
            if latent:
                s_loc = (_dot_nt(qh, k_ref[0, pl.ds(key0, N_LOCAL), pair(j)])
                         + bias_ref[2 * j + half])
                out.append((s_loc, _dot_nt(qh, ck_ref[0, j])))
            else:
                out.append((_dot_nt(qh, k_ref[0, :, pair(j)]),))
        return out

    def attend(j, halves):
        if latent:
            vals = (v_ref[0, pl.ds(key0, N_LOCAL), pair(j)], cv_ref[0, j])
        else:
            vals = (v_ref[0, :, pair(j)],)
        o = None
        for half, parts in enumerate(halves):
            m = functools.reduce(jnp.maximum, [jnp.max(s, axis=-1, keepdims=True) for s in parts])
            es = [jnp.exp(s - m) for s in parts]
            denom = sum(jnp.sum(e, axis=-1, keepdims=True) for e in es)
            oh = sum(_dot(e.astype(BF16), v * only[half]) for e, v in zip(es, vals)) / denom
            o = oh if o is None else o + oh
        na_ref[:, pair(j)] = o.astype(BF16)

    n_pairs = N_HEADS // 2
    pending = {j: scores(j) for j in range(min(ATTN_DEPTH, n_pairs))}
    for j in range(n_pairs):
        if j + ATTN_DEPTH < n_pairs:
            pending[j + ATTN_DEPTH] = scores(j + ATTN_DEPTH)
        attend(j, pending.pop(j))

    b = _dot(na_ref[...], wnp_ref[...])
    merged = gated_pool + _sigmoid(gn_ref[...]) * b
    y = _dot(merged.astype(BF16), wo_ref[...])
    gate = mod_ref[0][:, 2 * D_MODEL:3 * D_MODEL]
    o_ref[...] = x_ref[...] + gate * y


def _mixer_out(x, mod, mod_index, p, q, k, v, gp, gn, w_pool, s_pool, w_pp, w_np, w_o,
               *, seq_len, cache=None):
    n_tok = x.shape[0]
    n_tiles = n_tok // TILE
    tiles_per_seq = seq_len // TILE
    latent = cache is not None
    tok = lambda i: (i, 0)
    const2 = lambda i: (0, 0)
    seq = lambda i: (i // tiles_per_seq, 0, 0)
    k = k.reshape(n_tok // seq_len, seq_len, D_NA)
    v = v.reshape(n_tok // seq_len, seq_len, D_NA)

    in_specs = [
        pl.BlockSpec((TILE, D_MODEL), tok),
        pl.BlockSpec((1, 1, 6 * D_MODEL), lambda i: (mod_index(i), 0, 0)),
        pl.BlockSpec((TILE, D_POOL), tok),
    ]
    args = [x, mod, p]
    scratch_shapes = [pltpu.VMEM((TILE, D_NA), BF16), _POOL_SLABS]
    if latent:
        hb = TILE // POOL_HALO
        n_hb = n_tok // POOL_HALO
        in_specs += [
            pl.BlockSpec((POOL_HALO, D_POOL), lambda i: (jnp.maximum(i * hb - 1, 0), 0)),
            pl.BlockSpec((POOL_HALO, D_POOL), lambda i: (jnp.minimum((i + 1) * hb, n_hb - 1), 0)),
        ]
        args += [p, p]
    in_specs += [
        pl.BlockSpec((TILE, D_NA), tok),
        pl.BlockSpec((1, seq_len, D_NA), seq),
        pl.BlockSpec((1, seq_len, D_NA), seq),
        pl.BlockSpec((TILE, D_MODEL), tok),
        pl.BlockSpec((TILE, D_MODEL), tok),
    ]
    args += [q, k, v, gp, gn]
    if latent:
        cache_k, cache_v, toeplitz = cache
        in_specs += [
            pl.BlockSpec((1,) + cache_k.shape[1:], lambda i: (i // tiles_per_seq, 0, 0, 0)),
            pl.BlockSpec((1,) + cache_v.shape[1:], lambda i: (i // tiles_per_seq, 0, 0, 0)),
            pl.BlockSpec(toeplitz.shape, lambda i: (0, 0, 0, 0)),
        ]
        args += [cache_k, cache_v, toeplitz]
        scratch_shapes.append(pltpu.VMEM((N_HEADS, TILE, N_LOCAL), F32))
        assert tiles_per_seq >= 3 and seq_len // GRID_W >= KEY_ROWS
    in_specs += [
        pl.BlockSpec((len(POOL_WINDOWS), POOL_GROUP, POOL_GROUP), lambda i: (0, 0, 0)),
        pl.BlockSpec((1, D_POOL), const2),
        pl.BlockSpec((D_POOL, D_MODEL), const2),
        pl.BlockSpec((D_NA, D_MODEL), const2),
        pl.BlockSpec((D_MODEL, D_MODEL), const2),
    ]
    args += [w_pool, s_pool, w_pp, w_np, w_o]
    return pl.pallas_call(
        functools.partial(_mixer_out_kernel, latent=latent, tiles_per_seq=tiles_per_seq),
        grid=(n_tiles,),
        in_specs=in_specs,
        out_specs=pl.BlockSpec((TILE, D_MODEL), tok),
        out_shape=jax.ShapeDtypeStruct((n_tok, D_MODEL), F32),
        scratch_shapes=scratch_shapes,
        compiler_params=pltpu.CompilerParams(
            dimension_semantics=("arbitrary",), vmem_limit_bytes=VMEM_LIMIT),
        name="mixer_out_lat" if latent else "mixer_out_ctx",
    )(*args)


def _mixer_seq_kernel(x_ref, mod_ref, g_ref, w_ref, qg_ref, kg_ref,
                      wpool_ref, spool_ref, wpp_ref, wnp_ref, wo_ref,
                      o_ref, nk_ref, nv_ref,
                      p_ref, q_ref, k_ref, v_ref, gp_ref, gn_ref, na_ref, pslab_ref):
    _mixer_in_kernel(x_ref, mod_ref, g_ref, w_ref, qg_ref, kg_ref,
                     p_ref, q_ref, k_ref.at[0], v_ref.at[0], gp_ref, gn_ref, nk_ref, nv_ref,
                     emit_kv=True)
    _mixer_out_kernel(x_ref, mod_ref, p_ref, q_ref, k_ref, v_ref, gp_ref, gn_ref,
                      wpool_ref, spool_ref, wpp_ref, wnp_ref, wo_ref, o_ref, na_ref, pslab_ref,
                      latent=False, tiles_per_seq=1)


def _mixer_seq(x, mod, mod_index, norm_g, w_in, q_g, k_g, w_pool, s_pool, w_pp, w_np, w_o):
    n_tok = x.shape[0]
    n_tiles = n_tok // TILE
    tok = lambda i: (i, 0)
    const2 = lambda i: (0, 0)
    kv_shape = jax.ShapeDtypeStruct((n_tiles, 1, N_HEADS, TILE, HEAD_DIM), F32)
    kv_spec = pl.BlockSpec((1, 1, N_HEADS, TILE, HEAD_DIM), lambda i: (i, 0, 0, 0, 0))
    return pl.pallas_call(
        _mixer_seq_kernel,
        grid=(n_tiles,),
        in_specs=[
            pl.BlockSpec((TILE, D_MODEL), tok),
            pl.BlockSpec((1, 1, 6 * D_MODEL), lambda i: (mod_index(i), 0, 0)),
            pl.BlockSpec((1, D_MODEL), const2),
            pl.BlockSpec((D_MODEL, D_IN), const2),
            pl.BlockSpec((1, D_NA), const2),
            pl.BlockSpec((1, D_NA), const2),
            pl.BlockSpec((len(POOL_WINDOWS), POOL_GROUP, POOL_GROUP), lambda i: (0, 0, 0)),
            pl.BlockSpec((1, D_POOL), const2),
            pl.BlockSpec((D_POOL, D_MODEL), const2),
            pl.BlockSpec((D_NA, D_MODEL), const2),
            pl.BlockSpec((D_MODEL, D_MODEL), const2),
        ],
        out_specs=[pl.BlockSpec((TILE, D_MODEL), tok), kv_spec, kv_spec],
        out_shape=[jax.ShapeDtypeStruct((n_tok, D_MODEL), F32), kv_shape, kv_shape],
        scratch_shapes=[
            pltpu.VMEM((TILE, D_POOL), F32),
            pltpu.VMEM((TILE, D_NA), BF16),
            pltpu.VMEM((1, TILE, D_NA), BF16),
            pltpu.VMEM((1, TILE, D_NA), BF16),
            pltpu.VMEM((TILE, D_MODEL), F32),
            pltpu.VMEM((TILE, D_MODEL), F32),
            pltpu.VMEM((TILE, D_NA), BF16),
            _POOL_SLABS,
        ],
        compiler_params=pltpu.CompilerParams(
            dimension_semantics=("arbitrary",), vmem_limit_bytes=VMEM_LIMIT),
        name="mixer_seq",
    )(x, mod, norm_g, w_in, q_g, k_g, w_pool, s_pool, w_pp, w_np, w_o)


def _conv_ffn_kernel(*refs, halo, tiles_per_seq):
    if halo:
        (x_ref, xprev_ref, xnext_ref, mod_ref, g_ref, wup_ref, cw_ref, cb_ref, wdn_ref,
         o_ref, h_ref, u_ref, acc_ref) = refs
    else:
        (x_ref, mod_ref, g_ref, wup_ref, cw_ref, cb_ref, wdn_ref,
         o_ref, h_ref, u_ref, acc_ref) = refs
    i = pl.program_id(0)
    t_in_seq = i % tiles_per_seq
    mod = mod_ref[0]
    shift = mod[:, 3 * D_MODEL:4 * D_MODEL]
    scale = mod[:, 4 * D_MODEL:5 * D_MODEL]

    if halo:
        x_ext = jnp.concatenate([xprev_ref[...], x_ref[...], xnext_ref[...]], axis=0)
        u_rows = slice(0, TILE + 2 * CONV_HALO)
    else:
        x_ext = x_ref[...]
        u_rows = slice(CONV_HALO, CONV_HALO + TILE)
        zeros = jnp.zeros(u_ref.shape[:3] + (CONV_HALO, LANES), F32)
        u_ref[:, :, :, 0:CONV_HALO] = zeros
        u_ref[:, :, :, CONV_HALO + TILE:] = zeros
    h_ref[...] = _modulated_norm(x_ext, g_ref[...], shift, scale).astype(BF16)

    def cols(c, part):
        lo = part * D_FF + c * FF_CHUNK
        return slice(lo, lo + FF_CHUNK)

    n_slabs = FF_CHUNK // LANES

    def up_proj(c):
        for part in range(2):
            u = _dot(h_ref[...], wup_ref[:, cols(c, part)])
            for k in range(n_slabs):
                u_ref[c, part, k, u_rows] = u[:, k * LANES:(k + 1) * LANES]
                if halo:
                    last_prev = CONV_HALO - 1
                    first_next = CONV_HALO + TILE
                    u_ref[c, part, k, last_prev:last_prev + 1] = jnp.where(
                        t_in_seq == 0, 0.0, u[last_prev:last_prev + 1, k * LANES:(k + 1) * LANES])
                    u_ref[c, part, k, first_next:first_next + 1] = jnp.where(
                        t_in_seq == tiles_per_seq - 1, 0.0,
                        u[first_next:first_next + 1, k * LANES:(k + 1) * LANES])

    def conv(c, part):
        slabs = []
        for k in range(n_slabs):
            lo = cols(c, part).start + k * LANES
            sl = slice(lo, lo + LANES)
            u_prev = u_ref[c, part, k, CONV_HALO - 1:CONV_HALO - 1 + TILE]
            u_here = u_ref[c, part, k, CONV_HALO:CONV_HALO + TILE]
            u_next = u_ref[c, part, k, CONV_HALO + 1:CONV_HALO + 1 + TILE]
            slabs.append(u_prev * cw_ref[0:1, sl] + u_here * cw_ref[1:2, sl]
                         + u_next * cw_ref[2:3, sl] + cb_ref[:, sl])
        return jnp.concatenate(slabs, axis=-1)

    def gated_down(c):
        a = conv(c, 0)
        act = (a * _sigmoid(a) * conv(c, 1)).astype(BF16)
        return _dot(act, wdn_ref[c * FF_CHUNK:(c + 1) * FF_CHUNK, :])

    for c in range(min(PIPE_DEPTH, N_FF_CHUNKS)):
        up_proj(c)
    for c in range(N_FF_CHUNKS):
        if c + PIPE_DEPTH < N_FF_CHUNKS:
            up_proj(c + PIPE_DEPTH)
        part = gated_down(c)
        if c == 0:
            acc_ref[...] = part
        elif c + 1 < N_FF_CHUNKS:
            acc_ref[...] += part
        else:
            o_ref[...] = x_ref[...] + mod[:, 5 * D_MODEL:6 * D_MODEL] * (acc_ref[...] + part)


def _conv_ffn(x, mod, mod_index, norm_g, w_up, conv_w, conv_b, w_down, *, seq_len):
    n_tok = x.shape[0]
    n_tiles = n_tok // TILE
    tiles_per_seq = seq_len // TILE
    halo = tiles_per_seq > 1
    n_ext = TILE + 2 * CONV_HALO if halo else TILE
    tok = lambda i: (i, 0)
    const2 = lambda i: (0, 0)
    in_specs = [pl.BlockSpec((TILE, D_MODEL), tok)]
    args = [x]
    if halo:
        hb = TILE // CONV_HALO
        n_hb = n_tok // CONV_HALO
        in_specs += [
            pl.BlockSpec((CONV_HALO, D_MODEL), lambda i: (jnp.maximum(i * hb - 1, 0), 0)),
            pl.BlockSpec((CONV_HALO, D_MODEL), lambda i: (jnp.minimum((i + 1) * hb, n_hb - 1), 0)),
        ]
        args += [x, x]
    in_specs += [
        pl.BlockSpec((1, 1, 6 * D_MODEL), lambda i: (mod_index(i), 0, 0)),
        pl.BlockSpec((1, D_MODEL), const2),
        pl.BlockSpec((D_MODEL, 2 * D_FF), const2),
        pl.BlockSpec((3, 2 * D_FF), const2),
        pl.BlockSpec((1, 2 * D_FF), const2),
        pl.BlockSpec((D_FF, D_MODEL), const2),
    ]
    args += [mod, norm_g, w_up, conv_w, conv_b, w_down]
    return pl.pallas_call(
        functools.partial(_conv_ffn_kernel, halo=halo, tiles_per_seq=tiles_per_seq),
        grid=(n_tiles,),
        in_specs=in_specs,
        out_specs=pl.BlockSpec((TILE, D_MODEL), tok),
        out_shape=jax.ShapeDtypeStruct((n_tok, D_MODEL), F32),
        scratch_shapes=[
            pltpu.VMEM((n_ext, D_MODEL), BF16),
            pltpu.VMEM((N_FF_CHUNKS, 2, FF_CHUNK // LANES, TILE + 2 * CONV_HALO, LANES), F32),
            pltpu.VMEM((TILE, D_MODEL), F32),
        ],
        compiler_params=pltpu.CompilerParams(
            dimension_semantics=("arbitrary",), vmem_limit_bytes=VMEM_LIMIT),
        name="conv_ffn_lat" if halo else "conv_ffn_ctx",
    )(*args)


def _toeplitz_bias(rpb):
    n_dc = 2 * WIN_W - 1
    col = np.arange(GRID_W)
    dc = col[None, :] - col[:, None] + (WIN_W - 1)
    win_start = np.clip(col - WIN_W // 2, 0, GRID_W - WIN_W)
    col_ok = (col[None, :] >= win_start[:, None]) & (col[None, :] < win_start[:, None] + WIN_W)
    col_sel = (dc[None] == np.arange(n_dc)[:, None, None]).astype(np.float32)
    blocks = jnp.einsum("hdc,cqk->hdqk", rpb, col_sel, precision=lax.Precision.HIGHEST)
    blocks = jnp.where(col_ok[None, None], blocks, NEG_INF)
    blocks = jnp.pad(blocks, ((0, 0), (0, 1), (0, 0), (0, 0)), constant_values=NEG_INF)
    return jnp.concatenate([blocks, blocks], axis=-1)


def _pair_heads(t):
    b, _, n, _ = t.shape
    t = t.astype(BF16).reshape(b, N_HEADS // 2, 2, n, HEAD_DIM)
    return t.transpose(0, 1, 3, 2, 4).reshape(b, N_HEADS // 2, n, 2 * HEAD_DIM)


def kernel(x_prompt, x_sample, cache_k, cache_v, c, c_ctx, norm_mix_g, norm_ffn_g, w_mod, b_mod,
           w_in, q_norm_g, k_norm_g, pool_w, pool_scale, na_rpb, w_pool_proj, w_na_proj, w_o,
           w_up, ffn_conv_w, ffn_conv_b, w_down):
    depth = w_in.shape[0]
    assert depth == 1
    batch, seq, _ = x_prompt.shape
    dec_batch, dec_seq, _ = x_sample.shape
    assert seq == TILE and dec_seq % TILE == 0 and dec_batch + 1 <= MOD_ROWS
    l = 0

    cond = jnp.zeros((MOD_ROWS, D_MODEL), F32).at[0].set(c_ctx).at[1:1 + dec_batch].set(c)
    mod = _modulation(cond, w_mod[l], b_mod[l]).reshape(MOD_ROWS, 1, 6 * D_MODEL)

    w_in_b = w_in[l].astype(BF16)
    w_pool_b = pool_w[l].astype(BF16)
    w_pp_b = w_pool_proj[l].astype(BF16)
    w_np_b = w_na_proj[l].astype(BF16)
    w_o_b = w_o[l].astype(BF16)
    w_up_b = w_up[l].astype(BF16)
    w_down_b = w_down[l].astype(BF16)
    conv_w = ffn_conv_w[l]
    g_mix = norm_mix_g[l].reshape(1, D_MODEL)
    g_ffn = norm_ffn_g[l].reshape(1, D_MODEL)
    q_g = jnp.tile(q_norm_g[l], N_HEADS).reshape(1, D_NA)
    k_g = jnp.tile(k_norm_g[l], N_HEADS).reshape(1, D_NA)
    s_pool = pool_scale[l].reshape(1, D_POOL)
    conv_b = ffn_conv_b[l].reshape(1, 2 * D_FF)

    ctx_mod = lambda i: 0
    xc = x_prompt.reshape(batch * seq, D_MODEL)
    xc, new_k, new_v = _mixer_seq(xc, mod, ctx_mod, g_mix, w_in_b, q_g, k_g,
                                  w_pool_b, s_pool, w_pp_b, w_np_b, w_o_b)
    xc = _conv_ffn(xc, mod, ctx_mod, g_ffn, w_up_b, conv_w, conv_b, w_down_b, seq_len=seq)
    y_prompt = xc.reshape(batch, seq, D_MODEL)

    tiles_per_seq = dec_seq // TILE
    lat_mod = lambda i: 1 + i // tiles_per_seq
    cache = (_pair_heads(cache_k[:, l]), _pair_heads(cache_v[:, l]), _toeplitz_bias(na_rpb[l]))
    xs = x_sample.reshape(dec_batch * dec_seq, D_MODEL)
    p, q, k, v, gp, gn = _mixer_in(xs, mod, lat_mod, g_mix, w_in_b, q_g, k_g, emit_kv=False)
    xs = _mixer_out(xs, mod, lat_mod, p, q, k, v, gp, gn, w_pool_b, s_pool, w_pp_b, w_np_b, w_o_b,
                    seq_len=dec_seq, cache=cache)
    xs = _conv_ffn(xs, mod, lat_mod, g_ffn, w_up_b, conv_w, conv_b, w_down_b,
                   seq_len=dec_seq)
    y_sample = xs.reshape(dec_batch, dec_seq, D_MODEL)
    return (y_prompt, y_sample, new_k, new_v)
```

```python
import functools

import jax
import jax.numpy as jnp
import numpy as np
from jax import lax
from jax.experimental import pallas as pl
from jax.experimental.pallas import tpu as pltpu

D_MODEL = 1024
GRID_W = 64
N_HEADS = 8
HEAD_DIM = 64
D_NA = N_HEADS * HEAD_DIM
D_POOL = 512
POOL_WINDOWS = (2, 4, 8, 16)
POOL_GROUP = D_POOL // len(POOL_WINDOWS)
WIN_H = 8
WIN_W = 16
D_FF = 2816
Q_BLOCK = 128
EPS = 1e-6
NEG_INF = -1e30
OFF_Q = D_POOL
OFF_K = D_POOL + D_NA
OFF_V = D_POOL + 2 * D_NA
OFF_GP = D_POOL + 3 * D_NA
OFF_GN = OFF_GP + D_MODEL
D_IN = OFF_GN + D_MODEL

TILE = 256
TILE_ROWS = TILE // GRID_W
KEY_ROWS = 12
N_LOCAL = KEY_ROWS * GRID_W
SUBLANES = 8
LANES = 128
POOL_HALO = SUBLANES
CONV_HALO = SUBLANES
FF_CHUNK = 256
N_FF_CHUNKS = D_FF // FF_CHUNK
PIPE_DEPTH = 3
U_SLOTS = PIPE_DEPTH + 1
ATTN_DEPTH = 1
MOD_ROWS = 8
MOD_TILE = 1536
VMEM_LIMIT = 56 * 1024 * 1024

BF16 = jnp.bfloat16
F32 = jnp.float32


def _dot(a, b):
    return jnp.dot(a, b, preferred_element_type=F32)


def _dot_nt(a, b):
    return lax.dot_general(a, b, (((1,), (1,)), ((), ())), preferred_element_type=F32)


def _modulated_norm(x, g, shift, scale):
    ms = jnp.mean(x * x, axis=-1, keepdims=True)
    return (x * lax.rsqrt(ms + EPS) * g) * (1.0 + scale) + shift


def _sigmoid(x):
    return 1.0 / (1.0 + jnp.exp(-x))


def _params(**kwargs):
    return pltpu.CompilerParams(
        dimension_semantics=("arbitrary",), vmem_limit_bytes=VMEM_LIMIT, **kwargs)


def _resident(shape):
    return pl.BlockSpec(shape, lambda i: (0,) * len(shape), pipeline_mode=pl.Buffered(1))


def _mod_kernel(cond_ref, w_ref, b_ref, o_ref):
    cnd = cond_ref[...]
    s = (cnd * _sigmoid(cnd)).astype(BF16)
    o_ref[...] = _dot(s, w_ref[...].astype(BF16)) + b_ref[...]


def _modulation(cond, w_mod, b_mod):
    n = w_mod.shape[1]
    return pl.pallas_call(
        _mod_kernel,
        grid=(n // MOD_TILE,),
        in_specs=[
            pl.BlockSpec((MOD_ROWS, D_MODEL), lambda j: (0, 0)),
            pl.BlockSpec((D_MODEL, MOD_TILE), lambda j: (0, j)),
            pl.BlockSpec((1, MOD_TILE), lambda j: (0, j)),
        ],
        out_specs=pl.BlockSpec((MOD_ROWS, MOD_TILE), lambda j: (0, j)),
        out_shape=jax.ShapeDtypeStruct((MOD_ROWS, n), F32),
        compiler_params=_params(),
        name="modulation",
    )(cond, w_mod, b_mod.reshape(1, n))


def _mixer_in_kernel(x_ref, mod_ref, g_ref, w_ref, qg_ref, kg_ref, *out_refs, emit_kv):
    p_ref, q_ref, k_ref, v_ref, gp_ref, gn_ref = out_refs[:6]
    mod = mod_ref[0]
    h = _modulated_norm(x_ref[...], g_ref[...], mod[:, 0:D_MODEL], mod[:, D_MODEL:2 * D_MODEL])
    h = h.astype(BF16)

    def proj(lo, hi):
        return _dot(h, w_ref[:, lo:hi])

    first_head = lax.broadcasted_iota(jnp.int32, (1, LANES), 1) < HEAD_DIM

    def head_norm(t, g):
        assert 2 * HEAD_DIM == LANES
        tt = t * t
        blocks = []
        for j in range(D_NA // LANES):
            blk = tt[:, j * LANES:(j + 1) * LANES]
            both = jnp.sum(blk, axis=-1, keepdims=True)
            one = jnp.sum(jnp.where(first_head, blk, 0.0), axis=-1, keepdims=True)
            blocks.append(jnp.where(first_head, one, both - one))
        ss = jnp.concatenate(blocks, axis=-1)
        return t * lax.rsqrt(ss * (1.0 / HEAD_DIM) + EPS) * g

    q = proj(OFF_Q, OFF_K)
    k = proj(OFF_K, OFF_V)
    p_ref[...] = proj(0, OFF_Q)
    v = proj(OFF_V, OFF_GP)
    v_ref[...] = v.astype(BF16)
    q = (head_norm(q, qg_ref[...]) * (HEAD_DIM ** -0.5)).astype(BF16)
    gp_ref[...] = proj(OFF_GP, OFF_GN)
    k = head_norm(k, kg_ref[...])
    k_ref[...] = k.astype(BF16)
    gn_ref[...] = proj(OFF_GN, D_IN)
    if emit_kv:
        t = lax.broadcasted_iota(jnp.int32, (TILE, TILE), 0)
        j = lax.broadcasted_iota(jnp.int32, (TILE, TILE), 1)
        n_blocks = TILE // Q_BLOCK
        perm = jnp.where(j == (t % n_blocks) * Q_BLOCK + t // n_blocks, 1.0, 0.0).astype(BF16)
        q = _dot(perm, q).astype(BF16)
    q_ref[...] = q
    if emit_kv:
        nk_ref, nv_ref = out_refs[6:]
        for hd in range(N_HEADS):
            sl = slice(hd * HEAD_DIM, (hd + 1) * HEAD_DIM)
            nk_ref[0, 0, hd] = k[:, sl]
            nv_ref[0, 0, hd] = v[:, sl]


def _mixer_in(x, mod, mod_index, norm_g, w_in, q_g, k_g):
    n_tok = x.shape[0]
    tok = lambda i: (i, 0)
    widths = [(D_POOL, F32), (D_NA, BF16), (D_NA, BF16), (D_NA, BF16), (D_MODEL, F32),
              (D_MODEL, F32)]
    return pl.pallas_call(
        functools.partial(_mixer_in_kernel, emit_kv=False),
        grid=(n_tok // TILE,),
        in_specs=[
            pl.BlockSpec((TILE, D_MODEL), tok),
            pl.BlockSpec((1, 1, 6 * D_MODEL), lambda i: (mod_index(i), 0, 0)),
        ] + [_resident(w.shape) for w in (norm_g, w_in, q_g, k_g)],
        out_specs=[pl.BlockSpec((TILE, n), tok) for n, _ in widths],
        out_shape=[jax.ShapeDtypeStruct((n_tok, n), dt) for n, dt in widths],
        compiler_params=_params(),
        name="mixer_in",
    )(x, mod, norm_g, w_in, q_g, k_g)


_POOL_SLABS = pltpu.VMEM((len(POOL_WINDOWS), TILE + 2 * POOL_HALO, LANES), F32)


def _key_row0(tile, n_rows):
    return jnp.clip(tile * TILE_ROWS - WIN_H // 2, 0, n_rows - KEY_ROWS)


def _fill_local_bias(bias_ref, toep_ref, tile, n_rows):
    key_row0 = int(np.clip(tile * TILE_ROWS - WIN_H // 2, 0, n_rows - KEY_ROWS))
    for a in range(TILE_ROWS):
        r = tile * TILE_ROWS + a
        row_start = int(np.clip(r - WIN_H // 2, 0, n_rows - WIN_H))
        for b in range(KEY_ROWS):
            kr = key_row0 + b
            dr = kr - r + (WIN_H - 1) if row_start <= kr < row_start + WIN_H else 2 * WIN_H - 1
            half = slice((b % 2) * GRID_W, (b % 2 + 1) * GRID_W)
            bias_ref[:, a * GRID_W:(a + 1) * GRID_W, b * GRID_W:(b + 1) * GRID_W] = (
                toep_ref[:, dr, :, half])


def _mixer_out_kernel(*refs, latent, tiles_per_seq):
    if latent:
        (x_ref, mod_ref, p_ref, pprev_ref, pnext_ref, q_ref, k_ref, v_ref, gp_ref, gn_ref,
         ck_ref, cv_ref, toep_ref, wpool_ref, spool_ref, wpp_ref, wnp_ref, wo_ref,
         o_ref, na_ref, pslab_ref, bias_ref) = refs
    else:
        (x_ref, mod_ref, p_ref, q_ref, k_ref, v_ref, gp_ref, gn_ref,
         wpool_ref, spool_ref, wpp_ref, wnp_ref, wo_ref, o_ref, na_ref, pslab_ref) = refs
    i = pl.program_id(0)
    t_in_seq = i % tiles_per_seq
    seq_len = TILE * tiles_per_seq

    assert POOL_GROUP == LANES and POOL_HALO >= max(POOL_WINDOWS) // 2
    pos = t_in_seq * TILE + lax.broadcasted_iota(jnp.int32, (TILE, 1), 0)
    mixed = []
    for gi, w in enumerate(POOL_WINDOWS):
        gs = slice(gi * POOL_GROUP, (gi + 1) * POOL_GROUP)
        if latent:
            pslab_ref[gi, 0:POOL_HALO] = jnp.where(t_in_seq > 0, pprev_ref[:, gs], 0.0)
            pslab_ref[gi, POOL_HALO + TILE:] = jnp.where(
                t_in_seq < tiles_per_seq - 1, pnext_ref[:, gs], 0.0)
        else:
            zeros = jnp.zeros((POOL_HALO, LANES), F32)
            pslab_ref[gi, 0:POOL_HALO] = zeros
            pslab_ref[gi, POOL_HALO + TILE:] = zeros
        pslab_ref[gi, POOL_HALO:POOL_HALO + TILE] = p_ref[:, gs]
        sums = sum(pslab_ref[gi, POOL_HALO + d:POOL_HALO + d + TILE]
                   for d in range(-(w // 2), w - w // 2))
        lo = jnp.maximum(pos - w // 2, 0)
        hi = jnp.minimum(pos + (w - w // 2), seq_len)
        pooled = sums / (hi - lo).astype(F32) - p_ref[:, gs]
        mixed.append(_dot(pooled.astype(BF16), wpool_ref[gi]))
    pool_out = jnp.concatenate(mixed, axis=-1) * spool_ref[...]
    gated_pool = _sigmoid(gp_ref[...]) * _dot(pool_out.astype(BF16), wpp_ref[...])

    if latent:
        n_rows = seq_len // GRID_W
        key0 = pl.multiple_of(_key_row0(t_in_seq, n_rows) * GRID_W, TILE)
        for tile in (0, 1, tiles_per_seq - 1):
            pl.when(t_in_seq == tile)(
                functools.partial(_fill_local_bias, bias_ref, toep_ref, tile, n_rows))

    assert 2 * HEAD_DIM == LANES
    lane = lax.broadcasted_iota(jnp.int32, (1, LANES), 1)
    only = [jnp.where(lane < HEAD_DIM, 1.0, 0.0).astype(BF16),
            jnp.where(lane < HEAD_DIM, 0.0, 1.0).astype(BF16)]

    def pair(j):
        return slice(j * LANES, (j + 1) * LANES)

    def scores(j):
        qp = q_ref[:, pair(j)]
        out = []
        for half in range(2):
            qh = qp * only[half]
            if latent:
                s_loc = (_dot_nt(qh, k_ref[0, pl.ds(key0, N_LOCAL), pair(j)])
                         + bias_ref[2 * j + half])
                out.append((s_loc, _dot_nt(qh, ck_ref[0, j])))
            else:
                out.append((_dot_nt(qh, k_ref[0, :, pair(j)]),))
        return out

    def attend(j, halves):
        if latent:
            vals = (v_ref[0, pl.ds(key0, N_LOCAL), pair(j)], cv_ref[0, j])
        else:
            vals = (v_ref[0, :, pair(j)],)
        o = None
        for half, parts in enumerate(halves):
            m = functools.reduce(jnp.maximum, [jnp.max(s, axis=-1, keepdims=True) for s in parts])
            es = [jnp.exp(s - m) for s in parts]
            denom = sum(jnp.sum(e, axis=-1, keepdims=True) for e in es)
            oh = sum(_dot(e.astype(BF16), v * only[half]) for e, v in zip(es, vals)) / denom
            o = oh if o is None else o + oh
        na_ref[:, pair(j)] = o.astype(BF16)

    n_pairs = N_HEADS // 2
    pending = {j: scores(j) for j in range(min(ATTN_DEPTH, n_pairs))}
    for j in range(n_pairs):
        if j + ATTN_DEPTH < n_pairs:
            pending[j + ATTN_DEPTH] = scores(j + ATTN_DEPTH)
        attend(j, pending.pop(j))

    b = _dot(na_ref[...], wnp_ref[...])
    merged = gated_pool + _sigmoid(gn_ref[...]) * b
    y = _dot(merged.astype(BF16), wo_ref[...])
    gate = mod_ref[0][:, 2 * D_MODEL:3 * D_MODEL]
    o_ref[...] = x_ref[...] + gate * y


def _halo_specs(n_tok, halo, width):
    per_tile = TILE // halo
    last = n_tok // halo - 1
    return [
        pl.BlockSpec((halo, width), lambda i: (jnp.maximum(i * per_tile - 1, 0), 0)),
        pl.BlockSpec((halo, width), lambda i: (jnp.minimum((i + 1) * per_tile, last), 0)),
    ]


def _mixer_out(x, mod, mod_index, p, q, k, v, gp, gn, cache_k, cache_v, toeplitz,
               w_pool, s_pool, w_pp, w_np, w_o, *, seq_len):
    n_tok = x.shape[0]
    tiles_per_seq = seq_len // TILE
    assert tiles_per_seq >= 3 and seq_len // GRID_W >= KEY_ROWS
    tok = lambda i: (i, 0)
    seq3 = lambda i: (i // tiles_per_seq, 0, 0)
    seq4 = lambda i: (i // tiles_per_seq, 0, 0, 0)
    k = k.reshape(n_tok // seq_len, seq_len, D_NA)
    v = v.reshape(n_tok // seq_len, seq_len, D_NA)
    weights = (toeplitz, w_pool, s_pool, w_pp, w_np, w_o)
    return pl.pallas_call(
        functools.partial(_mixer_out_kernel, latent=True, tiles_per_seq=tiles_per_seq),
        grid=(n_tok // TILE,),
        in_specs=[
            pl.BlockSpec((TILE, D_MODEL), tok),
            pl.BlockSpec((1, 1, 6 * D_MODEL), lambda i: (mod_index(i), 0, 0)),
            pl.BlockSpec((TILE, D_POOL), tok),
        ] + _halo_specs(n_tok, POOL_HALO, D_POOL) + [
            pl.BlockSpec((TILE, D_NA), tok),
            pl.BlockSpec((1, seq_len, D_NA), seq3),
            pl.BlockSpec((1, seq_len, D_NA), seq3),
            pl.BlockSpec((TILE, D_MODEL), tok),
            pl.BlockSpec((TILE, D_MODEL), tok),
            pl.BlockSpec((1,) + cache_k.shape[1:], seq4),
            pl.BlockSpec((1,) + cache_v.shape[1:], seq4),
        ] + [_resident(w.shape) for w in weights],
        out_specs=pl.BlockSpec((TILE, D_MODEL), tok),
        out_shape=jax.ShapeDtypeStruct((n_tok, D_MODEL), F32),
        scratch_shapes=[
            pltpu.VMEM((TILE, D_NA), BF16),
            _POOL_SLABS,
            pltpu.VMEM((N_HEADS, TILE, N_LOCAL), F32),
        ],
        compiler_params=_params(),
        name="mixer_out",
    )(x, mod, p, p, p, q, k, v, gp, gn, cache_k, cache_v, *weights)


_FF_SLABS = pltpu.VMEM((U_SLOTS, 2, FF_CHUNK // LANES, TILE + 2 * CONV_HALO, LANES), F32)


def _conv_ffn_kernel(*refs, halo, tiles_per_seq):
    if halo:
        (x_ref, xprev_ref, xnext_ref, mod_ref, g_ref, wup_ref, cw_ref, cb_ref, wdn_ref,
         o_ref, h_ref, u_ref, acc_ref) = refs
    else:
        (x_ref, mod_ref, g_ref, wup_ref, cw_ref, cb_ref, wdn_ref,
         o_ref, h_ref, u_ref, acc_ref) = refs
    i = pl.program_id(0)
    t_in_seq = i % tiles_per_seq
    mod = mod_ref[0]
    shift = mod[:, 3 * D_MODEL:4 * D_MODEL]
    scale = mod[:, 4 * D_MODEL:5 * D_MODEL]

    if halo:
        x_ext = jnp.concatenate([xprev_ref[...], x_ref[...], xnext_ref[...]], axis=0)
        u_rows = slice(0, TILE + 2 * CONV_HALO)
    else:
        x_ext = x_ref[...]
        u_rows = slice(CONV_HALO, CONV_HALO + TILE)
        zeros = jnp.zeros(u_ref.shape[:3] + (CONV_HALO, LANES), F32)
        u_ref[:, :, :, 0:CONV_HALO] = zeros
        u_ref[:, :, :, CONV_HALO + TILE:] = zeros
    h_ref[...] = _modulated_norm(x_ext, g_ref[...], shift, scale).astype(BF16)

    def cols(c, part):
        lo = part * D_FF + c * FF_CHUNK
        return slice(lo, lo + FF_CHUNK)

    n_slabs = FF_CHUNK // LANES

    def up_proj(c):
        for part in range(2):
            u = _dot(h_ref[...], wup_ref[:, cols(c, part)])
            for k in range(n_slabs):
                u_ref[c % U_SLOTS, part, k, u_rows] = u[:, k * LANES:(k + 1) * LANES]
                if halo:
                    last_prev = CONV_HALO - 1
                    first_next = CONV_HALO + TILE
                    u_ref[c % U_SLOTS, part, k, last_prev:last_prev + 1] = jnp.where(
                        t_in_seq == 0, 0.0, u[last_prev:last_prev + 1, k * LANES:(k + 1) * LANES])
                    u_ref[c % U_SLOTS, part, k, first_next:first_next + 1] = jnp.where(
                        t_in_seq == tiles_per_seq - 1, 0.0,
                        u[first_next:first_next + 1, k * LANES:(k + 1) * LANES])

    def conv(c, part):
        slabs = []
        for k in range(n_slabs):
            lo = cols(c, part).start + k * LANES
            sl = slice(lo, lo + LANES)
            u_prev = u_ref[c % U_SLOTS, part, k, CONV_HALO - 1:CONV_HALO - 1 + TILE]
            u_here = u_ref[c % U_SLOTS, part, k, CONV_HALO:CONV_HALO + TILE]
            u_next = u_ref[c % U_SLOTS, part, k, CONV_HALO + 1:CONV_HALO + 1 + TILE]
            slabs.append(u_prev * cw_ref[0:1, sl] + u_here * cw_ref[1:2, sl]
                         + u_next * cw_ref[2:3, sl] + cb_ref[:, sl])
        return jnp.concatenate(slabs, axis=-1)

    def gated_down(c):
        a = conv(c, 0)
        act = (a * _sigmoid(a) * conv(c, 1)).astype(BF16)
        return _dot(act, wdn_ref[c * FF_CHUNK:(c + 1) * FF_CHUNK, :])

    for c in range(min(PIPE_DEPTH, N_FF_CHUNKS)):
        up_proj(c)
    for c in range(N_FF_CHUNKS):
        if c + PIPE_DEPTH < N_FF_CHUNKS:
            up_proj(c + PIPE_DEPTH)
        part = gated_down(c)
        if c == 0:
            acc_ref[...] = part
        elif c + 1 < N_FF_CHUNKS:
            acc_ref[...] += part
        else:
            o_ref[...] = x_ref[...] + mod[:, 5 * D_MODEL:6 * D_MODEL] * (acc_ref[...] + part)


def _conv_ffn(x, mod, mod_index, norm_g, w_up, conv_w, conv_b, w_down, *, seq_len):
    n_tok = x.shape[0]
    tiles_per_seq = seq_len // TILE
    assert tiles_per_seq > 1
    tok = lambda i: (i, 0)
    weights = (norm_g, w_up, conv_w, conv_b, w_down)
    return pl.pallas_call(
        functools.partial(_conv_ffn_kernel, halo=True, tiles_per_seq=tiles_per_seq),
        grid=(n_tok // TILE,),
        in_specs=[pl.BlockSpec((TILE, D_MODEL), tok)] + _halo_specs(n_tok, CONV_HALO, D_MODEL) + [
            pl.BlockSpec((1, 1, 6 * D_MODEL), lambda i: (mod_index(i), 0, 0)),
        ] + [_resident(w.shape) for w in weights],
        out_specs=pl.BlockSpec((TILE, D_MODEL), tok),
        out_shape=jax.ShapeDtypeStruct((n_tok, D_MODEL), F32),
        scratch_shapes=[
            pltpu.VMEM((TILE + 2 * CONV_HALO, D_MODEL), BF16),
            _FF_SLABS,
            pltpu.VMEM((TILE, D_MODEL), F32),
        ],
        compiler_params=_params(),
        name="conv_ffn",
    )(x, x, x, mod, *weights)


def _block_seq_kernel(x_ref, mod_ref, gmix_ref, win_ref, qg_ref, kg_ref,
                      wpool_ref, spool_ref, wpp_ref, wnp_ref, wo_ref,
                      gffn_ref, wup_ref, cw_ref, cb_ref, wdn_ref,
                      o_ref, nk_ref, nv_ref,
                      p_ref, q_ref, k_ref, v_ref, gp_ref, gn_ref, na_ref, pslab_ref,
                      x1_ref, h_ref, u_ref, acc_ref):
    _mixer_in_kernel(x_ref, mod_ref, gmix_ref, win_ref, qg_ref, kg_ref,
                     p_ref, q_ref, k_ref.at[0], v_ref.at[0], gp_ref, gn_ref, nk_ref, nv_ref,
                     emit_kv=True)
    _mixer_out_kernel(x_ref, mod_ref, p_ref, q_ref, k_ref, v_ref, gp_ref, gn_ref,
                      wpool_ref, spool_ref, wpp_ref, wnp_ref, wo_ref, x1_ref, na_ref, pslab_ref,
                      latent=False, tiles_per_seq=1)
    _conv_ffn_kernel(x1_ref, mod_ref, gffn_ref, wup_ref, cw_ref, cb_ref, wdn_ref,
                     o_ref, h_ref, u_ref, acc_ref, halo=False, tiles_per_seq=1)


def _block_seq(x, mod, mod_index, g_mix, w_in, q_g, k_g, w_pool, s_pool, w_pp, w_np, w_o,
               g_ffn, w_up, conv_w, conv_b, w_down):
    n_tok = x.shape[0]
    n_tiles = n_tok // TILE
    tok = lambda i: (i, 0)
    kv_shape = jax.ShapeDtypeStruct((n_tiles, 1, N_HEADS, TILE, HEAD_DIM), F32)
    kv_spec = pl.BlockSpec((1, 1, N_HEADS, TILE, HEAD_DIM), lambda i: (i, 0, 0, 0, 0))
    weights = (g_mix, w_in, q_g, k_g, w_pool, s_pool, w_pp, w_np, w_o,
               g_ffn, w_up, conv_w, conv_b, w_down)
    return pl.pallas_call(
        _block_seq_kernel,
        grid=(n_tiles,),
        in_specs=[
            pl.BlockSpec((TILE, D_MODEL), tok),
            pl.BlockSpec((1, 1, 6 * D_MODEL), lambda i: (mod_index(i), 0, 0)),
        ] + [_resident(w.shape) for w in weights],
        out_specs=[pl.BlockSpec((TILE, D_MODEL), tok), kv_spec, kv_spec],
        out_shape=[jax.ShapeDtypeStruct((n_tok, D_MODEL), F32), kv_shape, kv_shape],
        scratch_shapes=[
            pltpu.VMEM((TILE, D_POOL), F32),
            pltpu.VMEM((TILE, D_NA), BF16),
            pltpu.VMEM((1, TILE, D_NA), BF16),
            pltpu.VMEM((1, TILE, D_NA), BF16),
            pltpu.VMEM((TILE, D_MODEL), F32),
            pltpu.VMEM((TILE, D_MODEL), F32),
            pltpu.VMEM((TILE, D_NA), BF16),
            _POOL_SLABS,
            pltpu.VMEM((TILE, D_MODEL), F32),
            pltpu.VMEM((TILE, D_MODEL), BF16),
            _FF_SLABS,
            pltpu.VMEM((TILE, D_MODEL), F32),
        ],
        compiler_params=_params(),
        name="block_seq",
    )(x, mod, *weights)


def _toeplitz_bias(rpb):
    n_dc = 2 * WIN_W - 1
    col = np.arange(GRID_W)
    dc = col[None, :] - col[:, None] + (WIN_W - 1)
    win_start = np.clip(col - WIN_W // 2, 0, GRID_W - WIN_W)
    col_ok = (col[None, :] >= win_start[:, None]) & (col[None, :] < win_start[:, None] + WIN_W)
    col_sel = (dc[None] == np.arange(n_dc)[:, None, None]).astype(np.float32)
    blocks = jnp.einsum("hdc,cqk->hdqk", rpb, col_sel, precision=lax.Precision.HIGHEST)
    blocks = jnp.where(col_ok[None, None], blocks, NEG_INF)
    blocks = jnp.pad(blocks, ((0, 0), (0, 1), (0, 0), (0, 0)), constant_values=NEG_INF)
    return jnp.concatenate([blocks, blocks], axis=-1)


def _pair_heads(t):
    b, _, n, _ = t.shape
    t = t.astype(BF16).reshape(b, N_HEADS // 2, 2, n, HEAD_DIM)
    return t.transpose(0, 1, 3, 2, 4).reshape(b, N_HEADS // 2, n, 2 * HEAD_DIM)


def kernel(x_prompt, x_sample, cache_k, cache_v, c, c_ctx, norm_mix_g, norm_ffn_g, w_mod, b_mod,
           w_in, q_norm_g, k_norm_g, pool_w, pool_scale, na_rpb, w_pool_proj, w_na_proj, w_o,
           w_up, ffn_conv_w, ffn_conv_b, w_down):
    depth = w_in.shape[0]
    assert depth == 1
    batch, seq, _ = x_prompt.shape
    dec_batch, dec_seq, _ = x_sample.shape
    assert seq == TILE and dec_seq % TILE == 0 and dec_batch + 1 <= MOD_ROWS
    l = 0

    cond = jnp.zeros((MOD_ROWS, D_MODEL), F32).at[0].set(c_ctx).at[1:1 + dec_batch].set(c)
    mod = _modulation(cond, w_mod[l], b_mod[l]).reshape(MOD_ROWS, 1, 6 * D_MODEL)

    w_in_b = w_in[l].astype(BF16)
    w_pool_b = pool_w[l].astype(BF16)
    w_pp_b = w_pool_proj[l].astype(BF16)
    w_np_b = w_na_proj[l].astype(BF16)
    w_o_b = w_o[l].astype(BF16)
    w_up_b = w_up[l].astype(BF16)
    w_down_b = w_down[l].astype(BF16)
    conv_w = ffn_conv_w[l]
    g_mix = norm_mix_g[l].reshape(1, D_MODEL)
    g_ffn = norm_ffn_g[l].reshape(1, D_MODEL)
    q_g = jnp.tile(q_norm_g[l], N_HEADS).reshape(1, D_NA)
    k_g = jnp.tile(k_norm_g[l], N_HEADS).reshape(1, D_NA)
    s_pool = pool_scale[l].reshape(1, D_POOL)
    conv_b = ffn_conv_b[l].reshape(1, 2 * D_FF)

    ctx_mod = lambda i: 0
    xc = x_prompt.reshape(batch * seq, D_MODEL)
    xc, new_k, new_v = _block_seq(xc, mod, ctx_mod, g_mix, w_in_b, q_g, k_g,
                                  w_pool_b, s_pool, w_pp_b, w_np_b, w_o_b,
                                  g_ffn, w_up_b, conv_w, conv_b, w_down_b)
    y_prompt = xc.reshape(batch, seq, D_MODEL)

    tiles_per_seq = dec_seq // TILE
    lat_mod = lambda i: 1 + i // tiles_per_seq
    xs = x_sample.reshape(dec_batch * dec_seq, D_MODEL)
    p, q, k, v, gp, gn = _mixer_in(xs, mod, lat_mod, g_mix, w_in_b, q_g, k_g)
    xs = _mixer_out(xs, mod, lat_mod, p, q, k, v, gp, gn,
                    _pair_heads(cache_k[:, l]), _pair_heads(cache_v[:, l]),
                    _toeplitz_bias(na_rpb[l]), w_pool_b, s_pool, w_pp_b, w_np_b, w_o_b,
                    seq_len=dec_seq)
    xs = _conv_ffn(xs, mod, lat_mod, g_ffn, w_up_b, conv_w, conv_b, w_down_b, seq_len=dec_seq)
    y_sample = xs.reshape(dec_batch, dec_seq, D_MODEL)
    return (y_prompt, y_sample, new_k, new_v)
```

```python
import functools

import jax
import jax.numpy as jnp
import numpy as np
from jax import lax
from jax.experimental import pallas as pl
from jax.experimental.pallas import tpu as pltpu

D_MODEL = 1024
GRID_W = 64
N_HEADS = 8
HEAD_DIM = 64
D_NA = N_HEADS * HEAD_DIM
D_POOL = 512
POOL_WINDOWS = (2, 4, 8, 16)
POOL_GROUP = D_POOL // len(POOL_WINDOWS)
WIN_H = 8
WIN_W = 16
D_FF = 2816
Q_BLOCK = 128
EPS = 1e-6
NEG_INF = -1e30
OFF_Q = D_POOL
OFF_K = D_POOL + D_NA
OFF_V = D_POOL + 2 * D_NA
OFF_GP = D_POOL + 3 * D_NA
OFF_GN = OFF_GP + D_MODEL
D_IN = OFF_GN + D_MODEL

TILE = 256
TILE_ROWS = TILE // GRID_W
KEY_ROWS = 12
N_LOCAL = KEY_ROWS * GRID_W
SUBLANES = 8
LANES = 128
POOL_HALO = SUBLANES
CONV_HALO = SUBLANES
FF_CHUNK = 256
N_FF_CHUNKS = D_FF // FF_CHUNK
PIPE_DEPTH = 3
U_SLOTS = PIPE_DEPTH + 1
ATTN_DEPTH = 1
MIXER_IN_ROWS = 512
MOD_ROWS = 8
MOD_TILE = 768
VMEM_LIMIT = 56 * 1024 * 1024

BF16 = jnp.bfloat16
F32 = jnp.float32


def _dot(a, b):
    return jnp.dot(a, b, preferred_element_type=F32)


def _dot_nt(a, b):
    return lax.dot_general(a, b, (((1,), (1,)), ((), ())), preferred_element_type=F32)


def _modulated_norm(x, g, shift, scale):
    ms = jnp.mean(x * x, axis=-1, keepdims=True)
    return (x * lax.rsqrt(ms + EPS) * g) * (1.0 + scale) + shift


def _sigmoid(x):
    return 1.0 / (1.0 + jnp.exp(-x))


def _params(**kwargs):
    return pltpu.CompilerParams(
        dimension_semantics=("arbitrary",), vmem_limit_bytes=VMEM_LIMIT, **kwargs)


def _resident(shape):
    return pl.BlockSpec(shape, lambda i: (0,) * len(shape), pipeline_mode=pl.Buffered(1))


def _mod_kernel(cond_ref, w_ref, b_ref, o_ref):
    cnd = cond_ref[...]
    s = (cnd * _sigmoid(cnd)).astype(BF16)
    o_ref[...] = _dot(s, w_ref[...].astype(BF16)) + b_ref[...]


def _modulation(cond, w_mod, b_mod):
    n = w_mod.shape[1]
    return pl.pallas_call(
        _mod_kernel,
        grid=(n // MOD_TILE,),
        in_specs=[
            pl.BlockSpec((MOD_ROWS, D_MODEL), lambda j: (0, 0)),
            pl.BlockSpec((D_MODEL, MOD_TILE), lambda j: (0, j)),
            pl.BlockSpec((1, MOD_TILE), lambda j: (0, j)),
        ],
        out_specs=pl.BlockSpec((MOD_ROWS, MOD_TILE), lambda j: (0, j)),
        out_shape=jax.ShapeDtypeStruct((MOD_ROWS, n), F32),
        compiler_params=_params(),
        name="modulation",
    )(cond, w_mod, b_mod.reshape(1, n))


def _mixer_in_kernel(x_ref, mod_ref, g_ref, w_ref, qg_ref, kg_ref, *out_refs, emit_kv):
    p_ref, q_ref, k_ref, v_ref, gp_ref, gn_ref = out_refs[:6]
    mod = mod_ref[0]
    h = _modulated_norm(x_ref[...], g_ref[...], mod[:, 0:D_MODEL], mod[:, D_MODEL:2 * D_MODEL])
    h = h.astype(BF16)

    def proj(lo, hi):
        return _dot(h, w_ref[:, lo:hi])

    first_head = lax.broadcasted_iota(jnp.int32, (1, LANES), 1) < HEAD_DIM

    def head_norm(t, g):
        assert 2 * HEAD_DIM == LANES
        tt = t * t
        blocks = []
        for j in range(D_NA // LANES):
            blk = tt[:, j * LANES:(j + 1) * LANES]
            both = jnp.sum(blk, axis=-1, keepdims=True)
            one = jnp.sum(jnp.where(first_head, blk, 0.0), axis=-1, keepdims=True)
            blocks.append(jnp.where(first_head, one, both - one))
        ss = jnp.concatenate(blocks, axis=-1)
        return t * lax.rsqrt(ss * (1.0 / HEAD_DIM) + EPS) * g

    q = proj(OFF_Q, OFF_K)
    k = proj(OFF_K, OFF_V)
    p_ref[...] = proj(0, OFF_Q)
    v = proj(OFF_V, OFF_GP)
    v_ref[...] = v.astype(BF16)
    q = (head_norm(q, qg_ref[...]) * (HEAD_DIM ** -0.5)).astype(BF16)
    gp_ref[...] = proj(OFF_GP, OFF_GN)
    k = head_norm(k, kg_ref[...])
    k_ref[...] = k.astype(BF16)
    gn_ref[...] = proj(OFF_GN, D_IN)
    if emit_kv:
        t = lax.broadcasted_iota(jnp.int32, (TILE, TILE), 0)
        j = lax.broadcasted_iota(jnp.int32, (TILE, TILE), 1)
        n_blocks = TILE // Q_BLOCK
        perm = jnp.where(j == (t % n_blocks) * Q_BLOCK + t // n_blocks, 1.0, 0.0).astype(BF16)
        q = _dot(perm, q).astype(BF16)
    q_ref[...] = q
    if emit_kv:
        nk_ref, nv_ref = out_refs[6:]
        for hd in range(N_HEADS):
            sl = slice(hd * HEAD_DIM, (hd + 1) * HEAD_DIM)
            nk_ref[0, 0, hd] = k[:, sl]
            nv_ref[0, 0, hd] = v[:, sl]


def _mixer_in(x, mod, mod_index, norm_g, w_in, q_g, k_g, *, rows):
    n_tok = x.shape[0]
    tok = lambda i: (i, 0)
    widths = [(D_POOL, F32), (D_NA, BF16), (D_NA, BF16), (D_NA, BF16), (D_MODEL, F32),
              (D_MODEL, F32)]
    return pl.pallas_call(
        functools.partial(_mixer_in_kernel, emit_kv=False),
        grid=(n_tok // rows,),
        in_specs=[
            pl.BlockSpec((rows, D_MODEL), tok),
            pl.BlockSpec((1, 1, 6 * D_MODEL), lambda i: (mod_index(i), 0, 0)),
        ] + [_resident(w.shape) for w in (norm_g, w_in, q_g, k_g)],
        out_specs=[pl.BlockSpec((rows, n), tok) for n, _ in widths],
        out_shape=[jax.ShapeDtypeStruct((n_tok, n), dt) for n, dt in widths],
        compiler_params=_params(),
        name="mixer_in",
    )(x, mod, norm_g, w_in, q_g, k_g)


_POOL_SLABS = pltpu.VMEM((len(POOL_WINDOWS), TILE + 2 * POOL_HALO, LANES), F32)


def _key_row0(tile, n_rows):
    return jnp.clip(tile * TILE_ROWS - WIN_H // 2, 0, n_rows - KEY_ROWS)


def _fill_local_bias(bias_ref, toep_ref, tile, n_rows):
    key_row0 = int(np.clip(tile * TILE_ROWS - WIN_H // 2, 0, n_rows - KEY_ROWS))
    for a in range(TILE_ROWS):
        r = tile * TILE_ROWS + a
        row_start = int(np.clip(r - WIN_H // 2, 0, n_rows - WIN_H))
        for b in range(KEY_ROWS):
            kr = key_row0 + b
            dr = kr - r + (WIN_H - 1) if row_start <= kr < row_start + WIN_H else 2 * WIN_H - 1
            half = slice((b % 2) * GRID_W, (b % 2 + 1) * GRID_W)
            bias_ref[:, a * GRID_W:(a + 1) * GRID_W, b * GRID_W:(b + 1) * GRID_W] = (
                toep_ref[:, dr, :, half])


def _mixer_out_kernel(*refs, latent, tiles_per_seq):
    if latent:
        (x_ref, mod_ref, p_ref, pprev_ref, pnext_ref, q_ref, k_ref, v_ref, gp_ref, gn_ref,
         ck_ref, cv_ref, toep_ref, wpool_ref, spool_ref, wpp_ref, wnp_ref, wo_ref,
         o_ref, na_ref, pslab_ref, bias_ref) = refs
    else:
        (x_ref, mod_ref, p_ref, q_ref, k_ref, v_ref, gp_ref, gn_ref,
         wpool_ref, spool_ref, wpp_ref, wnp_ref, wo_ref, o_ref, na_ref, pslab_ref) = refs
    i = pl.program_id(0)
    t_in_seq = i % tiles_per_seq
    seq_len = TILE * tiles_per_seq

    assert POOL_GROUP == LANES and POOL_HALO >= max(POOL_WINDOWS) // 2
    pos = t_in_seq * TILE + lax.broadcasted_iota(jnp.int32, (TILE, 1), 0)
    mixed = []
    for gi, w in enumerate(POOL_WINDOWS):
        gs = slice(gi * POOL_GROUP, (gi + 1) * POOL_GROUP)
        if latent:
            pslab_ref[gi, 0:POOL_HALO] = jnp.where(t_in_seq > 0, pprev_ref[:, gs], 0.0)
            pslab_ref[gi, POOL_HALO + TILE:] = jnp.where(
                t_in_seq < tiles_per_seq - 1, pnext_ref[:, gs], 0.0)
        else:
            zeros = jnp.zeros((POOL_HALO, LANES), F32)
            pslab_ref[gi, 0:POOL_HALO] = zeros
            pslab_ref[gi, POOL_HALO + TILE:] = zeros
        pslab_ref[gi, POOL_HALO:POOL_HALO + TILE] = p_ref[:, gs]
        sums = sum(pslab_ref[gi, POOL_HALO + d:POOL_HALO + d + TILE]
                   for d in range(-(w // 2), w - w // 2))
        lo = jnp.maximum(pos - w // 2, 0)
        hi = jnp.minimum(pos + (w - w // 2), seq_len)
        pooled = sums / (hi - lo).astype(F32) - p_ref[:, gs]
        mixed.append(_dot(pooled.astype(BF16), wpool_ref[gi]))
    pool_out = jnp.concatenate(mixed, axis=-1) * spool_ref[...]
    gated_pool = _sigmoid(gp_ref[...]) * _dot(pool_out.astype(BF16), wpp_ref[...])

    if latent:
        n_rows = seq_len // GRID_W
        key0 = pl.multiple_of(_key_row0(t_in_seq, n_rows) * GRID_W, TILE)
        for tile in (0, 1, tiles_per_seq - 1):
            pl.when(t_in_seq == tile)(
                functools.partial(_fill_local_bias, bias_ref, toep_ref, tile, n_rows))

    assert 2 * HEAD_DIM == LANES
    lane = lax.broadcasted_iota(jnp.int32, (1, LANES), 1)
    only = [jnp.where(lane < HEAD_DIM, 1.0, 0.0).astype(BF16),
            jnp.where(lane < HEAD_DIM, 0.0, 1.0).astype(BF16)]

    def pair(j):
        return slice(j * LANES, (j + 1) * LANES)

    def scores(j):
        qp = q_ref[:, pair(j)]
        out = []
        for half in range(2):
            qh = qp * only[half]
            if latent:
                s_loc = (_dot_nt(qh, k_ref[0, pl.ds(key0, N_LOCAL), pair(j)])
                         + bias_ref[2 * j + half])
                out.append((s_loc, _dot_nt(qh, ck_ref[0, j])))
            else:
                out.append((_dot_nt(qh, k_ref[0, :, pair(j)]),))
        return out

    def attend(j, halves):
        if latent:
            vals = (v_ref[0, pl.ds(key0, N_LOCAL), pair(j)], cv_ref[0, j])
        else:
            vals = (v_ref[0, :, pair(j)],)
        o = None
        for half, parts in enumerate(halves):
            m = functools.reduce(jnp.maximum, [jnp.max(s, axis=-1, keepdims=True) for s in parts])
            es = [jnp.exp(s - m) for s in parts]
            denom = sum(jnp.sum(e, axis=-1, keepdims=True) for e in es)
            oh = sum(_dot(e.astype(BF16), v * only[half]) for e, v in zip(es, vals)) / denom
            o = oh if o is None else o + oh
        na_ref[:, pair(j)] = o.astype(BF16)

    n_pairs = N_HEADS // 2
    pending = {j: scores(j) for j in range(min(ATTN_DEPTH, n_pairs))}
    for j in range(n_pairs):
        if j + ATTN_DEPTH < n_pairs:
            pending[j + ATTN_DEPTH] = scores(j + ATTN_DEPTH)
        attend(j, pending.pop(j))

    b = _dot(na_ref[...], wnp_ref[...])
    merged = gated_pool + _sigmoid(gn_ref[...]) * b
    y = _dot(merged.astype(BF16), wo_ref[...])
    gate = mod_ref[0][:, 2 * D_MODEL:3 * D_MODEL]
    o_ref[...] = x_ref[...] + gate * y


def _halo_specs(n_tok, halo, width):
    per_tile = TILE // halo
    last = n_tok // halo - 1
    return [
        pl.BlockSpec((halo, width), lambda i: (jnp.maximum(i * per_tile - 1, 0), 0)),
        pl.BlockSpec((halo, width), lambda i: (jnp.minimum((i + 1) * per_tile, last), 0)),
    ]


def _mixer_out(x, mod, mod_index, p, q, k, v, gp, gn, cache_k, cache_v, toeplitz,
               w_pool, s_pool, w_pp, w_np, w_o, *, seq_len):
    n_tok = x.shape[0]
    tiles_per_seq = seq_len // TILE
    assert tiles_per_seq >= 3 and seq_len // GRID_W >= KEY_ROWS
    tok = lambda i: (i, 0)
    seq3 = lambda i: (i // tiles_per_seq, 0, 0)
    seq4 = lambda i: (i // tiles_per_seq, 0, 0, 0)
    k = k.reshape(n_tok // seq_len, seq_len, D_NA)
    v = v.reshape(n_tok // seq_len, seq_len, D_NA)
    weights = (toeplitz, w_pool, s_pool, w_pp, w_np, w_o)
    return pl.pallas_call(
        functools.partial(_mixer_out_kernel, latent=True, tiles_per_seq=tiles_per_seq),
        grid=(n_tok // TILE,),
        in_specs=[
            pl.BlockSpec((TILE, D_MODEL), tok),
            pl.BlockSpec((1, 1, 6 * D_MODEL), lambda i: (mod_index(i), 0, 0)),
            pl.BlockSpec((TILE, D_POOL), tok),
        ] + _halo_specs(n_tok, POOL_HALO, D_POOL) + [
            pl.BlockSpec((TILE, D_NA), tok),
            pl.BlockSpec((1, seq_len, D_NA), seq3),
            pl.BlockSpec((1, seq_len, D_NA), seq3),
            pl.BlockSpec((TILE, D_MODEL), tok),
            pl.BlockSpec((TILE, D_MODEL), tok),
            pl.BlockSpec((1,) + cache_k.shape[1:], seq4),
            pl.BlockSpec((1,) + cache_v.shape[1:], seq4),
        ] + [_resident(w.shape) for w in weights],
        out_specs=pl.BlockSpec((TILE, D_MODEL), tok),
        out_shape=jax.ShapeDtypeStruct((n_tok, D_MODEL), F32),
        scratch_shapes=[
            pltpu.VMEM((TILE, D_NA), BF16),
            _POOL_SLABS,
            pltpu.VMEM((N_HEADS, TILE, N_LOCAL), F32),
        ],
        compiler_params=_params(),
        name="mixer_out",
    )(x, mod, p, p, p, q, k, v, gp, gn, cache_k, cache_v, *weights)


_FF_SLABS = pltpu.VMEM((U_SLOTS, 2, FF_CHUNK // LANES, TILE + 2 * CONV_HALO, LANES), F32)


def _conv_ffn_kernel(*refs, halo, tiles_per_seq):
    if halo:
        (x_ref, xprev_ref, xnext_ref, mod_ref, g_ref, wup_ref, cw_ref, cb_ref, wdn_ref,
         o_ref, h_ref, u_ref, acc_ref) = refs
    else:
        (x_ref, mod_ref, g_ref, wup_ref, cw_ref, cb_ref, wdn_ref,
         o_ref, h_ref, u_ref, acc_ref) = refs
    i = pl.program_id(0)
    t_in_seq = i % tiles_per_seq
    mod = mod_ref[0]
    shift = mod[:, 3 * D_MODEL:4 * D_MODEL]
    scale = mod[:, 4 * D_MODEL:5 * D_MODEL]

    if halo:
        x_ext = jnp.concatenate([xprev_ref[...], x_ref[...], xnext_ref[...]], axis=0)
        u_rows = slice(0, TILE + 2 * CONV_HALO)
    else:
        x_ext = x_ref[...]
        u_rows = slice(CONV_HALO, CONV_HALO + TILE)
        zeros = jnp.zeros(u_ref.shape[:3] + (CONV_HALO, LANES), F32)
        u_ref[:, :, :, 0:CONV_HALO] = zeros
        u_ref[:, :, :, CONV_HALO + TILE:] = zeros
    h_ref[...] = _modulated_norm(x_ext, g_ref[...], shift, scale).astype(BF16)

    def cols(c, part):
        lo = part * D_FF + c * FF_CHUNK
        return slice(lo, lo + FF_CHUNK)

    n_slabs = FF_CHUNK // LANES

    def up_proj(c):
        for part in range(2):
            u = _dot(h_ref[...], wup_ref[:, cols(c, part)])
            for k in range(n_slabs):
                u_ref[c % U_SLOTS, part, k, u_rows] = u[:, k * LANES:(k + 1) * LANES]
                if halo:
                    last_prev = CONV_HALO - 1
                    first_next = CONV_HALO + TILE
                    u_ref[c % U_SLOTS, part, k, last_prev:last_prev + 1] = jnp.where(
                        t_in_seq == 0, 0.0, u[last_prev:last_prev + 1, k * LANES:(k + 1) * LANES])
                    u_ref[c % U_SLOTS, part, k, first_next:first_next + 1] = jnp.where(
                        t_in_seq == tiles_per_seq - 1, 0.0,
                        u[first_next:first_next + 1, k * LANES:(k + 1) * LANES])

    def conv(c, part):
        slabs = []
        for k in range(n_slabs):
            lo = cols(c, part).start + k * LANES
            sl = slice(lo, lo + LANES)
            u_prev = u_ref[c % U_SLOTS, part, k, CONV_HALO - 1:CONV_HALO - 1 + TILE]
            u_here = u_ref[c % U_SLOTS, part, k, CONV_HALO:CONV_HALO + TILE]
            u_next = u_ref[c % U_SLOTS, part, k, CONV_HALO + 1:CONV_HALO + 1 + TILE]
            slabs.append(u_prev * cw_ref[0:1, sl] + u_here * cw_ref[1:2, sl]
                         + u_next * cw_ref[2:3, sl] + cb_ref[:, sl])
        return jnp.concatenate(slabs, axis=-1)

    def gated_down(c):
        a = conv(c, 0)
        act = (a * _sigmoid(a) * conv(c, 1)).astype(BF16)
        return _dot(act, wdn_ref[c * FF_CHUNK:(c + 1) * FF_CHUNK, :])

    for c in range(min(PIPE_DEPTH, N_FF_CHUNKS)):
        up_proj(c)
    for c in range(N_FF_CHUNKS):
        if c + PIPE_DEPTH < N_FF_CHUNKS:
            up_proj(c + PIPE_DEPTH)
        part = gated_down(c)
        if c == 0:
            acc_ref[...] = part
        elif c + 1 < N_FF_CHUNKS:
            acc_ref[...] += part
        else:
            o_ref[...] = x_ref[...] + mod[:, 5 * D_MODEL:6 * D_MODEL] * (acc_ref[...] + part)


def _conv_ffn(x, mod, mod_index, norm_g, w_up, conv_w, conv_b, w_down, *, seq_len):
    n_tok = x.shape[0]
    tiles_per_seq = seq_len // TILE
    assert tiles_per_seq > 1
    tok = lambda i: (i, 0)
    weights = (norm_g, w_up, conv_w, conv_b, w_down)
    return pl.pallas_call(
        functools.partial(_conv_ffn_kernel, halo=True, tiles_per_seq=tiles_per_seq),
        grid=(n_tok // TILE,),
        in_specs=[pl.BlockSpec((TILE, D_MODEL), tok)] + _halo_specs(n_tok, CONV_HALO, D_MODEL) + [
            pl.BlockSpec((1, 1, 6 * D_MODEL), lambda i: (mod_index(i), 0, 0)),
        ] + [_resident(w.shape) for w in weights],
        out_specs=pl.BlockSpec((TILE, D_MODEL), tok),
        out_shape=jax.ShapeDtypeStruct((n_tok, D_MODEL), F32),
        scratch_shapes=[
            pltpu.VMEM((TILE + 2 * CONV_HALO, D_MODEL), BF16),
            _FF_SLABS,
            pltpu.VMEM((TILE, D_MODEL), F32),
        ],
        compiler_params=_params(),
        name="conv_ffn",
    )(x, x, x, mod, *weights)


def _block_seq_kernel(x_ref, mod_ref, gmix_ref, win_ref, qg_ref, kg_ref,
                      wpool_ref, spool_ref, wpp_ref, wnp_ref, wo_ref,
                      gffn_ref, wup_ref, cw_ref, cb_ref, wdn_ref,
                      o_ref, nk_ref, nv_ref,
                      p_ref, q_ref, k_ref, v_ref, gp_ref, gn_ref, na_ref, pslab_ref,
                      x1_ref, h_ref, u_ref, acc_ref):
    _mixer_in_kernel(x_ref, mod_ref, gmix_ref, win_ref, qg_ref, kg_ref,
                     p_ref, q_ref, k_ref.at[0], v_ref.at[0], gp_ref, gn_ref, nk_ref, nv_ref,
                     emit_kv=True)
    _mixer_out_kernel(x_ref, mod_ref, p_ref, q_ref, k_ref, v_ref, gp_ref, gn_ref,
                      wpool_ref, spool_ref, wpp_ref, wnp_ref, wo_ref, x1_ref, na_ref, pslab_ref,
                      latent=False, tiles_per_seq=1)
    _conv_ffn_kernel(x1_ref, mod_ref, gffn_ref, wup_ref, cw_ref, cb_ref, wdn_ref,
                     o_ref, h_ref, u_ref, acc_ref, halo=False, tiles_per_seq=1)


def _block_seq(x, mod, mod_index, g_mix, w_in, q_g, k_g, w_pool, s_pool, w_pp, w_np, w_o,
               g_ffn, w_up, conv_w, conv_b, w_down):
    n_tok = x.shape[0]
    n_tiles = n_tok // TILE
    tok = lambda i: (i, 0)
    kv_shape = jax.ShapeDtypeStruct((n_tiles, 1, N_HEADS, TILE, HEAD_DIM), F32)
    kv_spec = pl.BlockSpec((1, 1, N_HEADS, TILE, HEAD_DIM), lambda i: (i, 0, 0, 0, 0))
    weights = (g_mix, w_in, q_g, k_g, w_pool, s_pool, w_pp, w_np, w_o,
               g_ffn, w_up, conv_w, conv_b, w_down)
    return pl.pallas_call(
        _block_seq_kernel,
        grid=(n_tiles,),
        in_specs=[
            pl.BlockSpec((TILE, D_MODEL), tok),
            pl.BlockSpec((1, 1, 6 * D_MODEL), lambda i: (mod_index(i), 0, 0)),
        ] + [_resident(w.shape) for w in weights],
        out_specs=[pl.BlockSpec((TILE, D_MODEL), tok), kv_spec, kv_spec],
        out_shape=[jax.ShapeDtypeStruct((n_tok, D_MODEL), F32), kv_shape, kv_shape],
        scratch_shapes=[
            pltpu.VMEM((TILE, D_POOL), F32),
            pltpu.VMEM((TILE, D_NA), BF16),
            pltpu.VMEM((1, TILE, D_NA), BF16),
            pltpu.VMEM((1, TILE, D_NA), BF16),
            pltpu.VMEM((TILE, D_MODEL), F32),
            pltpu.VMEM((TILE, D_MODEL), F32),
            pltpu.VMEM((TILE, D_NA), BF16),
            _POOL_SLABS,
            pltpu.VMEM((TILE, D_MODEL), F32),
            pltpu.VMEM((TILE, D_MODEL), BF16),
            _FF_SLABS,
            pltpu.VMEM((TILE, D_MODEL), F32),
        ],
        compiler_params=_params(),
        name="block_seq",
    )(x, mod, *weights)


def _toeplitz_bias(rpb):
    n_heads, n_dr, n_dc = rpb.shape
    period = GRID_W + n_dc
    z = jnp.pad(rpb, ((0, 0), (0, 1), (0, period - n_dc)), constant_values=NEG_INF)
    rows = jnp.broadcast_to(z[:, :, None, :], (n_heads, n_dr + 1, GRID_W, period))
    rows = rows.reshape(n_heads, n_dr + 1, GRID_W * period)[:, :, :GRID_W * (period - 1)]
    skew = rows.reshape(n_heads, n_dr + 1, GRID_W, period - 1)
    blocks = skew[..., WIN_W - 1:WIN_W - 1 + GRID_W]
    col = np.arange(GRID_W)
    win_start = np.clip(col - WIN_W // 2, 0, GRID_W - WIN_W)
    col_ok = (col[None, :] >= win_start[:, None]) & (col[None, :] < win_start[:, None] + WIN_W)
    blocks = jnp.where(col_ok[None, None], blocks, NEG_INF)
    return jnp.concatenate([blocks, blocks], axis=-1)


def _pair_heads(t):
    b, _, n, _ = t.shape
    t = t.astype(BF16).reshape(b, N_HEADS // 2, 2, n, HEAD_DIM)
    return t.transpose(0, 1, 3, 2, 4).reshape(b, N_HEADS // 2, n, 2 * HEAD_DIM)


def kernel(x_prompt, x_sample, cache_k, cache_v, c, c_ctx, norm_mix_g, norm_ffn_g, w_mod, b_mod,
           w_in, q_norm_g, k_norm_g, pool_w, pool_scale, na_rpb, w_pool_proj, w_na_proj, w_o,
           w_up, ffn_conv_w, ffn_conv_b, w_down):
    depth = w_in.shape[0]
    assert depth == 1
    batch, seq, _ = x_prompt.shape
    dec_batch, dec_seq, _ = x_sample.shape
    assert seq == TILE and dec_seq % MIXER_IN_ROWS == 0 and dec_batch + 1 <= MOD_ROWS
    l = 0

    cond = jnp.concatenate(
        [c_ctx[None], c, jnp.zeros((MOD_ROWS - 1 - dec_batch, D_MODEL), F32)], axis=0)
    mod = _modulation(cond, w_mod[l], b_mod[l]).reshape(MOD_ROWS, 1, 6 * D_MODEL)

    w_in_b = w_in[l].astype(BF16)
    w_pool_b = pool_w[l].astype(BF16)
    w_pp_b = w_pool_proj[l].astype(BF16)
    w_np_b = w_na_proj[l].astype(BF16)
    w_o_b = w_o[l].astype(BF16)
    w_up_b = w_up[l].astype(BF16)
    w_down_b = w_down[l].astype(BF16)
    conv_w = ffn_conv_w[l]
    g_mix = norm_mix_g[l].reshape(1, D_MODEL)
    g_ffn = norm_ffn_g[l].reshape(1, D_MODEL)
    q_g = jnp.tile(q_norm_g[l], N_HEADS).reshape(1, D_NA)
    k_g = jnp.tile(k_norm_g[l], N_HEADS).reshape(1, D_NA)
    s_pool = pool_scale[l].reshape(1, D_POOL)
    conv_b = ffn_conv_b[l].reshape(1, 2 * D_FF)

    ctx_mod = lambda i: 0
    xc = x_prompt.reshape(batch * seq, D_MODEL)
    xc, new_k, new_v = _block_seq(xc, mod, ctx_mod, g_mix, w_in_b, q_g, k_g,
                                  w_pool_b, s_pool, w_pp_b, w_np_b, w_o_b,
                                  g_ffn, w_up_b, conv_w, conv_b, w_down_b)
    y_prompt = xc.reshape(batch, seq, D_MODEL)

    tiles_per_seq = dec_seq // TILE
    lat_mod = lambda i: 1 + i // tiles_per_seq
    xs = x_sample.reshape(dec_batch * dec_seq, D_MODEL)
    p, q, k, v, gp, gn = _mixer_in(
        xs, mod, lambda i: 1 + i // (dec_seq // MIXER_IN_ROWS), g_mix, w_in_b, q_g, k_g,
        rows=MIXER_IN_ROWS)
    xs = _mixer_out(xs, mod, lat_mod, p, q, k, v, gp, gn,
                    _pair_heads(cache_k[:, l]), _pair_heads(cache_v[:, l]),
                    _toeplitz_bias(na_rpb[l]), w_pool_b, s_pool, w_pp_b, w_np_b, w_o_b,
                    seq_len=dec_seq)
    xs = _conv_ffn(xs, mod, lat_mod, g_ffn, w_up_b, conv_w, conv_b, w_down_b, seq_len=dec_seq)
    y_sample = xs.reshape(dec_batch, dec_seq, D_MODEL)
    return (y_prompt, y_sample, new_k, new_v)
```

```python
import functools

import jax
import jax.numpy as jnp
import numpy as np
from jax import lax
from jax.experimental import pallas as pl
from jax.experimental.pallas import tpu as pltpu

D_MODEL = 1024
GRID_W = 64
N_HEADS = 8
HEAD_DIM = 64
D_NA = N_HEADS * HEAD_DIM
D_POOL = 512
POOL_WINDOWS = (2, 4, 8, 16)
POOL_GROUP = D_POOL // len(POOL_WINDOWS)
WIN_H = 8
WIN_W = 16
D_FF = 2816
Q_BLOCK = 128
EPS = 1e-6
NEG_INF = -1e30
OFF_Q = D_POOL
OFF_K = D_POOL + D_NA
OFF_V = D_POOL + 2 * D_NA
OFF_GP = D_POOL + 3 * D_NA
OFF_GN = OFF_GP + D_MODEL
D_IN = OFF_GN + D_MODEL

TILE = 256
TILE_ROWS = TILE // GRID_W
KEY_ROWS = 12
N_LOCAL = KEY_ROWS * GRID_W
SUBLANES = 8
LANES = 128
POOL_HALO = SUBLANES
CONV_HALO = SUBLANES
FF_CHUNK = 256
N_FF_CHUNKS = D_FF // FF_CHUNK
PIPE_DEPTH = 3
U_SLOTS = PIPE_DEPTH + 1
ATTN_DEPTH = 1
MIXER_IN_ROWS = 512
CONV_FFN_ROWS = 512
MOD_ROWS = 8
MOD_TILE = 1536
VMEM_LIMIT = 56 * 1024 * 1024

BF16 = jnp.bfloat16
F32 = jnp.float32


def _dot(a, b):
    return jnp.dot(a, b, preferred_element_type=F32)


def _dot_nt(a, b):
    return lax.dot_general(a, b, (((1,), (1,)), ((), ())), preferred_element_type=F32)


def _modulated_norm(x, g, shift, scale):
    ms = jnp.mean(x * x, axis=-1, keepdims=True)
    return (x * lax.rsqrt(ms + EPS) * g) * (1.0 + scale) + shift


def _sigmoid(x):
    return 1.0 / (1.0 + jnp.exp(-x))


def _params(**kwargs):
    return pltpu.CompilerParams(
        dimension_semantics=("arbitrary",), vmem_limit_bytes=VMEM_LIMIT, **kwargs)


def _resident(shape):
    return pl.BlockSpec(shape, lambda i: (0,) * len(shape), pipeline_mode=pl.Buffered(1))


def _mod_kernel(cond_ref, w_ref, b_ref, o_ref):
    cnd = cond_ref[...]
    s = (cnd * _sigmoid(cnd)).astype(BF16)
    o_ref[...] = _dot(s, w_ref[...].astype(BF16)) + b_ref[...]


def _modulation(cond, w_mod, b_mod):
    n = w_mod.shape[1]
    return pl.pallas_call(
        _mod_kernel,
        grid=(n // MOD_TILE,),
        in_specs=[
            pl.BlockSpec((MOD_ROWS, D_MODEL), lambda j: (0, 0)),
            pl.BlockSpec((D_MODEL, MOD_TILE), lambda j: (0, j)),
            pl.BlockSpec((1, MOD_TILE), lambda j: (0, j)),
        ],
        out_specs=pl.BlockSpec((MOD_ROWS, MOD_TILE), lambda j: (0, j)),
        out_shape=jax.ShapeDtypeStruct((MOD_ROWS, n), F32),
        compiler_params=_params(),
        name="modulation",
    )(cond, w_mod, b_mod.reshape(1, n))


def _mixer_in_kernel(x_ref, mod_ref, g_ref, w_ref, qg_ref, kg_ref, *out_refs, emit_kv):
    p_ref, q_ref, k_ref, v_ref, gp_ref, gn_ref = out_refs[:6]
    mod = mod_ref[0]
    h = _modulated_norm(x_ref[...], g_ref[...], mod[:, 0:D_MODEL], mod[:, D_MODEL:2 * D_MODEL])
    h = h.astype(BF16)

    def proj(lo, hi):
        return _dot(h, w_ref[:, lo:hi])

    first_head = lax.broadcasted_iota(jnp.int32, (1, LANES), 1) < HEAD_DIM

    def head_norm(t, g):
        assert 2 * HEAD_DIM == LANES
        tt = t * t
        blocks = []
        for j in range(D_NA // LANES):
            blk = tt[:, j * LANES:(j + 1) * LANES]
            both = jnp.sum(blk, axis=-1, keepdims=True)
            one = jnp.sum(jnp.where(first_head, blk, 0.0), axis=-1, keepdims=True)
            blocks.append(jnp.where(first_head, one, both - one))
        ss = jnp.concatenate(blocks, axis=-1)
        return t * lax.rsqrt(ss * (1.0 / HEAD_DIM) + EPS) * g

    q = proj(OFF_Q, OFF_K)
    k = proj(OFF_K, OFF_V)
    p_ref[...] = proj(0, OFF_Q)
    v = proj(OFF_V, OFF_GP)
    v_ref[...] = v.astype(BF16)
    q = (head_norm(q, qg_ref[...]) * (HEAD_DIM ** -0.5)).astype(BF16)
    gp_ref[...] = proj(OFF_GP, OFF_GN)
    k = head_norm(k, kg_ref[...])
    k_ref[...] = k.astype(BF16)
    gn_ref[...] = proj(OFF_GN, D_IN)
    if emit_kv:
        t = lax.broadcasted_iota(jnp.int32, (TILE, TILE), 0)
        j = lax.broadcasted_iota(jnp.int32, (TILE, TILE), 1)
        n_blocks = TILE // Q_BLOCK
        perm = jnp.where(j == (t % n_blocks) * Q_BLOCK + t // n_blocks, 1.0, 0.0).astype(BF16)
        q = _dot(perm, q).astype(BF16)
    q_ref[...] = q
    if emit_kv:
        nk_ref, nv_ref = out_refs[6:]
        for hd in range(N_HEADS):
            sl = slice(hd * HEAD_DIM, (hd + 1) * HEAD_DIM)
            nk_ref[0, 0, hd] = k[:, sl]
            nv_ref[0, 0, hd] = v[:, sl]


def _mixer_in(x, mod, mod_index, norm_g, w_in, q_g, k_g, *, rows):
    n_tok = x.shape[0]
    tok = lambda i: (i, 0)
    widths = [(D_POOL, F32), (D_NA, BF16), (D_NA, BF16), (D_NA, BF16), (D_MODEL, F32),
              (D_MODEL, F32)]
    return pl.pallas_call(
        functools.partial(_mixer_in_kernel, emit_kv=False),
        grid=(n_tok // rows,),
        in_specs=[
            pl.BlockSpec((rows, D_MODEL), tok),
            pl.BlockSpec((1, 1, 6 * D_MODEL), lambda i: (mod_index(i), 0, 0)),
        ] + [_resident(w.shape) for w in (norm_g, w_in, q_g, k_g)],
        out_specs=[pl.BlockSpec((rows, n), tok) for n, _ in widths],
        out_shape=[jax.ShapeDtypeStruct((n_tok, n), dt) for n, dt in widths],
        compiler_params=_params(),
        name="mixer_in",
    )(x, mod, norm_g, w_in, q_g, k_g)


_POOL_SLABS = pltpu.VMEM((len(POOL_WINDOWS), TILE + 2 * POOL_HALO, LANES), F32)


def _key_row0(tile, n_rows):
    return jnp.clip(tile * TILE_ROWS - WIN_H // 2, 0, n_rows - KEY_ROWS)


def _fill_local_bias(bias_ref, toep_ref, tile, n_rows):
    key_row0 = int(np.clip(tile * TILE_ROWS - WIN_H // 2, 0, n_rows - KEY_ROWS))
    for a in range(TILE_ROWS):
        r = tile * TILE_ROWS + a
        row_start = int(np.clip(r - WIN_H // 2, 0, n_rows - WIN_H))
        for b in range(KEY_ROWS):
            kr = key_row0 + b
            dr = kr - r + (WIN_H - 1) if row_start <= kr < row_start + WIN_H else 2 * WIN_H - 1
            half = slice((b % 2) * GRID_W, (b % 2 + 1) * GRID_W)
            bias_ref[:, a * GRID_W:(a + 1) * GRID_W, b * GRID_W:(b + 1) * GRID_W] = (
                toep_ref[:, dr, :, half])


def _mixer_out_kernel(*refs, latent, tiles_per_seq):
    if latent:
        (x_ref, mod_ref, p_ref, pprev_ref, pnext_ref, q_ref, k_ref, v_ref, gp_ref, gn_ref,
         ck_ref, cv_ref, toep_ref, wpool_ref, spool_ref, wpp_ref, wnp_ref, wo_ref,
         o_ref, na_ref, pslab_ref, bias_ref) = refs
    else:
        (x_ref, mod_ref, p_ref, q_ref, k_ref, v_ref, gp_ref, gn_ref,
         wpool_ref, spool_ref, wpp_ref, wnp_ref, wo_ref, o_ref, na_ref, pslab_ref) = refs
    i = pl.program_id(0)
    t_in_seq = i % tiles_per_seq
    seq_len = TILE * tiles_per_seq

    assert POOL_GROUP == LANES and POOL_HALO >= max(POOL_WINDOWS) // 2
    pos = t_in_seq * TILE + lax.broadcasted_iota(jnp.int32, (TILE, 1), 0)
    mixed = []
    for gi, w in enumerate(POOL_WINDOWS):
        gs = slice(gi * POOL_GROUP, (gi + 1) * POOL_GROUP)
        if latent:
            pslab_ref[gi, 0:POOL_HALO] = jnp.where(t_in_seq > 0, pprev_ref[:, gs], 0.0)
            pslab_ref[gi, POOL_HALO + TILE:] = jnp.where(
                t_in_seq < tiles_per_seq - 1, pnext_ref[:, gs], 0.0)
        else:
            zeros = jnp.zeros((POOL_HALO, LANES), F32)
            pslab_ref[gi, 0:POOL_HALO] = zeros
            pslab_ref[gi, POOL_HALO + TILE:] = zeros
        pslab_ref[gi, POOL_HALO:POOL_HALO + TILE] = p_ref[:, gs]
        sums = sum(pslab_ref[gi, POOL_HALO + d:POOL_HALO + d + TILE]
                   for d in range(-(w // 2), w - w // 2))
        lo = jnp.maximum(pos - w // 2, 0)
        hi = jnp.minimum(pos + (w - w // 2), seq_len)
        pooled = sums / (hi - lo).astype(F32) - p_ref[:, gs]
        mixed.append(_dot(pooled.astype(BF16), wpool_ref[gi]))
    pool_out = jnp.concatenate(mixed, axis=-1) * spool_ref[...]
    gated_pool = _sigmoid(gp_ref[...]) * _dot(pool_out.astype(BF16), wpp_ref[...])

    if latent:
        n_rows = seq_len // GRID_W
        key0 = pl.multiple_of(_key_row0(t_in_seq, n_rows) * GRID_W, TILE)
        for tile in (0, 1, tiles_per_seq - 1):
            pl.when(t_in_seq == tile)(
                functools.partial(_fill_local_bias, bias_ref, toep_ref, tile, n_rows))

    assert 2 * HEAD_DIM == LANES
    lane = lax.broadcasted_iota(jnp.int32, (1, LANES), 1)
    only = [jnp.where(lane < HEAD_DIM, 1.0, 0.0).astype(BF16),
            jnp.where(lane < HEAD_DIM, 0.0, 1.0).astype(BF16)]

    def pair(j):
        return slice(j * LANES, (j + 1) * LANES)

    def scores(j):
        qp = q_ref[:, pair(j)]
        out = []
        for half in range(2):
            qh = qp * only[half]
            if latent:
                s_loc = (_dot_nt(qh, k_ref[0, pl.ds(key0, N_LOCAL), pair(j)])
                         + bias_ref[2 * j + half])
                out.append((s_loc, _dot_nt(qh, ck_ref[0, j])))
            else:
                out.append((_dot_nt(qh, k_ref[0, :, pair(j)]),))
        return out

    def attend(j, halves):
        if latent:
            vals = (v_ref[0, pl.ds(key0, N_LOCAL), pair(j)], cv_ref[0, j])
        else:
            vals = (v_ref[0, :, pair(j)],)
        o = None
        for half, parts in enumerate(halves):
            m = functools.reduce(jnp.maximum, [jnp.max(s, axis=-1, keepdims=True) for s in parts])
            es = [jnp.exp(s - m) for s in parts]
            denom = sum(jnp.sum(e, axis=-1, keepdims=True) for e in es)
            oh = sum(_dot(e.astype(BF16), v * only[half]) for e, v in zip(es, vals)) / denom
            o = oh if o is None else o + oh
        na_ref[:, pair(j)] = o.astype(BF16)

    n_pairs = N_HEADS // 2
    pending = {j: scores(j) for j in range(min(ATTN_DEPTH, n_pairs))}
    for j in range(n_pairs):
        if j + ATTN_DEPTH < n_pairs:
            pending[j + ATTN_DEPTH] = scores(j + ATTN_DEPTH)
        attend(j, pending.pop(j))

    b = _dot(na_ref[...], wnp_ref[...])
    merged = gated_pool + _sigmoid(gn_ref[...]) * b
    y = _dot(merged.astype(BF16), wo_ref[...])
    gate = mod_ref[0][:, 2 * D_MODEL:3 * D_MODEL]
    o_ref[...] = x_ref[...] + gate * y


def _halo_specs(n_tok, rows, halo, width):
    per_tile = rows // halo
    last = n_tok // halo - 1
    return [
        pl.BlockSpec((halo, width), lambda i: (jnp.maximum(i * per_tile - 1, 0), 0)),
        pl.BlockSpec((halo, width), lambda i: (jnp.minimum((i + 1) * per_tile, last), 0)),
    ]


def _mixer_out(x, mod, mod_index, p, q, k, v, gp, gn, cache_k, cache_v, toeplitz,
               w_pool, s_pool, w_pp, w_np, w_o, *, seq_len):
    n_tok = x.shape[0]
    tiles_per_seq = seq_len // TILE
    assert tiles_per_seq >= 3 and seq_len // GRID_W >= KEY_ROWS
    tok = lambda i: (i, 0)
    seq3 = lambda i: (i // tiles_per_seq, 0, 0)
    seq4 = lambda i: (i // tiles_per_seq, 0, 0, 0)
    k = k.reshape(n_tok // seq_len, seq_len, D_NA)
    v = v.reshape(n_tok // seq_len, seq_len, D_NA)
    weights = (toeplitz, w_pool, s_pool, w_pp, w_np, w_o)
    return pl.pallas_call(
        functools.partial(_mixer_out_kernel, latent=True, tiles_per_seq=tiles_per_seq),
        grid=(n_tok // TILE,),
        in_specs=[
            pl.BlockSpec((TILE, D_MODEL), tok),
            pl.BlockSpec((1, 1, 6 * D_MODEL), lambda i: (mod_index(i), 0, 0)),
            pl.BlockSpec((TILE, D_POOL), tok),
        ] + _halo_specs(n_tok, TILE, POOL_HALO, D_POOL) + [
            pl.BlockSpec((TILE, D_NA), tok),
            pl.BlockSpec((1, seq_len, D_NA), seq3),
            pl.BlockSpec((1, seq_len, D_NA), seq3),
            pl.BlockSpec((TILE, D_MODEL), tok),
            pl.BlockSpec((TILE, D_MODEL), tok),
            pl.BlockSpec((1,) + cache_k.shape[1:], seq4),
            pl.BlockSpec((1,) + cache_v.shape[1:], seq4),
        ] + [_resident(w.shape) for w in weights],
        out_specs=pl.BlockSpec((TILE, D_MODEL), tok),
        out_shape=jax.ShapeDtypeStruct((n_tok, D_MODEL), F32),
        scratch_shapes=[
            pltpu.VMEM((TILE, D_NA), BF16),
            _POOL_SLABS,
            pltpu.VMEM((N_HEADS, TILE, N_LOCAL), F32),
        ],
        compiler_params=_params(),
        name="mixer_out",
    )(x, mod, p, p, p, q, k, v, gp, gn, cache_k, cache_v, *weights)


def _ff_slabs(rows):
    return pltpu.VMEM((U_SLOTS, 2, FF_CHUNK // LANES, rows + 2 * CONV_HALO, LANES), F32)


def _conv_ffn_kernel(*refs, halo, tiles_per_seq):
    if halo:
        (x_ref, xprev_ref, xnext_ref, mod_ref, g_ref, wup_ref, cw_ref, cb_ref, wdn_ref,
         o_ref, h_ref, u_ref, acc_ref) = refs
    else:
        (x_ref, mod_ref, g_ref, wup_ref, cw_ref, cb_ref, wdn_ref,
         o_ref, h_ref, u_ref, acc_ref) = refs
    i = pl.program_id(0)
    t_in_seq = i % tiles_per_seq
    rows = x_ref.shape[0]
    mod = mod_ref[0]
    shift = mod[:, 3 * D_MODEL:4 * D_MODEL]
    scale = mod[:, 4 * D_MODEL:5 * D_MODEL]

    if halo:
        x_ext = jnp.concatenate([xprev_ref[...], x_ref[...], xnext_ref[...]], axis=0)
        u_rows = slice(0, rows + 2 * CONV_HALO)
    else:
        x_ext = x_ref[...]
        u_rows = slice(CONV_HALO, CONV_HALO + rows)
        zeros = jnp.zeros(u_ref.shape[:3] + (CONV_HALO, LANES), F32)
        u_ref[:, :, :, 0:CONV_HALO] = zeros
        u_ref[:, :, :, CONV_HALO + rows:] = zeros
    h_ref[...] = _modulated_norm(x_ext, g_ref[...], shift, scale).astype(BF16)

    def cols(c, part):
        lo = part * D_FF + c * FF_CHUNK
        return slice(lo, lo + FF_CHUNK)

    n_slabs = FF_CHUNK // LANES

    def up_proj(c):
        for part in range(2):
            u = _dot(h_ref[...], wup_ref[:, cols(c, part)])
            for k in range(n_slabs):
                u_ref[c % U_SLOTS, part, k, u_rows] = u[:, k * LANES:(k + 1) * LANES]
                if halo:
                    last_prev = CONV_HALO - 1
                    first_next = CONV_HALO + rows
                    u_ref[c % U_SLOTS, part, k, last_prev:last_prev + 1] = jnp.where(
                        t_in_seq == 0, 0.0, u[last_prev:last_prev + 1, k * LANES:(k + 1) * LANES])
                    u_ref[c % U_SLOTS, part, k, first_next:first_next + 1] = jnp.where(
                        t_in_seq == tiles_per_seq - 1, 0.0,
                        u[first_next:first_next + 1, k * LANES:(k + 1) * LANES])

    def conv(c, part):
        slabs = []
        for k in range(n_slabs):
            lo = cols(c, part).start + k * LANES
            sl = slice(lo, lo + LANES)
            u_prev = u_ref[c % U_SLOTS, part, k, CONV_HALO - 1:CONV_HALO - 1 + rows]
            u_here = u_ref[c % U_SLOTS, part, k, CONV_HALO:CONV_HALO + rows]
            u_next = u_ref[c % U_SLOTS, part, k, CONV_HALO + 1:CONV_HALO + 1 + rows]
            slabs.append(u_prev * cw_ref[0:1, sl] + u_here * cw_ref[1:2, sl]
                         + u_next * cw_ref[2:3, sl] + cb_ref[:, sl])
        return jnp.concatenate(slabs, axis=-1)

    def gated_down(c):
        a = conv(c, 0)
        act = (a * _sigmoid(a) * conv(c, 1)).astype(BF16)
        return _dot(act, wdn_ref[c * FF_CHUNK:(c + 1) * FF_CHUNK, :])

    for c in range(min(PIPE_DEPTH, N_FF_CHUNKS)):
        up_proj(c)
    for c in range(N_FF_CHUNKS):
        if c + PIPE_DEPTH < N_FF_CHUNKS:
            up_proj(c + PIPE_DEPTH)
        part = gated_down(c)
        if c == 0:
            acc_ref[...] = part
        elif c + 1 < N_FF_CHUNKS:
            acc_ref[...] += part
        else:
            o_ref[...] = x_ref[...] + mod[:, 5 * D_MODEL:6 * D_MODEL] * (acc_ref[...] + part)


def _conv_ffn(x, mod, mod_index, norm_g, w_up, conv_w, conv_b, w_down, *, seq_len, rows):
    n_tok = x.shape[0]
    tiles_per_seq = seq_len // rows
    assert tiles_per_seq > 1
    tok = lambda i: (i, 0)
    weights = (norm_g, w_up, conv_w, conv_b, w_down)
    return pl.pallas_call(
        functools.partial(_conv_ffn_kernel, halo=True, tiles_per_seq=tiles_per_seq),
        grid=(n_tok // rows,),
        in_specs=[pl.BlockSpec((rows, D_MODEL), tok)] + _halo_specs(n_tok, rows, CONV_HALO, D_MODEL) + [
            pl.BlockSpec((1, 1, 6 * D_MODEL), lambda i: (mod_index(i), 0, 0)),
        ] + [_resident(w.shape) for w in weights],
        out_specs=pl.BlockSpec((rows, D_MODEL), tok),
        out_shape=jax.ShapeDtypeStruct((n_tok, D_MODEL), F32),
        scratch_shapes=[
            pltpu.VMEM((rows + 2 * CONV_HALO, D_MODEL), BF16),
            _ff_slabs(rows),
            pltpu.VMEM((rows, D_MODEL), F32),
        ],
        compiler_params=_params(),
        name="conv_ffn",
    )(x, x, x, mod, *weights)


def _block_seq_kernel(x_ref, mod_ref, gmix_ref, win_ref, qg_ref, kg_ref,
                      wpool_ref, spool_ref, wpp_ref, wnp_ref, wo_ref,
                      gffn_ref, wup_ref, cw_ref, cb_ref, wdn_ref,
                      o_ref, nk_ref, nv_ref,
                      p_ref, q_ref, k_ref, v_ref, gp_ref, gn_ref, na_ref, pslab_ref,
                      x1_ref, h_ref, u_ref, acc_ref):
    _mixer_in_kernel(x_ref, mod_ref, gmix_ref, win_ref, qg_ref, kg_ref,
                     p_ref, q_ref, k_ref.at[0], v_ref.at[0], gp_ref, gn_ref, nk_ref, nv_ref,
                     emit_kv=True)
    _mixer_out_kernel(x_ref, mod_ref, p_ref, q_ref, k_ref, v_ref, gp_ref, gn_ref,
                      wpool_ref, spool_ref, wpp_ref, wnp_ref, wo_ref, x1_ref, na_ref, pslab_ref,
                      latent=False, tiles_per_seq=1)
    _conv_ffn_kernel(x1_ref, mod_ref, gffn_ref, wup_ref, cw_ref, cb_ref, wdn_ref,
                     o_ref, h_ref, u_ref, acc_ref, halo=False, tiles_per_seq=1)


def _block_seq(x, mod, mod_index, g_mix, w_in, q_g, k_g, w_pool, s_pool, w_pp, w_np, w_o,
               g_ffn, w_up, conv_w, conv_b, w_down):
    n_tok = x.shape[0]
    n_tiles = n_tok // TILE
    tok = lambda i: (i, 0)
    kv_shape = jax.ShapeDtypeStruct((n_tiles, 1, N_HEADS, TILE, HEAD_DIM), F32)
    kv_spec = pl.BlockSpec((1, 1, N_HEADS, TILE, HEAD_DIM), lambda i: (i, 0, 0, 0, 0))
    weights = (g_mix, w_in, q_g, k_g, w_pool, s_pool, w_pp, w_np, w_o,
               g_ffn, w_up, conv_w, conv_b, w_down)
    return pl.pallas_call(
        _block_seq_kernel,
        grid=(n_tiles,),
        in_specs=[
            pl.BlockSpec((TILE, D_MODEL), tok),
            pl.BlockSpec((1, 1, 6 * D_MODEL), lambda i: (mod_index(i), 0, 0)),
        ] + [_resident(w.shape) for w in weights],
        out_specs=[pl.BlockSpec((TILE, D_MODEL), tok), kv_spec, kv_spec],
        out_shape=[jax.ShapeDtypeStruct((n_tok, D_MODEL), F32), kv_shape, kv_shape],
        scratch_shapes=[
            pltpu.VMEM((TILE, D_POOL), F32),
            pltpu.VMEM((TILE, D_NA), BF16),
            pltpu.VMEM((1, TILE, D_NA), BF16),
            pltpu.VMEM((1, TILE, D_NA), BF16),
            pltpu.VMEM((TILE, D_MODEL), F32),
            pltpu.VMEM((TILE, D_MODEL), F32),
            pltpu.VMEM((TILE, D_NA), BF16),
            _POOL_SLABS,
            pltpu.VMEM((TILE, D_MODEL), F32),
            pltpu.VMEM((TILE, D_MODEL), BF16),
            _ff_slabs(TILE),
            pltpu.VMEM((TILE, D_MODEL), F32),
        ],
        compiler_params=_params(),
        name="block_seq",
    )(x, mod, *weights)


def _toeplitz_bias(rpb):
    n_dc = 2 * WIN_W - 1
    col = np.arange(GRID_W)
    dc = col[None, :] - col[:, None] + (WIN_W - 1)
    win_start = np.clip(col - WIN_W // 2, 0, GRID_W - WIN_W)
    col_ok = (col[None, :] >= win_start[:, None]) & (col[None, :] < win_start[:, None] + WIN_W)
    col_sel = (dc[None] == np.arange(n_dc)[:, None, None]).astype(np.float32)
    blocks = jnp.einsum("hdc,cqk->hdqk", rpb, col_sel, precision=lax.Precision.HIGHEST)
    blocks = jnp.where(col_ok[None, None], blocks, NEG_INF)
    blocks = jnp.pad(blocks, ((0, 0), (0, 1), (0, 0), (0, 0)), constant_values=NEG_INF)
    return jnp.concatenate([blocks, blocks], axis=-1)


def _pair_heads(t):
    b, _, n, _ = t.shape
    t = t.astype(BF16).reshape(b, N_HEADS // 2, 2, n, HEAD_DIM)
    return t.transpose(0, 1, 3, 2, 4).reshape(b, N_HEADS // 2, n, 2 * HEAD_DIM)


def kernel(x_prompt, x_sample, cache_k, cache_v, c, c_ctx, norm_mix_g, norm_ffn_g, w_mod, b_mod,
           w_in, q_norm_g, k_norm_g, pool_w, pool_scale, na_rpb, w_pool_proj, w_na_proj, w_o,
           w_up, ffn_conv_w, ffn_conv_b, w_down):
    depth = w_in.shape[0]
    assert depth == 1
    batch, seq, _ = x_prompt.shape
    dec_batch, dec_seq, _ = x_sample.shape
    assert seq == TILE and dec_seq % MIXER_IN_ROWS == 0 and dec_batch + 1 <= MOD_ROWS
    l = 0

    cond = jnp.concatenate(
        [c_ctx[None], c, jnp.zeros((MOD_ROWS - 1 - dec_batch, D_MODEL), F32)], axis=0)
    mod = _modulation(cond, w_mod[l], b_mod[l]).reshape(MOD_ROWS, 1, 6 * D_MODEL)

    w_in_b = w_in[l].astype(BF16)
    w_pool_b = pool_w[l].astype(BF16)
    w_pp_b = w_pool_proj[l].astype(BF16)
    w_np_b = w_na_proj[l].astype(BF16)
    w_o_b = w_o[l].astype(BF16)
    w_up_b = w_up[l].astype(BF16)
    w_down_b = w_down[l].astype(BF16)
    conv_w = ffn_conv_w[l]
    g_mix = norm_mix_g[l].reshape(1, D_MODEL)
    g_ffn = norm_ffn_g[l].reshape(1, D_MODEL)
    q_g = jnp.tile(q_norm_g[l], N_HEADS).reshape(1, D_NA)
    k_g = jnp.tile(k_norm_g[l], N_HEADS).reshape(1, D_NA)
    s_pool = pool_scale[l].reshape(1, D_POOL)
    conv_b = ffn_conv_b[l].reshape(1, 2 * D_FF)

    ctx_mod = lambda i: 0
    xc = x_prompt.reshape(batch * seq, D_MODEL)
    xc, new_k, new_v = _block_seq(xc, mod, ctx_mod, g_mix, w_in_b, q_g, k_g,
                                  w_pool_b, s_pool, w_pp_b, w_np_b, w_o_b,
                                  g_ffn, w_up_b, conv_w, conv_b, w_down_b)
    y_prompt = xc.reshape(batch, seq, D_MODEL)

    tiles_per_seq = dec_seq // TILE
    lat_mod = lambda i: 1 + i // tiles_per_seq
    xs = x_sample.reshape(dec_batch * dec_seq, D_MODEL)
    p, q, k, v, gp, gn = _mixer_in(
        xs, mod, lambda i: 1 + i // (dec_seq // MIXER_IN_ROWS), g_mix, w_in_b, q_g, k_g,
        rows=MIXER_IN_ROWS)
    xs = _mixer_out(xs, mod, lat_mod, p, q, k, v, gp, gn,
                    _pair_heads(cache_k[:, l]), _pair_heads(cache_v[:, l]),
                    _toeplitz_bias(na_rpb[l]), w_pool_b, s_pool, w_pp_b, w_np_b, w_o_b,
                    seq_len=dec_seq)
    xs = _conv_ffn(xs, mod, lambda i: 1 + i // (dec_seq // CONV_FFN_ROWS), g_ffn, w_up_b, conv_w, conv_b,
                   w_down_b, seq_len=dec_seq, rows=CONV_FFN_ROWS)
    y_sample = xs.reshape(dec_batch, dec_seq, D_MODEL)
    return (y_prompt, y_sample, new_k, new_v)
```

```python
import functools

import jax
import jax.numpy as jnp
import numpy as np
from jax import lax
from jax.experimental import pallas as pl
from jax.experimental.pallas import tpu as pltpu

D_MODEL = 1024
GRID_W = 64
N_HEADS = 8
HEAD_DIM = 64
D_NA = N_HEADS * HEAD_DIM
D_POOL = 512
POOL_WINDOWS = (2, 4, 8, 16)
POOL_GROUP = D_POOL // len(POOL_WINDOWS)
WIN_H = 8
WIN_W = 16
D_FF = 2816
Q_BLOCK = 128
EPS = 1e-6
NEG_INF = -1e30
LOG2_E = 1.4426950408889634
OFF_Q = D_POOL
OFF_K = D_POOL + D_NA
OFF_V = D_POOL + 2 * D_NA
OFF_GP = D_POOL + 3 * D_NA
OFF_GN = OFF_GP + D_MODEL
D_IN = OFF_GN + D_MODEL

TILE = 256
TILE_ROWS = TILE // GRID_W
KEY_ROWS = 12
N_LOCAL = KEY_ROWS * GRID_W
SUBLANES = 8
LANES = 128
POOL_HALO = SUBLANES
CONV_HALO = SUBLANES
FF_CHUNK = 256
N_FF_CHUNKS = D_FF // FF_CHUNK
PIPE_DEPTH = 3
U_SLOTS = PIPE_DEPTH + 1
ATTN_DEPTH = 1
MIXER_IN_ROWS = 512
CONV_FFN_ROWS = 512
MOD_ROWS = 8
MOD_TILE = 1536
VMEM_LIMIT = 56 * 1024 * 1024

BF16 = jnp.bfloat16
F32 = jnp.float32


def _dot(a, b):
    return jnp.dot(a, b, preferred_element_type=F32)


def _dot_nt(a, b):
    return lax.dot_general(a, b, (((1,), (1,)), ((), ())), preferred_element_type=F32)


def _modulated_norm(x, g, shift, scale):
    ms = jnp.mean(x * x, axis=-1, keepdims=True)
    return (x * lax.rsqrt(ms + EPS) * g) * (1.0 + scale) + shift


def _sigmoid(x):
    return 1.0 / (1.0 + jnp.exp(-x))


def _params(**kwargs):
    return pltpu.CompilerParams(
        dimension_semantics=("arbitrary",), vmem_limit_bytes=VMEM_LIMIT, **kwargs)


def _resident(shape):
    return pl.BlockSpec(shape, lambda i: (0,) * len(shape), pipeline_mode=pl.Buffered(1))


def _mod_kernel(cond_ref, w_ref, b_ref, o_ref):
    cnd = cond_ref[...]
    s = (cnd * _sigmoid(cnd)).astype(BF16)
    o_ref[...] = _dot(s, w_ref[...].astype(BF16)) + b_ref[...]


def _modulation(cond, w_mod, b_mod):
    n = w_mod.shape[1]
    return pl.pallas_call(
        _mod_kernel,
        grid=(n // MOD_TILE,),
        in_specs=[
            pl.BlockSpec((MOD_ROWS, D_MODEL), lambda j: (0, 0)),
            pl.BlockSpec((D_MODEL, MOD_TILE), lambda j: (0, j)),
            pl.BlockSpec((1, MOD_TILE), lambda j: (0, j)),
        ],
        out_specs=pl.BlockSpec((MOD_ROWS, MOD_TILE), lambda j: (0, j)),
        out_shape=jax.ShapeDtypeStruct((MOD_ROWS, n), F32),
        compiler_params=_params(),
        name="modulation",
    )(cond, w_mod, b_mod.reshape(1, n))


def _mixer_in_kernel(x_ref, mod_ref, g_ref, w_ref, qg_ref, kg_ref, *out_refs, emit_kv):
    p_ref, q_ref, k_ref, v_ref, gp_ref, gn_ref = out_refs[:6]
    mod = mod_ref[0]
    h = _modulated_norm(x_ref[...], g_ref[...], mod[:, 0:D_MODEL], mod[:, D_MODEL:2 * D_MODEL])
    h = h.astype(BF16)

    def proj(lo, hi):
        return _dot(h, w_ref[:, lo:hi])

    first_head = lax.broadcasted_iota(jnp.int32, (1, LANES), 1) < HEAD_DIM

    def head_norm(t, g):
        assert 2 * HEAD_DIM == LANES
        tt = t * t
        blocks = []
        for j in range(D_NA // LANES):
            blk = tt[:, j * LANES:(j + 1) * LANES]
            both = jnp.sum(blk, axis=-1, keepdims=True)
            one = jnp.sum(jnp.where(first_head, blk, 0.0), axis=-1, keepdims=True)
            blocks.append(jnp.where(first_head, one, both - one))
        ss = jnp.concatenate(blocks, axis=-1)
        return t * lax.rsqrt(ss * (1.0 / HEAD_DIM) + EPS) * g

    q = proj(OFF_Q, OFF_K)
    k = proj(OFF_K, OFF_V)
    p_ref[...] = proj(0, OFF_Q)
    v = proj(OFF_V, OFF_GP)
    v_ref[...] = v.astype(BF16)
    q = (head_norm(q, qg_ref[...]) * (HEAD_DIM ** -0.5 * LOG2_E)).astype(BF16)
    gp_ref[...] = proj(OFF_GP, OFF_GN)
    k = head_norm(k, kg_ref[...])
    k_ref[...] = k.astype(BF16)
    gn_ref[...] = proj(OFF_GN, D_IN)
    if emit_kv:
        t = lax.broadcasted_iota(jnp.int32, (TILE, TILE), 0)
        j = lax.broadcasted_iota(jnp.int32, (TILE, TILE), 1)
        n_blocks = TILE // Q_BLOCK
        perm = jnp.where(j == (t % n_blocks) * Q_BLOCK + t // n_blocks, 1.0, 0.0).astype(BF16)
        q = _dot(perm, q).astype(BF16)
    q_ref[...] = q
    if emit_kv:
        nk_ref, nv_ref = out_refs[6:]
        for hd in range(N_HEADS):
            sl = slice(hd * HEAD_DIM, (hd + 1) * HEAD_DIM)
            nk_ref[0, 0, hd] = k[:, sl]
            nv_ref[0, 0, hd] = v[:, sl]


def _mixer_in(x, mod, mod_index, norm_g, w_in, q_g, k_g, *, rows):
    n_tok = x.shape[0]
    tok = lambda i: (i, 0)
    widths = [(D_POOL, F32), (D_NA, BF16), (D_NA, BF16), (D_NA, BF16), (D_MODEL, F32),
              (D_MODEL, F32)]
    return pl.pallas_call(
        functools.partial(_mixer_in_kernel, emit_kv=False),
        grid=(n_tok // rows,),
        in_specs=[
            pl.BlockSpec((rows, D_MODEL), tok),
            pl.BlockSpec((1, 1, 6 * D_MODEL), lambda i: (mod_index(i), 0, 0)),
        ] + [_resident(w.shape) for w in (norm_g, w_in, q_g, k_g)],
        out_specs=[pl.BlockSpec((rows, n), tok) for n, _ in widths],
        out_shape=[jax.ShapeDtypeStruct((n_tok, n), dt) for n, dt in widths],
        compiler_params=_params(),
        name="mixer_in",
    )(x, mod, norm_g, w_in, q_g, k_g)


_POOL_SLABS = pltpu.VMEM((len(POOL_WINDOWS), TILE + 2 * POOL_HALO, LANES), F32)


def _key_row0(tile, n_rows):
    return jnp.clip(tile * TILE_ROWS - WIN_H // 2, 0, n_rows - KEY_ROWS)


def _fill_local_bias(bias_ref, toep_ref, tile, n_rows):
    key_row0 = int(np.clip(tile * TILE_ROWS - WIN_H // 2, 0, n_rows - KEY_ROWS))
    for a in range(TILE_ROWS):
        r = tile * TILE_ROWS + a
        row_start = int(np.clip(r - WIN_H // 2, 0, n_rows - WIN_H))
        for b in range(KEY_ROWS):
            kr = key_row0 + b
            dr = kr - r + (WIN_H - 1) if row_start <= kr < row_start + WIN_H else 2 * WIN_H - 1
            half = slice((b % 2) * GRID_W, (b % 2 + 1) * GRID_W)
            bias_ref[:, a * GRID_W:(a + 1) * GRID_W, b * GRID_W:(b + 1) * GRID_W] = (
                toep_ref[:, dr, :, half])


def _mixer_out_kernel(*refs, latent, tiles_per_seq):
    if latent:
        (x_ref, mod_ref, p_ref, pprev_ref, pnext_ref, q_ref, k_ref, v_ref, gp_ref, gn_ref,
         ck_ref, cv_ref, toep_ref, wpool_ref, spool_ref, wpp_ref, wnp_ref, wo_ref,
         o_ref, na_ref, pslab_ref, bias_ref) = refs
    else:
        (x_ref, mod_ref, p_ref, q_ref, k_ref, v_ref, gp_ref, gn_ref,
         wpool_ref, spool_ref, wpp_ref, wnp_ref, wo_ref, o_ref, na_ref, pslab_ref) = refs
    i = pl.program_id(0)
    t_in_seq = i % tiles_per_seq
    seq_len = TILE * tiles_per_seq

    assert POOL_GROUP == LANES and POOL_HALO >= max(POOL_WINDOWS) // 2
    pos = t_in_seq * TILE + lax.broadcasted_iota(jnp.int32, (TILE, 1), 0)
    mixed = []
    for gi, w in enumerate(POOL_WINDOWS):
        gs = slice(gi * POOL_GROUP, (gi + 1) * POOL_GROUP)
        if latent:
            pslab_ref[gi, 0:POOL_HALO] = jnp.where(t_in_seq > 0, pprev_ref[:, gs], 0.0)
            pslab_ref[gi, POOL_HALO + TILE:] = jnp.where(
                t_in_seq < tiles_per_seq - 1, pnext_ref[:, gs], 0.0)
        else:
            zeros = jnp.zeros((POOL_HALO, LANES), F32)
            pslab_ref[gi, 0:POOL_HALO] = zeros
            pslab_ref[gi, POOL_HALO + TILE:] = zeros
        pslab_ref[gi, POOL_HALO:POOL_HALO + TILE] = p_ref[:, gs]
        sums = sum(pslab_ref[gi, POOL_HALO + d:POOL_HALO + d + TILE]
                   for d in range(-(w // 2), w - w // 2))
        lo = jnp.maximum(pos - w // 2, 0)
        hi = jnp.minimum(pos + (w - w // 2), seq_len)
        pooled = sums / (hi - lo).astype(F32) - p_ref[:, gs]
        mixed.append(_dot(pooled.astype(BF16), wpool_ref[gi]))
    pool_out = jnp.concatenate(mixed, axis=-1) * spool_ref[...]
    gated_pool = _sigmoid(gp_ref[...]) * _dot(pool_out.astype(BF16), wpp_ref[...])

    if latent:
        n_rows = seq_len // GRID_W
        key0 = pl.multiple_of(_key_row0(t_in_seq, n_rows) * GRID_W, TILE)
        for tile in (0, 1, tiles_per_seq - 1):
            pl.when(t_in_seq == tile)(
                functools.partial(_fill_local_bias, bias_ref, toep_ref, tile, n_rows))

    assert 2 * HEAD_DIM == LANES
    lane = lax.broadcasted_iota(jnp.int32, (1, LANES), 1)
    only = [jnp.where(lane < HEAD_DIM, 1.0, 0.0).astype(BF16),
            jnp.where(lane < HEAD_DIM, 0.0, 1.0).astype(BF16)]

    def pair(j):
        return slice(j * LANES, (j + 1) * LANES)

    def scores(j):
        qp = q_ref[:, pair(j)]
        out = []
        for half in range(2):
            qh = qp * only[half]
            if latent:
                s_loc = (_dot_nt(qh, k_ref[0, pl.ds(key0, N_LOCAL), pair(j)])
                         + bias_ref[2 * j + half])
                out.append((s_loc, _dot_nt(qh, ck_ref[0, j])))
            else:
                out.append((_dot_nt(qh, k_ref[0, :, pair(j)]),))
        return out

    def attend(j, halves):
        if latent:
            vals = (v_ref[0, pl.ds(key0, N_LOCAL), pair(j)], cv_ref[0, j])
        else:
            vals = (v_ref[0, :, pair(j)],)
        normed = []
        for half, parts in enumerate(halves):
            m = functools.reduce(jnp.maximum, [jnp.max(s, axis=-1, keepdims=True) for s in parts])
            es = [jnp.exp2(s - m).astype(BF16) for s in parts]
            oh = sum(_dot(e, v * only[half] + only[1 - half]) for e, v in zip(es, vals))
            normed.append(oh / pltpu.roll(oh, HEAD_DIM, axis=1))
        na_ref[:, pair(j)] = jnp.where(lane < HEAD_DIM, normed[0], normed[1]).astype(BF16)

    n_pairs = N_HEADS // 2
    pending = {j: scores(j) for j in range(min(ATTN_DEPTH, n_pairs))}
    for j in range(n_pairs):
        if j + ATTN_DEPTH < n_pairs:
            pending[j + ATTN_DEPTH] = scores(j + ATTN_DEPTH)
        attend(j, pending.pop(j))

    b = _dot(na_ref[...], wnp_ref[...])
    merged = gated_pool + _sigmoid(gn_ref[...]) * b
    y = _dot(merged.astype(BF16), wo_ref[...])
    gate = mod_ref[0][:, 2 * D_MODEL:3 * D_MODEL]
    o_ref[...] = x_ref[...] + gate * y


def _halo_specs(n_tok, rows, halo, width):
    per_tile = rows // halo
    last = n_tok // halo - 1
    return [
        pl.BlockSpec((halo, width), lambda i: (jnp.maximum(i * per_tile - 1, 0), 0)),
        pl.BlockSpec((halo, width), lambda i: (jnp.minimum((i + 1) * per_tile, last), 0)),
    ]


def _mixer_out(x, mod, mod_index, p, q, k, v, gp, gn, cache_k, cache_v, toeplitz,
               w_pool, s_pool, w_pp, w_np, w_o, *, seq_len):
    n_tok = x.shape[0]
    tiles_per_seq = seq_len // TILE
    assert tiles_per_seq >= 3 and seq_len // GRID_W >= KEY_ROWS
    tok = lambda i: (i, 0)
    seq3 = lambda i: (i // tiles_per_seq, 0, 0)
    seq4 = lambda i: (i // tiles_per_seq, 0, 0, 0)
    k = k.reshape(n_tok // seq_len, seq_len, D_NA)
    v = v.reshape(n_tok // seq_len, seq_len, D_NA)
    weights = (toeplitz, w_pool, s_pool, w_pp, w_np, w_o)
    return pl.pallas_call(
        functools.partial(_mixer_out_kernel, latent=True, tiles_per_seq=tiles_per_seq),
        grid=(n_tok // TILE,),
        in_specs=[
            pl.BlockSpec((TILE, D_MODEL), tok),
            pl.BlockSpec((1, 1, 6 * D_MODEL), lambda i: (mod_index(i), 0, 0)),
            pl.BlockSpec((TILE, D_POOL), tok),
        ] + _halo_specs(n_tok, TILE, POOL_HALO, D_POOL) + [
            pl.BlockSpec((TILE, D_NA), tok),
            pl.BlockSpec((1, seq_len, D_NA), seq3),
            pl.BlockSpec((1, seq_len, D_NA), seq3),
            pl.BlockSpec((TILE, D_MODEL), tok),
            pl.BlockSpec((TILE, D_MODEL), tok),
            pl.BlockSpec((1,) + cache_k.shape[1:], seq4),
            pl.BlockSpec((1,) + cache_v.shape[1:], seq4),
        ] + [_resident(w.shape) for w in weights],
        out_specs=pl.BlockSpec((TILE, D_MODEL), tok),
        out_shape=jax.ShapeDtypeStruct((n_tok, D_MODEL), F32),
        scratch_shapes=[
            pltpu.VMEM((TILE, D_NA), BF16),
            _POOL_SLABS,
            pltpu.VMEM((N_HEADS, TILE, N_LOCAL), F32),
        ],
        compiler_params=_params(),
        name="mixer_out",
    )(x, mod, p, p, p, q, k, v, gp, gn, cache_k, cache_v, *weights)


def _ff_slabs(rows):
    return pltpu.VMEM((U_SLOTS, 2, FF_CHUNK // LANES, rows + 2 * CONV_HALO, LANES), F32)


def _conv_ffn_kernel(*refs, halo, tiles_per_seq):
    if halo:
        (x_ref, xprev_ref, xnext_ref, mod_ref, g_ref, wup_ref, cw_ref, cb_ref, wdn_ref,
         o_ref, h_ref, u_ref, acc_ref) = refs
    else:
        (x_ref, mod_ref, g_ref, wup_ref, cw_ref, cb_ref, wdn_ref,
         o_ref, h_ref, u_ref, acc_ref) = refs
    i = pl.program_id(0)
    t_in_seq = i % tiles_per_seq
    rows = x_ref.shape[0]
    mod = mod_ref[0]
    shift = mod[:, 3 * D_MODEL:4 * D_MODEL]
    scale = mod[:, 4 * D_MODEL:5 * D_MODEL]

    if halo:
        x_ext = jnp.concatenate([xprev_ref[...], x_ref[...], xnext_ref[...]], axis=0)
        u_rows = slice(0, rows + 2 * CONV_HALO)
    else:
        x_ext = x_ref[...]
        u_rows = slice(CONV_HALO, CONV_HALO + rows)
        zeros = jnp.zeros(u_ref.shape[:3] + (CONV_HALO, LANES), F32)
        u_ref[:, :, :, 0:CONV_HALO] = zeros
        u_ref[:, :, :, CONV_HALO + rows:] = zeros
    h_ref[...] = _modulated_norm(x_ext, g_ref[...], shift, scale).astype(BF16)

    def cols(c, part):
        lo = part * D_FF + c * FF_CHUNK
        return slice(lo, lo + FF_CHUNK)

    n_slabs = FF_CHUNK // LANES

    def up_proj(c):
        for part in range(2):
            u = _dot(h_ref[...], wup_ref[:, cols(c, part)])
            for k in range(n_slabs):
                u_ref[c % U_SLOTS, part, k, u_rows] = u[:, k * LANES:(k + 1) * LANES]
                if halo:
                    last_prev = CONV_HALO - 1
                    first_next = CONV_HALO + rows
                    u_ref[c % U_SLOTS, part, k, last_prev:last_prev + 1] = jnp.where(
                        t_in_seq == 0, 0.0, u[last_prev:last_prev + 1, k * LANES:(k + 1) * LANES])
                    u_ref[c % U_SLOTS, part, k, first_next:first_next + 1] = jnp.where(
                        t_in_seq == tiles_per_seq - 1, 0.0,
                        u[first_next:first_next + 1, k * LANES:(k + 1) * LANES])

    def conv(c, part):
        slabs = []
        for k in range(n_slabs):
            lo = cols(c, part).start + k * LANES
            sl = slice(lo, lo + LANES)
            u_prev = u_ref[c % U_SLOTS, part, k, CONV_HALO - 1:CONV_HALO - 1 + rows]
            u_here = u_ref[c % U_SLOTS, part, k, CONV_HALO:CONV_HALO + rows]
            u_next = u_ref[c % U_SLOTS, part, k, CONV_HALO + 1:CONV_HALO + 1 + rows]
            slabs.append(u_prev * cw_ref[0:1, sl] + u_here * cw_ref[1:2, sl]
                         + u_next * cw_ref[2:3, sl] + cb_ref[:, sl])
        return jnp.concatenate(slabs, axis=-1)

    def gated_down(c):
        a = conv(c, 0)
        act = (a * _sigmoid(a) * conv(c, 1)).astype(BF16)
        return _dot(act, wdn_ref[c * FF_CHUNK:(c + 1) * FF_CHUNK, :])

    for c in range(min(PIPE_DEPTH, N_FF_CHUNKS)):
        up_proj(c)
    for c in range(N_FF_CHUNKS):
        if c + PIPE_DEPTH < N_FF_CHUNKS:
            up_proj(c + PIPE_DEPTH)
        part = gated_down(c)
        if c == 0:
            acc_ref[...] = part
        elif c + 1 < N_FF_CHUNKS:
            acc_ref[...] += part
        else:
            o_ref[...] = x_ref[...] + mod[:, 5 * D_MODEL:6 * D_MODEL] * (acc_ref[...] + part)


def _conv_ffn(x, mod, mod_index, norm_g, w_up, conv_w, conv_b, w_down, *, seq_len, rows):
    n_tok = x.shape[0]
    tiles_per_seq = seq_len // rows
    assert tiles_per_seq > 1
    tok = lambda i: (i, 0)
    weights = (norm_g, w_up, conv_w, conv_b, w_down)
    return pl.pallas_call(
        functools.partial(_conv_ffn_kernel, halo=True, tiles_per_seq=tiles_per_seq),
        grid=(n_tok // rows,),
        in_specs=[pl.BlockSpec((rows, D_MODEL), tok)] + _halo_specs(n_tok, rows, CONV_HALO, D_MODEL) + [
            pl.BlockSpec((1, 1, 6 * D_MODEL), lambda i: (mod_index(i), 0, 0)),
        ] + [_resident(w.shape) for w in weights],
        out_specs=pl.BlockSpec((rows, D_MODEL), tok),
        out_shape=jax.ShapeDtypeStruct((n_tok, D_MODEL), F32),
        scratch_shapes=[
            pltpu.VMEM((rows + 2 * CONV_HALO, D_MODEL), BF16),
            _ff_slabs(rows),
            pltpu.VMEM((rows, D_MODEL), F32),
        ],
        compiler_params=_params(),
        name="conv_ffn",
    )(x, x, x, mod, *weights)


def _block_seq_kernel(x_ref, mod_ref, gmix_ref, win_ref, qg_ref, kg_ref,
                      wpool_ref, spool_ref, wpp_ref, wnp_ref, wo_ref,
                      gffn_ref, wup_ref, cw_ref, cb_ref, wdn_ref,
                      o_ref, nk_ref, nv_ref,
                      p_ref, q_ref, k_ref, v_ref, gp_ref, gn_ref, na_ref, pslab_ref,
                      x1_ref, h_ref, u_ref, acc_ref):
    _mixer_in_kernel(x_ref, mod_ref, gmix_ref, win_ref, qg_ref, kg_ref,
                     p_ref, q_ref, k_ref.at[0], v_ref.at[0], gp_ref, gn_ref, nk_ref, nv_ref,
                     emit_kv=True)
    _mixer_out_kernel(x_ref, mod_ref, p_ref, q_ref, k_ref, v_ref, gp_ref, gn_ref,
                      wpool_ref, spool_ref, wpp_ref, wnp_ref, wo_ref, x1_ref, na_ref, pslab_ref,
                      latent=False, tiles_per_seq=1)
    _conv_ffn_kernel(x1_ref, mod_ref, gffn_ref, wup_ref, cw_ref, cb_ref, wdn_ref,
                     o_ref, h_ref, u_ref, acc_ref, halo=False, tiles_per_seq=1)


def _block_seq(x, mod, mod_index, g_mix, w_in, q_g, k_g, w_pool, s_pool, w_pp, w_np, w_o,
               g_ffn, w_up, conv_w, conv_b, w_down):
    n_tok = x.shape[0]
    n_tiles = n_tok // TILE
    tok = lambda i: (i, 0)
    kv_shape = jax.ShapeDtypeStruct((n_tiles, 1, N_HEADS, TILE, HEAD_DIM), F32)
    kv_spec = pl.BlockSpec((1, 1, N_HEADS, TILE, HEAD_DIM), lambda i: (i, 0, 0, 0, 0))
    weights = (g_mix, w_in, q_g, k_g, w_pool, s_pool, w_pp, w_np, w_o,
               g_ffn, w_up, conv_w, conv_b, w_down)
    return pl.pallas_call(
        _block_seq_kernel,
        grid=(n_tiles,),
        in_specs=[
            pl.BlockSpec((TILE, D_MODEL), tok),
            pl.BlockSpec((1, 1, 6 * D_MODEL), lambda i: (mod_index(i), 0, 0)),
        ] + [_resident(w.shape) for w in weights],
        out_specs=[pl.BlockSpec((TILE, D_MODEL), tok), kv_spec, kv_spec],
        out_shape=[jax.ShapeDtypeStruct((n_tok, D_MODEL), F32), kv_shape, kv_shape],
        scratch_shapes=[
            pltpu.VMEM((TILE, D_POOL), F32),
            pltpu.VMEM((TILE, D_NA), BF16),
            pltpu.VMEM((1, TILE, D_NA), BF16),
            pltpu.VMEM((1, TILE, D_NA), BF16),
            pltpu.VMEM((TILE, D_MODEL), F32),
            pltpu.VMEM((TILE, D_MODEL), F32),
            pltpu.VMEM((TILE, D_NA), BF16),
            _POOL_SLABS,
            pltpu.VMEM((TILE, D_MODEL), F32),
            pltpu.VMEM((TILE, D_MODEL), BF16),
            _ff_slabs(TILE),
            pltpu.VMEM((TILE, D_MODEL), F32),
        ],
        compiler_params=_params(),
        name="block_seq",
    )(x, mod, *weights)


def _toeplitz_bias(rpb):
    n_dc = 2 * WIN_W - 1
    col = np.arange(GRID_W)
    dc = col[None, :] - col[:, None] + (WIN_W - 1)
    win_start = np.clip(col - WIN_W // 2, 0, GRID_W - WIN_W)
    col_ok = (col[None, :] >= win_start[:, None]) & (col[None, :] < win_start[:, None] + WIN_W)
    col_sel = (dc[None] == np.arange(n_dc)[:, None, None]).astype(np.float32)
    blocks = jnp.einsum("hdc,cqk->hdqk", rpb, col_sel, precision=lax.Precision.HIGHEST)
    blocks = jnp.where(col_ok[None, None], blocks * LOG2_E, NEG_INF)
    blocks = jnp.pad(blocks, ((0, 0), (0, 1), (0, 0), (0, 0)), constant_values=NEG_INF)
    return jnp.concatenate([blocks, blocks], axis=-1)


def _pair_heads(t):
    b, _, n, _ = t.shape
    t = t.astype(BF16).reshape(b, N_HEADS // 2, 2, n, HEAD_DIM)
    return t.transpose(0, 1, 3, 2, 4).reshape(b, N_HEADS // 2, n, 2 * HEAD_DIM)


def kernel(x_prompt, x_sample, cache_k, cache_v, c, c_ctx, norm_mix_g, norm_ffn_g, w_mod, b_mod,
           w_in, q_norm_g, k_norm_g, pool_w, pool_scale, na_rpb, w_pool_proj, w_na_proj, w_o,
           w_up, ffn_conv_w, ffn_conv_b, w_down):
    depth = w_in.shape[0]
    assert depth == 1
    batch, seq, _ = x_prompt.shape
    dec_batch, dec_seq, _ = x_sample.shape
    assert seq == TILE and dec_seq % MIXER_IN_ROWS == 0 and dec_batch + 1 <= MOD_ROWS
    l = 0

    cond = jnp.concatenate(
        [c_ctx[None], c, jnp.zeros((MOD_ROWS - 1 - dec_batch, D_MODEL), F32)], axis=0)
    mod = _modulation(cond, w_mod[l], b_mod[l]).reshape(MOD_ROWS, 1, 6 * D_MODEL)

    w_in_b = w_in[l].astype(BF16)
    w_pool_b = pool_w[l].astype(BF16)
    w_pp_b = w_pool_proj[l].astype(BF16)
    w_np_b = w_na_proj[l].astype(BF16)
    w_o_b = w_o[l].astype(BF16)
    w_up_b = w_up[l].astype(BF16)
    w_down_b = w_down[l].astype(BF16)
    conv_w = ffn_conv_w[l]
    g_mix = norm_mix_g[l].reshape(1, D_MODEL)
    g_ffn = norm_ffn_g[l].reshape(1, D_MODEL)
    q_g = jnp.tile(q_norm_g[l], N_HEADS).reshape(1, D_NA)
    k_g = jnp.tile(k_norm_g[l], N_HEADS).reshape(1, D_NA)
    s_pool = pool_scale[l].reshape(1, D_POOL)
    conv_b = ffn_conv_b[l].reshape(1, 2 * D_FF)

    ctx_mod = lambda i: 0
    xc = x_prompt.reshape(batch * seq, D_MODEL)
    xc, new_k, new_v = _block_seq(xc, mod, ctx_mod, g_mix, w_in_b, q_g, k_g,
                                  w_pool_b, s_pool, w_pp_b, w_np_b, w_o_b,
                                  g_ffn, w_up_b, conv_w, conv_b, w_down_b)
    y_prompt = xc.reshape(batch, seq, D_MODEL)

    tiles_per_seq = dec_seq // TILE
    lat_mod = lambda i: 1 + i // tiles_per_seq
    xs = x_sample.reshape(dec_batch * dec_seq, D_MODEL)
    p, q, k, v, gp, gn = _mixer_in(
        xs, mod, lambda i: 1 + i // (dec_seq // MIXER_IN_ROWS), g_mix, w_in_b, q_g, k_g,
        rows=MIXER_IN_ROWS)
    xs = _mixer_out(xs, mod, lat_mod, p, q, k, v, gp, gn,
                    _pair_heads(cache_k[:, l]), _pair_heads(cache_v[:, l]),
                    _toeplitz_bias(na_rpb[l]), w_pool_b, s_pool, w_pp_b, w_np_b, w_o_b,
                    seq_len=dec_seq)
    xs = _conv_ffn(xs, mod, lambda i: 1 + i // (dec_seq // CONV_FFN_ROWS), g_ffn, w_up_b, conv_w, conv_b,
                   w_down_b, seq_len=dec_seq, rows=CONV_FFN_ROWS)
    y_sample = xs.reshape(dec_batch, dec_seq, D_MODEL)
    return (y_prompt, y_sample, new_k, new_v)
```

```python
import functools

import jax
import jax.numpy as jnp
import numpy as np
from jax import lax
from jax.experimental import pallas as pl
from jax.experimental.pallas import tpu as pltpu

D_MODEL = 1024
GRID_W = 64
N_HEADS = 8
HEAD_DIM = 64
D_NA = N_HEADS * HEAD_DIM
D_POOL = 512
POOL_WINDOWS = (2, 4, 8, 16)
POOL_GROUP = D_POOL // len(POOL_WINDOWS)
WIN_H = 8
WIN_W = 16
D_FF = 2816
Q_BLOCK = 128
EPS = 1e-6
NEG_INF = -1e30
LOG2_E = 1.4426950408889634
OFF_Q = D_POOL
OFF_K = D_POOL + D_NA
OFF_V = D_POOL + 2 * D_NA
OFF_GP = D_POOL + 3 * D_NA
OFF_GN = OFF_GP + D_MODEL
D_IN = OFF_GN + D_MODEL

TILE = 256
TILE_ROWS = TILE // GRID_W
KEY_ROWS = 12
N_LOCAL = KEY_ROWS * GRID_W
SUBLANES = 8
LANES = 128
POOL_HALO = SUBLANES
CONV_HALO = SUBLANES
FF_CHUNK = 256
N_FF_CHUNKS = D_FF // FF_CHUNK
PIPE_DEPTH = 3
U_SLOTS = PIPE_DEPTH + 1
ATTN_DEPTH = 1
MIXER_IN_ROWS = 512
CONV_FFN_ROWS = 512
MOD_ROWS = 8
MOD_TILE = 1536
VMEM_LIMIT = 56 * 1024 * 1024

BF16 = jnp.bfloat16
F32 = jnp.float32


def _dot(a, b):
    return jnp.dot(a, b, preferred_element_type=F32)


def _dot_nt(a, b):
    return lax.dot_general(a, b, (((1,), (1,)), ((), ())), preferred_element_type=F32)


def _modulated_norm(x, g, shift, scale):
    ms = jnp.mean(x * x, axis=-1, keepdims=True)
    return (x * lax.rsqrt(ms + EPS) * g) * (1.0 + scale) + shift


def _sigmoid(x):
    return 1.0 / (1.0 + jnp.exp(-x))


def _mod_row(mod_ref, steps_per_cond):
    if steps_per_cond is None:
        return mod_ref[0:1, :]
    return mod_ref[pl.ds(1 + pl.program_id(0) // steps_per_cond, 1), :]


def _params(**kwargs):
    return pltpu.CompilerParams(
        dimension_semantics=("arbitrary",), vmem_limit_bytes=VMEM_LIMIT, **kwargs)


def _resident(shape):
    return pl.BlockSpec(shape, lambda i: (0,) * len(shape), pipeline_mode=pl.Buffered(1))


def _mod_kernel(cond_ref, w_ref, b_ref, o_ref):
    cnd = cond_ref[...]
    s = (cnd * _sigmoid(cnd)).astype(BF16)
    o_ref[...] = _dot(s, w_ref[...].astype(BF16)) + b_ref[...]


def _modulation(cond, w_mod, b_mod):
    n = w_mod.shape[1]
    return pl.pallas_call(
        _mod_kernel,
        grid=(n // MOD_TILE,),
        in_specs=[
            pl.BlockSpec((MOD_ROWS, D_MODEL), lambda j: (0, 0)),
            pl.BlockSpec((D_MODEL, MOD_TILE), lambda j: (0, j)),
            pl.BlockSpec((1, MOD_TILE), lambda j: (0, j)),
        ],
        out_specs=pl.BlockSpec((MOD_ROWS, MOD_TILE), lambda j: (0, j)),
        out_shape=jax.ShapeDtypeStruct((MOD_ROWS, n), F32),
        compiler_params=_params(),
        name="modulation",
    )(cond, w_mod, b_mod.reshape(1, n))


def _mixer_in_kernel(x_ref, mod_ref, g_ref, w_ref, qg_ref, kg_ref, *out_refs, emit_kv,
                     steps_per_cond):
    p_ref, q_ref, k_ref, v_ref, gp_ref, gn_ref = out_refs[:6]
    mod = _mod_row(mod_ref, steps_per_cond)
    h = _modulated_norm(x_ref[...], g_ref[...], mod[:, 0:D_MODEL], mod[:, D_MODEL:2 * D_MODEL])
    h = h.astype(BF16)

    def proj(lo, hi):
        return _dot(h, w_ref[:, lo:hi])

    first_head = lax.broadcasted_iota(jnp.int32, (1, LANES), 1) < HEAD_DIM

    def head_norm(t, g):
        assert 2 * HEAD_DIM == LANES
        tt = t * t
        blocks = []
        for j in range(D_NA // LANES):
            blk = tt[:, j * LANES:(j + 1) * LANES]
            both = jnp.sum(blk, axis=-1, keepdims=True)
            one = jnp.sum(jnp.where(first_head, blk, 0.0), axis=-1, keepdims=True)
            blocks.append(jnp.where(first_head, one, both - one))
        ss = jnp.concatenate(blocks, axis=-1)
        return t * lax.rsqrt(ss * (1.0 / HEAD_DIM) + EPS) * g

    q = proj(OFF_Q, OFF_K)
    k = proj(OFF_K, OFF_V)
    p_ref[...] = proj(0, OFF_Q)
    v = proj(OFF_V, OFF_GP)
    v_ref[...] = v.astype(BF16)
    q = (head_norm(q, qg_ref[...]) * (HEAD_DIM ** -0.5 * LOG2_E)).astype(BF16)
    gp_ref[...] = proj(OFF_GP, OFF_GN)
    k = head_norm(k, kg_ref[...])
    k_ref[...] = k.astype(BF16)
    gn_ref[...] = proj(OFF_GN, D_IN)
    if emit_kv:
        t = lax.broadcasted_iota(jnp.int32, (TILE, TILE), 0)
        j = lax.broadcasted_iota(jnp.int32, (TILE, TILE), 1)
        n_blocks = TILE // Q_BLOCK
        perm = jnp.where(j == (t % n_blocks) * Q_BLOCK + t // n_blocks, 1.0, 0.0).astype(BF16)
        q = _dot(perm, q).astype(BF16)
    q_ref[...] = q
    if emit_kv:
        nk_ref, nv_ref = out_refs[6:]
        for hd in range(N_HEADS):
            sl = slice(hd * HEAD_DIM, (hd + 1) * HEAD_DIM)
            nk_ref[0, 0, hd] = k[:, sl]
            nv_ref[0, 0, hd] = v[:, sl]


def _mixer_in(x, mod, norm_g, w_in, q_g, k_g, *, seq_len, rows):
    n_tok = x.shape[0]
    tok = lambda i: (i, 0)
    widths = [(D_POOL, F32), (D_NA, BF16), (D_NA, BF16), (D_NA, BF16), (D_MODEL, F32),
              (D_MODEL, F32)]
    return pl.pallas_call(
        functools.partial(_mixer_in_kernel, emit_kv=False, steps_per_cond=seq_len // rows),
        grid=(n_tok // rows,),
        in_specs=[
            pl.BlockSpec((rows, D_MODEL), tok),
        ] + [_resident(w.shape) for w in (mod, norm_g, w_in, q_g, k_g)],
        out_specs=[pl.BlockSpec((rows, n), tok) for n, _ in widths],
        out_shape=[jax.ShapeDtypeStruct((n_tok, n), dt) for n, dt in widths],
        compiler_params=_params(),
        name="mixer_in",
    )(x, mod, norm_g, w_in, q_g, k_g)


_POOL_SLABS = pltpu.VMEM((len(POOL_WINDOWS), TILE + 2 * POOL_HALO, LANES), F32)


def _key_row0(tile, n_rows):
    return jnp.clip(tile * TILE_ROWS - WIN_H // 2, 0, n_rows - KEY_ROWS)


def _fill_local_bias(bias_ref, toep_ref, tile, n_rows):
    key_row0 = int(np.clip(tile * TILE_ROWS - WIN_H // 2, 0, n_rows - KEY_ROWS))
    for a in range(TILE_ROWS):
        r = tile * TILE_ROWS + a
        row_start = int(np.clip(r - WIN_H // 2, 0, n_rows - WIN_H))
        for b in range(KEY_ROWS):
            kr = key_row0 + b
            dr = kr - r + (WIN_H - 1) if row_start <= kr < row_start + WIN_H else 2 * WIN_H - 1
            half = slice((b % 2) * GRID_W, (b % 2 + 1) * GRID_W)
            bias_ref[:, a * GRID_W:(a + 1) * GRID_W, b * GRID_W:(b + 1) * GRID_W] = (
                toep_ref[:, dr, :, half])


def _mixer_out_kernel(*refs, latent, tiles_per_seq):
    if latent:
        (x_ref, mod_ref, p_ref, pprev_ref, pnext_ref, q_ref, k_ref, v_ref, gp_ref, gn_ref,
         ck_ref, cv_ref, toep_ref, wpool_ref, spool_ref, wpp_ref, wnp_ref, wo_ref,
         o_ref, na_ref, pslab_ref, bias_ref) = refs
    else:
        (x_ref, mod_ref, p_ref, q_ref, k_ref, v_ref, gp_ref, gn_ref,
         wpool_ref, spool_ref, wpp_ref, wnp_ref, wo_ref, o_ref, na_ref, pslab_ref) = refs
    i = pl.program_id(0)
    t_in_seq = i % tiles_per_seq
    seq_len = TILE * tiles_per_seq

    if latent:
        n_rows = seq_len // GRID_W
        key0 = pl.multiple_of(_key_row0(t_in_seq, n_rows) * GRID_W, TILE)
        for tile in (0, 1, tiles_per_seq - 1):
            pl.when(t_in_seq == tile)(
                functools.partial(_fill_local_bias, bias_ref, toep_ref, tile, n_rows))

    assert 2 * HEAD_DIM == LANES
    lane = lax.broadcasted_iota(jnp.int32, (1, LANES), 1)
    only = [jnp.where(lane < HEAD_DIM, 1.0, 0.0).astype(BF16),
            jnp.where(lane < HEAD_DIM, 0.0, 1.0).astype(BF16)]

    def pair(j):
        return slice(j * LANES, (j + 1) * LANES)

    def scores(j):
        qp = q_ref[:, pair(j)]
        out = []
        for half in range(2):
            qh = qp * only[half]
            if latent:
                s_loc = (_dot_nt(qh, k_ref[0, pl.ds(key0, N_LOCAL), pair(j)])
                         + bias_ref[2 * j + half])
                out.append((s_loc, _dot_nt(qh, ck_ref[0, j])))
            else:
                out.append((_dot_nt(qh, k_ref[0, :, pair(j)]),))
        return out

    def attend(j, halves):
        if latent:
            vals = (v_ref[0, pl.ds(key0, N_LOCAL), pair(j)], cv_ref[0, j])
        else:
            vals = (v_ref[0, :, pair(j)],)
        normed = []
        for half, parts in enumerate(halves):
            m = functools.reduce(jnp.maximum, [jnp.max(s, axis=-1, keepdims=True) for s in parts])
            es = [jnp.exp2(s - m) for s in parts]
            if latent:
                oh = sum(_dot(e.astype(BF16), v * only[half] + only[1 - half])
                         for e, v in zip(es, vals))
                normed.append(oh / pltpu.roll(oh, HEAD_DIM, axis=1))
            else:
                denom = sum(jnp.sum(e, axis=-1, keepdims=True) for e in es)
                oh = sum(_dot(e.astype(BF16), v * only[half]) for e, v in zip(es, vals))
                normed.append(oh / denom)
        na_ref[:, pair(j)] = jnp.where(lane < HEAD_DIM, normed[0], normed[1]).astype(BF16)

    n_pairs = N_HEADS // 2
    pending = {j: scores(j) for j in range(min(ATTN_DEPTH, n_pairs))}

    assert POOL_GROUP == LANES and POOL_HALO >= max(POOL_WINDOWS) // 2
    pos = t_in_seq * TILE + lax.broadcasted_iota(jnp.int32, (TILE, 1), 0)
    mixed = []
    for gi, w in enumerate(POOL_WINDOWS):
        gs = slice(gi * POOL_GROUP, (gi + 1) * POOL_GROUP)
        if latent:
            pslab_ref[gi, 0:POOL_HALO] = jnp.where(t_in_seq > 0, pprev_ref[:, gs], 0.0)
            pslab_ref[gi, POOL_HALO + TILE:] = jnp.where(
                t_in_seq < tiles_per_seq - 1, pnext_ref[:, gs], 0.0)
        else:
            zeros = jnp.zeros((POOL_HALO, LANES), F32)
            pslab_ref[gi, 0:POOL_HALO] = zeros
            pslab_ref[gi, POOL_HALO + TILE:] = zeros
        pslab_ref[gi, POOL_HALO:POOL_HALO + TILE] = p_ref[:, gs]
        sums = sum(pslab_ref[gi, POOL_HALO + d:POOL_HALO + d + TILE]
                   for d in range(-(w // 2), w - w // 2))
        lo = jnp.maximum(pos - w // 2, 0)
        hi = jnp.minimum(pos + (w - w // 2), seq_len)
        pooled = sums / (hi - lo).astype(F32) - p_ref[:, gs]
        mixed.append(_dot(pooled.astype(BF16), wpool_ref[gi]))
    pool_out = jnp.concatenate(mixed, axis=-1) * spool_ref[...]
    gated_pool = _sigmoid(gp_ref[...]) * _dot(pool_out.astype(BF16), wpp_ref[...])

    for j in range(n_pairs):
        if j + ATTN_DEPTH < n_pairs:
            pending[j + ATTN_DEPTH] = scores(j + ATTN_DEPTH)
        attend(j, pending.pop(j))

    b = _dot(na_ref[...], wnp_ref[...])
    merged = gated_pool + _sigmoid(gn_ref[...]) * b
    y = _dot(merged.astype(BF16), wo_ref[...])
    gate = _mod_row(mod_ref, tiles_per_seq if latent else None)[:, 2 * D_MODEL:3 * D_MODEL]
    o_ref[...] = x_ref[...] + gate * y


def _halo_specs(n_tok, rows, halo, width):
    per_tile = rows // halo
    last = n_tok // halo - 1
    return [
        pl.BlockSpec((halo, width), lambda i: (jnp.maximum(i * per_tile - 1, 0), 0)),
        pl.BlockSpec((halo, width), lambda i: (jnp.minimum((i + 1) * per_tile, last), 0)),
    ]


def _mixer_out(x, mod, p, q, k, v, gp, gn, cache_k, cache_v, toeplitz,
               w_pool, s_pool, w_pp, w_np, w_o, *, seq_len):
    n_tok = x.shape[0]
    tiles_per_seq = seq_len // TILE
    assert tiles_per_seq >= 3 and seq_len // GRID_W >= KEY_ROWS
    tok = lambda i: (i, 0)
    seq3 = lambda i: (i // tiles_per_seq, 0, 0)
    seq4 = lambda i: (i // tiles_per_seq, 0, 0, 0)
    k = k.reshape(n_tok // seq_len, seq_len, D_NA)
    v = v.reshape(n_tok // seq_len, seq_len, D_NA)
    weights = (toeplitz, w_pool, s_pool, w_pp, w_np, w_o)
    return pl.pallas_call(
        functools.partial(_mixer_out_kernel, latent=True, tiles_per_seq=tiles_per_seq),
        grid=(n_tok // TILE,),
        in_specs=[
            pl.BlockSpec((TILE, D_MODEL), tok),
            _resident(mod.shape),
            pl.BlockSpec((TILE, D_POOL), tok),
        ] + _halo_specs(n_tok, TILE, POOL_HALO, D_POOL) + [
            pl.BlockSpec((TILE, D_NA), tok),
            pl.BlockSpec((1, seq_len, D_NA), seq3),
            pl.BlockSpec((1, seq_len, D_NA), seq3),
            pl.BlockSpec((TILE, D_MODEL), tok),
            pl.BlockSpec((TILE, D_MODEL), tok),
            pl.BlockSpec((1,) + cache_k.shape[1:], seq4),
            pl.BlockSpec((1,) + cache_v.shape[1:], seq4),
        ] + [_resident(w.shape) for w in weights],
        out_specs=pl.BlockSpec((TILE, D_MODEL), tok),
        out_shape=jax.ShapeDtypeStruct((n_tok, D_MODEL), F32),
        scratch_shapes=[
            pltpu.VMEM((TILE, D_NA), BF16),
            _POOL_SLABS,
            pltpu.VMEM((N_HEADS, TILE, N_LOCAL), F32),
        ],
        compiler_params=_params(),
        name="mixer_out",
    )(x, mod, p, p, p, q, k, v, gp, gn, cache_k, cache_v, *weights)


def _ff_slabs(rows):
    return pltpu.VMEM((U_SLOTS, 2, FF_CHUNK // LANES, rows + 2 * CONV_HALO, LANES), F32)


def _conv_ffn_kernel(*refs, halo, tiles_per_seq):
    if halo:
        (x_ref, xprev_ref, xnext_ref, mod_ref, g_ref, wup_ref, cw_ref, cb_ref, wdn_ref,
         o_ref, h_ref, u_ref, acc_ref) = refs
    else:
        (x_ref, mod_ref, g_ref, wup_ref, cw_ref, cb_ref, wdn_ref,
         o_ref, h_ref, u_ref, acc_ref) = refs
    i = pl.program_id(0)
    t_in_seq = i % tiles_per_seq
    rows = x_ref.shape[0]
    mod = _mod_row(mod_ref, tiles_per_seq if halo else None)
    shift = mod[:, 3 * D_MODEL:4 * D_MODEL]
    scale = mod[:, 4 * D_MODEL:5 * D_MODEL]

    if halo:
        x_ext = jnp.concatenate([xprev_ref[...], x_ref[...], xnext_ref[...]], axis=0)
        u_rows = slice(0, rows + 2 * CONV_HALO)
    else:
        x_ext = x_ref[...]
        u_rows = slice(CONV_HALO, CONV_HALO + rows)
        zeros = jnp.zeros(u_ref.shape[:3] + (CONV_HALO, LANES), F32)
        u_ref[:, :, :, 0:CONV_HALO] = zeros
        u_ref[:, :, :, CONV_HALO + rows:] = zeros
    h_ref[...] = _modulated_norm(x_ext, g_ref[...], shift, scale).astype(BF16)

    def cols(c, part):
        lo = part * D_FF + c * FF_CHUNK
        return slice(lo, lo + FF_CHUNK)

    n_slabs = FF_CHUNK // LANES

    def up_proj(c):
        for part in range(2):
            u = _dot(h_ref[...], wup_ref[:, cols(c, part)])
            for k in range(n_slabs):
                u_ref[c % U_SLOTS, part, k, u_rows] = u[:, k * LANES:(k + 1) * LANES]
                if halo:
                    last_prev = CONV_HALO - 1
                    first_next = CONV_HALO + rows
                    u_ref[c % U_SLOTS, part, k, last_prev:last_prev + 1] = jnp.where(
                        t_in_seq == 0, 0.0, u[last_prev:last_prev + 1, k * LANES:(k + 1) * LANES])
                    u_ref[c % U_SLOTS, part, k, first_next:first_next + 1] = jnp.where(
                        t_in_seq == tiles_per_seq - 1, 0.0,
                        u[first_next:first_next + 1, k * LANES:(k + 1) * LANES])

    def conv(c, part):
        slabs = []
        for k in range(n_slabs):
            lo = cols(c, part).start + k * LANES
            sl = slice(lo, lo + LANES)
            u_prev = u_ref[c % U_SLOTS, part, k, CONV_HALO - 1:CONV_HALO - 1 + rows]
            u_here = u_ref[c % U_SLOTS, part, k, CONV_HALO:CONV_HALO + rows]
            u_next = u_ref[c % U_SLOTS, part, k, CONV_HALO + 1:CONV_HALO + 1 + rows]
            slabs.append(u_prev * cw_ref[0:1, sl] + u_here * cw_ref[1:2, sl]
                         + u_next * cw_ref[2:3, sl] + cb_ref[:, sl])
        return jnp.concatenate(slabs, axis=-1)

    def gated_down(c):
        a = conv(c, 0)
        act = (a * _sigmoid(a) * conv(c, 1)).astype(BF16)
        return _dot(act, wdn_ref[c * FF_CHUNK:(c + 1) * FF_CHUNK, :])

    for c in range(min(PIPE_DEPTH, N_FF_CHUNKS)):
        up_proj(c)
    for c in range(N_FF_CHUNKS):
        if c + PIPE_DEPTH < N_FF_CHUNKS:
            up_proj(c + PIPE_DEPTH)
        part = gated_down(c)
        if c == 0:
            acc_ref[...] = part
        elif c + 1 < N_FF_CHUNKS:
            acc_ref[...] += part
        else:
            o_ref[...] = x_ref[...] + mod[:, 5 * D_MODEL:6 * D_MODEL] * (acc_ref[...] + part)


def _conv_ffn(x, mod, norm_g, w_up, conv_w, conv_b, w_down, *, seq_len, rows):
    n_tok = x.shape[0]
    tiles_per_seq = seq_len // rows
    assert tiles_per_seq > 1
    tok = lambda i: (i, 0)
    weights = (norm_g, w_up, conv_w, conv_b, w_down)
    return pl.pallas_call(
        functools.partial(_conv_ffn_kernel, halo=True, tiles_per_seq=tiles_per_seq),
        grid=(n_tok // rows,),
        in_specs=[pl.BlockSpec((rows, D_MODEL), tok)] + _halo_specs(n_tok, rows, CONV_HALO, D_MODEL) + [
            _resident(w.shape) for w in (mod,) + weights],
        out_specs=pl.BlockSpec((rows, D_MODEL), tok),
        out_shape=jax.ShapeDtypeStruct((n_tok, D_MODEL), F32),
        scratch_shapes=[
            pltpu.VMEM((rows + 2 * CONV_HALO, D_MODEL), BF16),
            _ff_slabs(rows),
            pltpu.VMEM((rows, D_MODEL), F32),
        ],
        compiler_params=_params(),
        name="conv_ffn",
    )(x, x, x, mod, *weights)


def _block_seq_kernel(x_ref, mod_ref, gmix_ref, win_ref, qg_ref, kg_ref,
                      wpool_ref, spool_ref, wpp_ref, wnp_ref, wo_ref,
                      gffn_ref, wup_ref, cw_ref, cb_ref, wdn_ref,
                      o_ref, nk_ref, nv_ref,
                      p_ref, q_ref, k_ref, v_ref, gp_ref, gn_ref, na_ref, pslab_ref,
                      x1_ref, h_ref, u_ref, acc_ref):
    _mixer_in_kernel(x_ref, mod_ref, gmix_ref, win_ref, qg_ref, kg_ref,
                     p_ref, q_ref, k_ref.at[0], v_ref.at[0], gp_ref, gn_ref, nk_ref, nv_ref,
                     emit_kv=True, steps_per_cond=None)
    _mixer_out_kernel(x_ref, mod_ref, p_ref, q_ref, k_ref, v_ref, gp_ref, gn_ref,
                      wpool_ref, spool_ref, wpp_ref, wnp_ref, wo_ref, x1_ref, na_ref, pslab_ref,
                      latent=False, tiles_per_seq=1)
    _conv_ffn_kernel(x1_ref, mod_ref, gffn_ref, wup_ref, cw_ref, cb_ref, wdn_ref,
                     o_ref, h_ref, u_ref, acc_ref, halo=False, tiles_per_seq=1)


def _block_seq(x, mod, g_mix, w_in, q_g, k_g, w_pool, s_pool, w_pp, w_np, w_o,
               g_ffn, w_up, conv_w, conv_b, w_down):
    n_tok = x.shape[0]
    n_tiles = n_tok // TILE
    tok = lambda i: (i, 0)
    kv_shape = jax.ShapeDtypeStruct((n_tiles, 1, N_HEADS, TILE, HEAD_DIM), F32)
    kv_spec = pl.BlockSpec((1, 1, N_HEADS, TILE, HEAD_DIM), lambda i: (i, 0, 0, 0, 0))
    weights = (g_mix, w_in, q_g, k_g, w_pool, s_pool, w_pp, w_np, w_o,
               g_ffn, w_up, conv_w, conv_b, w_down)
    return pl.pallas_call(
        _block_seq_kernel,
        grid=(n_tiles,),
        in_specs=[
            pl.BlockSpec((TILE, D_MODEL), tok),
        ] + [_resident(w.shape) for w in (mod,) + weights],
        out_specs=[pl.BlockSpec((TILE, D_MODEL), tok), kv_spec, kv_spec],
        out_shape=[jax.ShapeDtypeStruct((n_tok, D_MODEL), F32), kv_shape, kv_shape],
        scratch_shapes=[
            pltpu.VMEM((TILE, D_POOL), F32),
            pltpu.VMEM((TILE, D_NA), BF16),
            pltpu.VMEM((1, TILE, D_NA), BF16),
            pltpu.VMEM((1, TILE, D_NA), BF16),
            pltpu.VMEM((TILE, D_MODEL), F32),
            pltpu.VMEM((TILE, D_MODEL), F32),
            pltpu.VMEM((TILE, D_NA), BF16),
            _POOL_SLABS,
            pltpu.VMEM((TILE, D_MODEL), F32),
            pltpu.VMEM((TILE, D_MODEL), BF16),
            _ff_slabs(TILE),
            pltpu.VMEM((TILE, D_MODEL), F32),
        ],
        compiler_params=_params(),
        name="block_seq",
    )(x, mod, *weights)


def _toeplitz_bias(rpb):
    n_dr, n_dc = 2 * WIN_H - 1, 2 * WIN_W - 1
    col = np.arange(GRID_W)
    dc = col[None, :] - col[:, None] + (WIN_W - 1)
    win_start = np.clip(col - WIN_W // 2, 0, GRID_W - WIN_W)
    col_ok = (col[None, :] >= win_start[:, None]) & (col[None, :] < win_start[:, None] + WIN_W)
    col_sel = (dc[None] == np.arange(n_dc)[:, None, None]).astype(np.float32)
    col_sel = np.concatenate([col_sel, col_sel], axis=-1)
    valid = np.concatenate([col_ok, col_ok], axis=-1)[None] & (np.arange(n_dr + 1) < n_dr)[:, None, None]
    rows = jnp.pad(rpb, ((0, 0), (0, 1), (0, 0)))
    blocks = jnp.einsum("hdc,cqk->hdqk", rows, col_sel, precision=lax.Precision.HIGHEST)
    return jnp.where(valid[None], blocks * LOG2_E, NEG_INF)


def _pair_heads(t):
    return jnp.concatenate([t[:, 0::2], t[:, 1::2]], axis=-1).astype(BF16)


def kernel(x_prompt, x_sample, cache_k, cache_v, c, c_ctx, norm_mix_g, norm_ffn_g, w_mod, b_mod,
           w_in, q_norm_g, k_norm_g, pool_w, pool_scale, na_rpb, w_pool_proj, w_na_proj, w_o,
           w_up, ffn_conv_w, ffn_conv_b, w_down):
    depth = w_in.shape[0]
    assert depth == 1
    batch, seq, _ = x_prompt.shape
    dec_batch, dec_seq, _ = x_sample.shape
    assert seq == TILE and dec_seq % MIXER_IN_ROWS == 0 and dec_batch + 1 <= MOD_ROWS
    l = 0

    cond = jnp.concatenate(
        [c_ctx[None], c, jnp.zeros((MOD_ROWS - 1 - dec_batch, D_MODEL), F32)], axis=0)
    mod = _modulation(cond, w_mod[l], b_mod[l])

    w_in_b = w_in[l].astype(BF16)
    w_pool_b = pool_w[l].astype(BF16)
    w_pp_b = w_pool_proj[l].astype(BF16)
    w_np_b = w_na_proj[l].astype(BF16)
    w_o_b = w_o[l].astype(BF16)
    w_up_b = w_up[l].astype(BF16)
    w_down_b = w_down[l].astype(BF16)
    conv_w = ffn_conv_w[l]
    g_mix = norm_mix_g[l].reshape(1, D_MODEL)
    g_ffn = norm_ffn_g[l].reshape(1, D_MODEL)
    q_g = jnp.tile(q_norm_g[l], N_HEADS).reshape(1, D_NA)
    k_g = jnp.tile(k_norm_g[l], N_HEADS).reshape(1, D_NA)
    s_pool = pool_scale[l].reshape(1, D_POOL)
    conv_b = ffn_conv_b[l].reshape(1, 2 * D_FF)

    xc = x_prompt.reshape(batch * seq, D_MODEL)
    xc, new_k, new_v = _block_seq(xc, mod, g_mix, w_in_b, q_g, k_g,
                                  w_pool_b, s_pool, w_pp_b, w_np_b, w_o_b,
                                  g_ffn, w_up_b, conv_w, conv_b, w_down_b)
    y_prompt = xc.reshape(batch, seq, D_MODEL)

    xs = x_sample.reshape(dec_batch * dec_seq, D_MODEL)
    p, q, k, v, gp, gn = _mixer_in(xs, mod, g_mix, w_in_b, q_g, k_g,
                                   seq_len=dec_seq, rows=MIXER_IN_ROWS)
    xs = _mixer_out(xs, mod, p, q, k, v, gp, gn,
                    _pair_heads(cache_k[:, l]), _pair_heads(cache_v[:, l]),
                    _toeplitz_bias(na_rpb[l]), w_pool_b, s_pool, w_pp_b, w_np_b, w_o_b,
                    seq_len=dec_seq)
    xs = _conv_ffn(xs, mod, g_ffn, w_up_b, conv_w, conv_b, w_down_b,
                   seq_len=dec_seq, rows=CONV_FFN_ROWS)
    y_sample = xs.reshape(dec_batch, dec_seq, D_MODEL)
    return (y_prompt, y_sample, new_k, new_v)
```

```python
import functools

import jax
import jax.numpy as jnp
import numpy as np
from jax import lax
from jax.experimental import pallas as pl
from jax.experimental.pallas import tpu as pltpu

D_MODEL = 1024
GRID_W = 64
N_HEADS = 8
HEAD_DIM = 64
D_NA = N_HEADS * HEAD_DIM
D_POOL = 512
POOL_WINDOWS = (2, 4, 8, 16)
POOL_GROUP = D_POOL // len(POOL_WINDOWS)
WIN_H = 8
WIN_W = 16
D_FF = 2816
Q_BLOCK = 128
EPS = 1e-6
NEG_INF = -1e30
LOG2_E = 1.4426950408889634
OFF_Q = D_POOL
OFF_K = D_POOL + D_NA
OFF_V = D_POOL + 2 * D_NA
OFF_GP = D_POOL + 3 * D_NA
OFF_GN = OFF_GP + D_MODEL
D_IN = OFF_GN + D_MODEL

TILE = 256
TILE_ROWS = TILE // GRID_W
KEY_ROWS = TILE_ROWS + WIN_H
N_LOCAL = KEY_ROWS * GRID_W
SUBLANES = 8
LANES = 128
POOL_HALO = SUBLANES
CONV_HALO = SUBLANES
FF_CHUNK = 256
N_FF_CHUNKS = D_FF // FF_CHUNK
PIPE_DEPTH = 3
U_SLOTS = PIPE_DEPTH + 1
ATTN_DEPTH = 1
KV_ROWS = 512
CONV_FFN_ROWS = 512
MOD_ROWS = 8
MOD_TILE = 1536
VMEM_LIMIT = 56 * 1024 * 1024

BF16 = jnp.bfloat16
F32 = jnp.float32


def _dot(a, b):
    return jnp.dot(a, b, preferred_element_type=F32)


def _dot_nt(a, b):
    return lax.dot_general(a, b, (((1,), (1,)), ((), ())), preferred_element_type=F32)


def _modulated_norm(x, g, shift, scale):
    ms = jnp.mean(x * x, axis=-1, keepdims=True)
    return (x * lax.rsqrt(ms + EPS) * g) * (1.0 + scale) + shift


def _sigmoid(x):
    return 1.0 / (1.0 + jnp.exp(-x))


def _mod_row(mod_ref, steps_per_cond):
    if steps_per_cond is None:
        return mod_ref[0:1, :]
    return mod_ref[pl.ds(1 + pl.program_id(0) // steps_per_cond, 1), :]


def _params(**kwargs):
    return pltpu.CompilerParams(
        dimension_semantics=("arbitrary",), vmem_limit_bytes=VMEM_LIMIT, **kwargs)


def _resident(shape):
    return pl.BlockSpec(shape, lambda i: (0,) * len(shape), pipeline_mode=pl.Buffered(1))


def _mod_kernel(cond_ref, w_ref, b_ref, o_ref):
    cnd = cond_ref[...]
    s = (cnd * _sigmoid(cnd)).astype(BF16)
    o_ref[...] = _dot(s, w_ref[...].astype(BF16)) + b_ref[...]


def _modulation(cond, w_mod, b_mod):
    n = w_mod.shape[1]
    return pl.pallas_call(
        _mod_kernel,
        grid=(n // MOD_TILE,),
        in_specs=[
            pl.BlockSpec((MOD_ROWS, D_MODEL), lambda j: (0, 0)),
            pl.BlockSpec((D_MODEL, MOD_TILE), lambda j: (0, j)),
            pl.BlockSpec((1, MOD_TILE), lambda j: (0, j)),
        ],
        out_specs=pl.BlockSpec((MOD_ROWS, MOD_TILE), lambda j: (0, j)),
        out_shape=jax.ShapeDtypeStruct((MOD_ROWS, n), F32),
        compiler_params=_params(),
        name="modulation",
    )(cond, w_mod, b_mod.reshape(1, n))


def _head_norm(t, g):
    assert 2 * HEAD_DIM == LANES
    first_head = lax.broadcasted_iota(jnp.int32, (1, LANES), 1) < HEAD_DIM
    tt = t * t
    blocks = []
    for j in range(D_NA // LANES):
        blk = tt[:, j * LANES:(j + 1) * LANES]
        both = jnp.sum(blk, axis=-1, keepdims=True)
        one = jnp.sum(jnp.where(first_head, blk, 0.0), axis=-1, keepdims=True)
        blocks.append(jnp.where(first_head, one, both - one))
    ss = jnp.concatenate(blocks, axis=-1)
    return t * lax.rsqrt(ss * (1.0 / HEAD_DIM) + EPS) * g


def _scaled_query(q, g):
    return (_head_norm(q, g) * (HEAD_DIM ** -0.5 * LOG2_E)).astype(BF16)


def _mixer_in_kernel(x_ref, mod_ref, g_ref, w_ref, qg_ref, kg_ref,
                     p_ref, q_ref, k_ref, v_ref, gp_ref, gn_ref, nk_ref, nv_ref):
    mod = _mod_row(mod_ref, None)
    h = _modulated_norm(x_ref[...], g_ref[...], mod[:, 0:D_MODEL], mod[:, D_MODEL:2 * D_MODEL])
    h = h.astype(BF16)

    def proj(lo, hi):
        return _dot(h, w_ref[:, lo:hi])

    q = proj(OFF_Q, OFF_K)
    k = proj(OFF_K, OFF_V)
    p_ref[...] = proj(0, OFF_Q)
    v = proj(OFF_V, OFF_GP)
    v_ref[...] = v.astype(BF16)
    q = _scaled_query(q, qg_ref[...])
    gp_ref[...] = proj(OFF_GP, OFF_GN)
    k = _head_norm(k, kg_ref[...])
    k_ref[...] = k.astype(BF16)
    gn_ref[...] = proj(OFF_GN, D_IN)
    t = lax.broadcasted_iota(jnp.int32, (TILE, TILE), 0)
    j = lax.broadcasted_iota(jnp.int32, (TILE, TILE), 1)
    n_blocks = TILE // Q_BLOCK
    perm = jnp.where(j == (t % n_blocks) * Q_BLOCK + t // n_blocks, 1.0, 0.0).astype(BF16)
    q_ref[...] = _dot(perm, q).astype(BF16)
    for hd in range(N_HEADS):
        sl = slice(hd * HEAD_DIM, (hd + 1) * HEAD_DIM)
        nk_ref[0, 0, hd] = k[:, sl]
        nv_ref[0, 0, hd] = v[:, sl]


def _kv_kernel(x_ref, mod_ref, g_ref, wkv_ref, kg_ref, k_ref, v_ref, *, steps_per_cond):
    mod = _mod_row(mod_ref, steps_per_cond)
    h = _modulated_norm(x_ref[...], g_ref[...], mod[:, 0:D_MODEL], mod[:, D_MODEL:2 * D_MODEL])
    h = h.astype(BF16)
    k_ref[...] = _head_norm(_dot(h, wkv_ref[:, 0:D_NA]), kg_ref[...]).astype(BF16)
    v_ref[...] = _dot(h, wkv_ref[:, D_NA:2 * D_NA]).astype(BF16)


def _kv_proj(x, mod, norm_g, w_in, k_g, *, seq_len, rows):
    n_tok = x.shape[0]
    tok = lambda i: (i, 0)
    assert OFF_K % (2 * D_NA) == 0
    kv_cols = pl.BlockSpec((D_MODEL, 2 * D_NA), lambda i: (0, OFF_K // (2 * D_NA)),
                           pipeline_mode=pl.Buffered(1))
    return pl.pallas_call(
        functools.partial(_kv_kernel, steps_per_cond=seq_len // rows),
        grid=(n_tok // rows,),
        in_specs=[pl.BlockSpec((rows, D_MODEL), tok), _resident(mod.shape), _resident(norm_g.shape),
                  kv_cols, _resident(k_g.shape)],
        out_specs=[pl.BlockSpec((rows, D_NA), tok)] * 2,
        out_shape=[jax.ShapeDtypeStruct((n_tok, D_NA), BF16)] * 2,
        compiler_params=_params(),
        name="kv_proj",
    )(x, mod, norm_g, w_in, k_g)


_POOL_SLABS = pltpu.VMEM((len(POOL_WINDOWS), TILE + 2 * POOL_HALO, LANES), F32)


def _key_row0(tile, n_rows):
    return jnp.clip(tile * TILE_ROWS - WIN_H // 2, 0, n_rows - KEY_ROWS)


def _fill_local_bias(bias_ref, toep_ref, tile, n_rows):
    key_row0 = int(np.clip(tile * TILE_ROWS - WIN_H // 2, 0, n_rows - KEY_ROWS))
    for a in range(TILE_ROWS):
        r = tile * TILE_ROWS + a
        row_start = int(np.clip(r - WIN_H // 2, 0, n_rows - WIN_H))
        for b in range(KEY_ROWS):
            kr = key_row0 + b
            dr = kr - r + (WIN_H - 1) if row_start <= kr < row_start + WIN_H else 2 * WIN_H - 1
            half = slice((b % 2) * GRID_W, (b % 2 + 1) * GRID_W)
            bias_ref[:, a * GRID_W:(a + 1) * GRID_W, b * GRID_W:(b + 1) * GRID_W] = (
                toep_ref[:, dr, :, half])


def _mixer_out_kernel(*refs, latent, tiles_per_seq):
    if latent:
        (x_ref, xprev_ref, xnext_ref, mod_ref, k_ref, v_ref, ck_ref, cv_ref,
         gmix_ref, win_ref, qg_ref, toep_ref, wpool_ref, spool_ref, wpp_ref, wnp_ref, wo_ref,
         o_ref, na_ref, pslab_ref, p_ref, q_ref, gp_ref, gn_ref, bias_ref) = refs
    else:
        (x_ref, mod_ref, p_ref, q_ref, k_ref, v_ref, gp_ref, gn_ref,
         wpool_ref, spool_ref, wpp_ref, wnp_ref, wo_ref, o_ref, na_ref, pslab_ref) = refs
    i = pl.program_id(0)
    t_in_seq = i % tiles_per_seq
    seq_len = TILE * tiles_per_seq

    if latent:
        n_rows = seq_len // GRID_W
        key0 = pl.multiple_of(_key_row0(t_in_seq, n_rows) * GRID_W, TILE)
        for tile in (0, 1, tiles_per_seq - 1):
            pl.when(t_in_seq == tile)(
                functools.partial(_fill_local_bias, bias_ref, toep_ref, tile, n_rows))

        mod = _mod_row(mod_ref, tiles_per_seq)
        x_ext = jnp.concatenate([x_ref[...], xprev_ref[...], xnext_ref[...]], axis=0)
        h_ext = _modulated_norm(x_ext, gmix_ref[...], mod[:, 0:D_MODEL],
                                mod[:, D_MODEL:2 * D_MODEL]).astype(BF16)
        h = h_ext[:TILE]
        q_ref[...] = _scaled_query(_dot(h, win_ref[:, OFF_Q:OFF_K]), qg_ref[...])
        p_ext = _dot(h_ext, win_ref[:, 0:OFF_Q])
        p_ref[...] = p_ext[:TILE]
        p_prev, p_next = p_ext[TILE:TILE + POOL_HALO], p_ext[TILE + POOL_HALO:]
        gp_ref[...] = _dot(h, win_ref[:, OFF_GP:OFF_GN])
        gn_ref[...] = _dot(h, win_ref[:, OFF_GN:D_IN])

    assert 2 * HEAD_DIM == LANES
    lane = lax.broadcasted_iota(jnp.int32, (1, LANES), 1)
    only = [jnp.where(lane < HEAD_DIM, 1.0, 0.0).astype(BF16),
            jnp.where(lane < HEAD_DIM, 0.0, 1.0).astype(BF16)]

    def pair(j):
        return slice(j * LANES, (j + 1) * LANES)

    def scores(j):
        qp = q_ref[:, pair(j)]
        out = []
        for half in range(2):
            qh = qp * only[half]
            if latent:
                s_loc = (_dot_nt(qh, k_ref[0, pl.ds(key0, N_LOCAL), pair(j)])
                         + bias_ref[2 * j + half])
                out.append((s_loc, _dot_nt(qh, ck_ref[0, j])))
            else:
                out.append((_dot_nt(qh, k_ref[0, :, pair(j)]),))
        return out

    def attend(j, halves):
        if latent:
            vals = (v_ref[0, pl.ds(key0, N_LOCAL), pair(j)], cv_ref[0, j])
        else:
            vals = (v_ref[0, :, pair(j)],)
        normed = []
        for half, parts in enumerate(halves):
            m = functools.reduce(jnp.maximum, [jnp.max(s, axis=-1, keepdims=True) for s in parts])
            es = [jnp.exp2(s - m) for s in parts]
            if latent:
                oh = sum(_dot(e.astype(BF16), v * only[half] + only[1 - half])
                         for e, v in zip(es, vals))
                normed.append(oh / pltpu.roll(oh, HEAD_DIM, axis=1))
            else:
                denom = sum(jnp.sum(e, axis=-1, keepdims=True) for e in es)
                oh = sum(_dot(e.astype(BF16), v * only[half]) for e, v in zip(es, vals))
                normed.append(oh / denom)
        na_ref[:, pair(j)] = jnp.where(lane < HEAD_DIM, normed[0], normed[1]).astype(BF16)

    n_pairs = N_HEADS // 2
    pending = {j: scores(j) for j in range(min(ATTN_DEPTH, n_pairs))}

    assert POOL_GROUP == LANES and POOL_HALO >= max(POOL_WINDOWS) // 2
    pos = t_in_seq * TILE + lax.broadcasted_iota(jnp.int32, (TILE, 1), 0)
    mixed = []
    for gi, w in enumerate(POOL_WINDOWS):
        gs = slice(gi * POOL_GROUP, (gi + 1) * POOL_GROUP)
        if latent:
            pslab_ref[gi, 0:POOL_HALO] = jnp.where(t_in_seq > 0, p_prev[:, gs], 0.0)
            pslab_ref[gi, POOL_HALO + TILE:] = jnp.where(
                t_in_seq < tiles_per_seq - 1, p_next[:, gs], 0.0)
        else:
            zeros = jnp.zeros((POOL_HALO, LANES), F32)
            pslab_ref[gi, 0:POOL_HALO] = zeros
            pslab_ref[gi, POOL_HALO + TILE:] = zeros
        pslab_ref[gi, POOL_HALO:POOL_HALO + TILE] = p_ref[:, gs]
        sums = sum(pslab_ref[gi, POOL_HALO + d:POOL_HALO + d + TILE]
                   for d in range(-(w // 2), w - w // 2))
        lo = jnp.maximum(pos - w // 2, 0)
        hi = jnp.minimum(pos + (w - w // 2), seq_len)
        pooled = sums / (hi - lo).astype(F32) - p_ref[:, gs]
        mixed.append(_dot(pooled.astype(BF16), wpool_ref[gi]))
    pool_out = jnp.concatenate(mixed, axis=-1) * spool_ref[...]
    gated_pool = _sigmoid(gp_ref[...]) * _dot(pool_out.astype(BF16), wpp_ref[...])

    for j in range(n_pairs):
        if j + ATTN_DEPTH < n_pairs:
            pending[j + ATTN_DEPTH] = scores(j + ATTN_DEPTH)
        attend(j, pending.pop(j))

    b = _dot(na_ref[...], wnp_ref[...])
    merged = gated_pool + _sigmoid(gn_ref[...]) * b
    y = _dot(merged.astype(BF16), wo_ref[...])
    gate = _mod_row(mod_ref, tiles_per_seq if latent else None)[:, 2 * D_MODEL:3 * D_MODEL]
    o_ref[...] = x_ref[...] + gate * y


def _halo_specs(n_tok, rows, halo, width):
    per_tile = rows // halo
    last = n_tok // halo - 1
    return [
        pl.BlockSpec((halo, width), lambda i: (jnp.maximum(i * per_tile - 1, 0), 0)),
        pl.BlockSpec((halo, width), lambda i: (jnp.minimum((i + 1) * per_tile, last), 0)),
    ]


def _mixer_out(x, mod, k, v, cache_k, cache_v, g_mix, w_in, q_g, toeplitz,
               w_pool, s_pool, w_pp, w_np, w_o, *, seq_len):
    n_tok = x.shape[0]
    tiles_per_seq = seq_len // TILE
    assert tiles_per_seq >= 3 and seq_len // GRID_W >= KEY_ROWS
    tok = lambda i: (i, 0)
    seq3 = lambda i: (i // tiles_per_seq, 0, 0)
    seq4 = lambda i: (i // tiles_per_seq, 0, 0, 0)
    k = k.reshape(n_tok // seq_len, seq_len, D_NA)
    v = v.reshape(n_tok // seq_len, seq_len, D_NA)
    weights = (g_mix, w_in, q_g, toeplitz, w_pool, s_pool, w_pp, w_np, w_o)
    return pl.pallas_call(
        functools.partial(_mixer_out_kernel, latent=True, tiles_per_seq=tiles_per_seq),
        grid=(n_tok // TILE,),
        in_specs=[pl.BlockSpec((TILE, D_MODEL), tok)] + _halo_specs(n_tok, TILE, POOL_HALO, D_MODEL) + [
            _resident(mod.shape),
            pl.BlockSpec((1, seq_len, D_NA), seq3),
            pl.BlockSpec((1, seq_len, D_NA), seq3),
            pl.BlockSpec((1,) + cache_k.shape[1:], seq4),
            pl.BlockSpec((1,) + cache_v.shape[1:], seq4),
        ] + [_resident(w.shape) for w in weights],
        out_specs=pl.BlockSpec((TILE, D_MODEL), tok),
        out_shape=jax.ShapeDtypeStruct((n_tok, D_MODEL), F32),
        scratch_shapes=[
            pltpu.VMEM((TILE, D_NA), BF16),
            _POOL_SLABS,
            pltpu.VMEM((TILE, D_POOL), F32),
            pltpu.VMEM((TILE, D_NA), BF16),
            pltpu.VMEM((TILE, D_MODEL), F32),
            pltpu.VMEM((TILE, D_MODEL), F32),
            pltpu.VMEM((N_HEADS, TILE, N_LOCAL), F32),
        ],
        compiler_params=_params(),
        name="mixer_out",
    )(x, x, x, mod, k, v, cache_k, cache_v, *weights)


def _ff_slabs(rows):
    return pltpu.VMEM((U_SLOTS, 2, FF_CHUNK // LANES, rows + 2 * CONV_HALO, LANES), F32)


def _conv_ffn_kernel(*refs, halo, tiles_per_seq):
    if halo:
        (x_ref, xprev_ref, xnext_ref, mod_ref, g_ref, wup_ref, cw_ref, cb_ref, wdn_ref,
         o_ref, h_ref, u_ref, acc_ref) = refs
    else:
        (x_ref, mod_ref, g_ref, wup_ref, cw_ref, cb_ref, wdn_ref,
         o_ref, h_ref, u_ref, acc_ref) = refs
    i = pl.program_id(0)
    t_in_seq = i % tiles_per_seq
    rows = x_ref.shape[0]
    mod = _mod_row(mod_ref, tiles_per_seq if halo else None)
    shift = mod[:, 3 * D_MODEL:4 * D_MODEL]
    scale = mod[:, 4 * D_MODEL:5 * D_MODEL]

    if halo:
        x_ext = jnp.concatenate([xprev_ref[...], x_ref[...], xnext_ref[...]], axis=0)
        u_rows = slice(0, rows + 2 * CONV_HALO)
    else:
        x_ext = x_ref[...]
        u_rows = slice(CONV_HALO, CONV_HALO + rows)
        zeros = jnp.zeros(u_ref.shape[:3] + (CONV_HALO, LANES), F32)
        u_ref[:, :, :, 0:CONV_HALO] = zeros
        u_ref[:, :, :, CONV_HALO + rows:] = zeros
    h_ref[...] = _modulated_norm(x_ext, g_ref[...], shift, scale).astype(BF16)

    def cols(c, part):
        lo = part * D_FF + c * FF_CHUNK
        return slice(lo, lo + FF_CHUNK)

    n_slabs = FF_CHUNK // LANES

    def up_proj(c):
        for part in range(2):
            u = _dot(h_ref[...], wup_ref[:, cols(c, part)])
            for k in range(n_slabs):
                u_ref[c % U_SLOTS, part, k, u_rows] = u[:, k * LANES:(k + 1) * LANES]
                if halo:
                    last_prev = CONV_HALO - 1
                    first_next = CONV_HALO + rows
                    u_ref[c % U_SLOTS, part, k, last_prev:last_prev + 1] = jnp.where(
                        t_in_seq == 0, 0.0, u[last_prev:last_prev + 1, k * LANES:(k + 1) * LANES])
                    u_ref[c % U_SLOTS, part, k, first_next:first_next + 1] = jnp.where(
                        t_in_seq == tiles_per_seq - 1, 0.0,
                        u[first_next:first_next + 1, k * LANES:(k + 1) * LANES])

    def conv(c, part):
        slabs = []
        for k in range(n_slabs):
            lo = cols(c, part).start + k * LANES
            sl = slice(lo, lo + LANES)
            u_prev = u_ref[c % U_SLOTS, part, k, CONV_HALO - 1:CONV_HALO - 1 + rows]
            u_here = u_ref[c % U_SLOTS, part, k, CONV_HALO:CONV_HALO + rows]
            u_next = u_ref[c % U_SLOTS, part, k, CONV_HALO + 1:CONV_HALO + 1 + rows]
            slabs.append(u_prev * cw_ref[0:1, sl] + u_here * cw_ref[1:2, sl]
                         + u_next * cw_ref[2:3, sl] + cb_ref[:, sl])
        return jnp.concatenate(slabs, axis=-1)

    def gated_down(c):
        a = conv(c, 0)
        act = (a * _sigmoid(a) * conv(c, 1)).astype(BF16)
        return _dot(act, wdn_ref[c * FF_CHUNK:(c + 1) * FF_CHUNK, :])

    for c in range(min(PIPE_DEPTH, N_FF_CHUNKS)):
        up_proj(c)
    for c in range(N_FF_CHUNKS):
        if c + PIPE_DEPTH < N_FF_CHUNKS:
            up_proj(c + PIPE_DEPTH)
        part = gated_down(c)
        if c == 0:
            acc_ref[...] = part
        elif c + 1 < N_FF_CHUNKS:
            acc_ref[...] += part
        else:
            o_ref[...] = x_ref[...] + mod[:, 5 * D_MODEL:6 * D_MODEL] * (acc_ref[...] + part)


def _conv_ffn(x, mod, norm_g, w_up, conv_w, conv_b, w_down, *, seq_len, rows):
    n_tok = x.shape[0]
    tiles_per_seq = seq_len // rows
    assert tiles_per_seq > 1
    tok = lambda i: (i, 0)
    weights = (norm_g, w_up, conv_w, conv_b, w_down)
    return pl.pallas_call(
        functools.partial(_conv_ffn_kernel, halo=True, tiles_per_seq=tiles_per_seq),
        grid=(n_tok // rows,),
        in_specs=[pl.BlockSpec((rows, D_MODEL), tok)] + _halo_specs(n_tok, rows, CONV_HALO, D_MODEL) + [
            _resident(w.shape) for w in (mod,) + weights],
        out_specs=pl.BlockSpec((rows, D_MODEL), tok),
        out_shape=jax.ShapeDtypeStruct((n_tok, D_MODEL), F32),
        scratch_shapes=[
            pltpu.VMEM((rows + 2 * CONV_HALO, D_MODEL), BF16),
            _ff_slabs(rows),
            pltpu.VMEM((rows, D_MODEL), F32),
        ],
        compiler_params=_params(),
        name="conv_ffn",
    )(x, x, x, mod, *weights)


def _block_seq_kernel(x_ref, mod_ref, gmix_ref, win_ref, qg_ref, kg_ref,
                      wpool_ref, spool_ref, wpp_ref, wnp_ref, wo_ref,
                      gffn_ref, wup_ref, cw_ref, cb_ref, wdn_ref,
                      o_ref, nk_ref, nv_ref,
                      p_ref, q_ref, k_ref, v_ref, gp_ref, gn_ref, na_ref, pslab_ref,
                      x1_ref, h_ref, u_ref, acc_ref):
    _mixer_in_kernel(x_ref, mod_ref, gmix_ref, win_ref, qg_ref, kg_ref,
                     p_ref, q_ref, k_ref.at[0], v_ref.at[0], gp_ref, gn_ref, nk_ref, nv_ref)
    _mixer_out_kernel(x_ref, mod_ref, p_ref, q_ref, k_ref, v_ref, gp_ref, gn_ref,
                      wpool_ref, spool_ref, wpp_ref, wnp_ref, wo_ref, x1_ref, na_ref, pslab_ref,
                      latent=False, tiles_per_seq=1)
    _conv_ffn_kernel(x1_ref, mod_ref, gffn_ref, wup_ref, cw_ref, cb_ref, wdn_ref,
                     o_ref, h_ref, u_ref, acc_ref, halo=False, tiles_per_seq=1)


def _block_seq(x, mod, g_mix, w_in, q_g, k_g, w_pool, s_pool, w_pp, w_np, w_o,
               g_ffn, w_up, conv_w, conv_b, w_down):
    n_tok = x.shape[0]
    n_tiles = n_tok // TILE
    tok = lambda i: (i, 0)
    kv_shape = jax.ShapeDtypeStruct((n_tiles, 1, N_HEADS, TILE, HEAD_DIM), F32)
    kv_spec = pl.BlockSpec((1, 1, N_HEADS, TILE, HEAD_DIM), lambda i: (i, 0, 0, 0, 0))
    weights = (g_mix, w_in, q_g, k_g, w_pool, s_pool, w_pp, w_np, w_o,
               g_ffn, w_up, conv_w, conv_b, w_down)
    return pl.pallas_call(
        _block_seq_kernel,
        grid=(n_tiles,),
        in_specs=[
            pl.BlockSpec((TILE, D_MODEL), tok),
        ] + [_resident(w.shape) for w in (mod,) + weights],
        out_specs=[pl.BlockSpec((TILE, D_MODEL), tok), kv_spec, kv_spec],
        out_shape=[jax.ShapeDtypeStruct((n_tok, D_MODEL), F32), kv_shape, kv_shape],
        scratch_shapes=[
            pltpu.VMEM((TILE, D_POOL), F32),
            pltpu.VMEM((TILE, D_NA), BF16),
            pltpu.VMEM((1, TILE, D_NA), BF16),
            pltpu.VMEM((1, TILE, D_NA), BF16),
            pltpu.VMEM((TILE, D_MODEL), F32),
            pltpu.VMEM((TILE, D_MODEL), F32),
            pltpu.VMEM((TILE, D_NA), BF16),
            _POOL_SLABS,
            pltpu.VMEM((TILE, D_MODEL), F32),
            pltpu.VMEM((TILE, D_MODEL), BF16),
            _ff_slabs(TILE),
            pltpu.VMEM((TILE, D_MODEL), F32),
        ],
        compiler_params=_params(),
        name="block_seq",
    )(x, mod, *weights)


def _toeplitz_bias(rpb):
    n_dr, n_dc = 2 * WIN_H - 1, 2 * WIN_W - 1
    col = np.arange(GRID_W)
    dc = col[None, :] - col[:, None] + (WIN_W - 1)
    win_start = np.clip(col - WIN_W // 2, 0, GRID_W - WIN_W)
    col_ok = (col[None, :] >= win_start[:, None]) & (col[None, :] < win_start[:, None] + WIN_W)
    col_sel = (dc[None] == np.arange(n_dc)[:, None, None]).astype(np.float32)
    col_sel = np.concatenate([col_sel, col_sel], axis=-1)
    valid = np.concatenate([col_ok, col_ok], axis=-1)[None] & (np.arange(n_dr + 1) < n_dr)[:, None, None]
    rows = jnp.pad(rpb, ((0, 0), (0, 1), (0, 0)))
    blocks = jnp.einsum("hdc,cqk->hdqk", rows, col_sel, precision=lax.Precision.HIGHEST)
    return jnp.where(valid[None], blocks * LOG2_E, NEG_INF)


def _pair_heads(t):
    return jnp.concatenate([t[:, 0::2], t[:, 1::2]], axis=-1).astype(BF16)


def kernel(x_prompt, x_sample, cache_k, cache_v, c, c_ctx, norm_mix_g, norm_ffn_g, w_mod, b_mod,
           w_in, q_norm_g, k_norm_g, pool_w, pool_scale, na_rpb, w_pool_proj, w_na_proj, w_o,
           w_up, ffn_conv_w, ffn_conv_b, w_down):
    depth = w_in.shape[0]
    assert depth == 1
    batch, seq, _ = x_prompt.shape
    dec_batch, dec_seq, _ = x_sample.shape
    assert seq == TILE and dec_seq % KV_ROWS == 0 and dec_batch + 1 <= MOD_ROWS
    l = 0

    cond = jnp.concatenate(
        [c_ctx[None], c, jnp.zeros((MOD_ROWS - 1 - dec_batch, D_MODEL), F32)], axis=0)
    mod = _modulation(cond, w_mod[l], b_mod[l])

    w_in_b = w_in[l].astype(BF16)
    w_pool_b = pool_w[l].astype(BF16)
    w_pp_b = w_pool_proj[l].astype(BF16)
    w_np_b = w_na_proj[l].astype(BF16)
    w_o_b = w_o[l].astype(BF16)
    w_up_b = w_up[l].astype(BF16)
    w_down_b = w_down[l].astype(BF16)
    conv_w = ffn_conv_w[l]
    g_mix = norm_mix_g[l].reshape(1, D_MODEL)
    g_ffn = norm_ffn_g[l].reshape(1, D_MODEL)
    q_g = jnp.tile(q_norm_g[l], N_HEADS).reshape(1, D_NA)
    k_g = jnp.tile(k_norm_g[l], N_HEADS).reshape(1, D_NA)
    s_pool = pool_scale[l].reshape(1, D_POOL)
    conv_b = ffn_conv_b[l].reshape(1, 2 * D_FF)

    xc = x_prompt.reshape(batch * seq, D_MODEL)
    xc, new_k, new_v = _block_seq(xc, mod, g_mix, w_in_b, q_g, k_g,
                                  w_pool_b, s_pool, w_pp_b, w_np_b, w_o_b,
                                  g_ffn, w_up_b, conv_w, conv_b, w_down_b)
    y_prompt = xc.reshape(batch, seq, D_MODEL)

    xs = x_sample.reshape(dec_batch * dec_seq, D_MODEL)
    k, v = _kv_proj(xs, mod, g_mix, w_in_b, k_g, seq_len=dec_seq, rows=KV_ROWS)
    xs = _mixer_out(xs, mod, k, v, _pair_heads(cache_k[:, l]), _pair_heads(cache_v[:, l]),
                    g_mix, w_in_b, q_g, _toeplitz_bias(na_rpb[l]),
                    w_pool_b, s_pool, w_pp_b, w_np_b, w_o_b, seq_len=dec_seq)
    xs = _conv_ffn(xs, mod, g_ffn, w_up_b, conv_w, conv_b, w_down_b,
                   seq_len=dec_seq, rows=CONV_FFN_ROWS)
    y_sample = xs.reshape(dec_batch, dec_seq, D_MODEL)
    return (y_prompt, y_sample, new_k, new_v)
```

```python
import functools

import jax
import jax.numpy as jnp
import numpy as np
from jax import lax
from jax.experimental import pallas as pl
from jax.experimental.pallas import tpu as pltpu

D_MODEL = 1024
GRID_W = 64
N_HEADS = 8
HEAD_DIM = 64
D_NA = N_HEADS * HEAD_DIM
D_POOL = 512
POOL_WINDOWS = (2, 4, 8, 16)
POOL_GROUP = D_POOL // len(POOL_WINDOWS)
WIN_H = 8
WIN_W = 16
D_FF = 2816
Q_BLOCK = 128
EPS = 1e-6
NEG_INF = -1e30
LOG2_E = 1.4426950408889634
OFF_Q = D_POOL
OFF_K = D_POOL + D_NA
OFF_V = D_POOL + 2 * D_NA
OFF_GP = D_POOL + 3 * D_NA
OFF_GN = OFF_GP + D_MODEL
D_IN = OFF_GN + D_MODEL

TILE = 256
TILE_ROWS = TILE // GRID_W
KEY_ROWS = TILE_ROWS + WIN_H
N_LOCAL = KEY_ROWS * GRID_W
SUBLANES = 8
LANES = 128
POOL_HALO = SUBLANES
CONV_HALO = SUBLANES
FF_CHUNK = 256
N_FF_CHUNKS = D_FF // FF_CHUNK
PIPE_DEPTH = 4
U_SLOTS = PIPE_DEPTH + 1
ATTN_DEPTH = 1
KV_ROWS = 1024
KV_X_SPLIT = 4
CONV_FFN_ROWS = 512
MOD_ROWS = 8
MOD_TILE = 1536
VMEM_LIMIT = 56 * 1024 * 1024

BF16 = jnp.bfloat16
F32 = jnp.float32


def _dot(a, b):
    return jnp.dot(a, b, preferred_element_type=F32)


def _dot_nt(a, b):
    return lax.dot_general(a, b, (((1,), (1,)), ((), ())), preferred_element_type=F32)


def _modulated_norm(x, g, shift, scale):
    ms = jnp.mean(x * x, axis=-1, keepdims=True)
    return (x * lax.rsqrt(ms + EPS) * g) * (1.0 + scale) + shift


def _sigmoid(x):
    return 1.0 / (1.0 + jnp.exp(-x))


def _mod_row(mod_ref, steps_per_cond):
    if steps_per_cond is None:
        return mod_ref[0:1, :]
    return mod_ref[pl.ds(1 + pl.program_id(0) // steps_per_cond, 1), :]


def _params(**kwargs):
    return pltpu.CompilerParams(
        dimension_semantics=("arbitrary",), vmem_limit_bytes=VMEM_LIMIT, **kwargs)


def _resident(shape):
    return pl.BlockSpec(shape, lambda i: (0,) * len(shape), pipeline_mode=pl.Buffered(1))


def _mod_kernel(cond_ref, w_ref, b_ref, o_ref):
    cnd = cond_ref[...]
    s = (cnd * _sigmoid(cnd)).astype(BF16)
    o_ref[...] = _dot(s, w_ref[...].astype(BF16)) + b_ref[...]


def _modulation(cond, w_mod, b_mod):
    n = w_mod.shape[1]
    return pl.pallas_call(
        _mod_kernel,
        grid=(n // MOD_TILE,),
        in_specs=[
            pl.BlockSpec((MOD_ROWS, D_MODEL), lambda j: (0, 0)),
            pl.BlockSpec((D_MODEL, MOD_TILE), lambda j: (0, j)),
            pl.BlockSpec((1, MOD_TILE), lambda j: (0, j)),
        ],
        out_specs=pl.BlockSpec((MOD_ROWS, MOD_TILE), lambda j: (0, j)),
        out_shape=jax.ShapeDtypeStruct((MOD_ROWS, n), F32),
        compiler_params=_params(),
        name="modulation",
    )(cond, w_mod, b_mod.reshape(1, n))


def _head_norm(t, g):
    assert 2 * HEAD_DIM == LANES
    first_head = lax.broadcasted_iota(jnp.int32, (1, LANES), 1) < HEAD_DIM
    tt = t * t
    blocks = []
    for j in range(D_NA // LANES):
        blk = tt[:, j * LANES:(j + 1) * LANES]
        both = jnp.sum(blk, axis=-1, keepdims=True)
        one = jnp.sum(jnp.where(first_head, blk, 0.0), axis=-1, keepdims=True)
        blocks.append(jnp.where(first_head, one, both - one))
    ss = jnp.concatenate(blocks, axis=-1)
    return t * lax.rsqrt(ss * (1.0 / HEAD_DIM) + EPS) * g


def _scaled_query(q, g):
    return (_head_norm(q, g) * (HEAD_DIM ** -0.5 * LOG2_E)).astype(BF16)


def _mixer_norm(x_ref, mod_ref, g_ref, steps_per_cond):
    mod = _mod_row(mod_ref, steps_per_cond)
    h = _modulated_norm(x_ref[...], g_ref[...], mod[:, 0:D_MODEL], mod[:, D_MODEL:2 * D_MODEL])
    return h.astype(BF16)


def _mixer_in_kernel(x_ref, mod_ref, g_ref, w_ref, qg_ref, kg_ref,
                     p_ref, q_ref, k_ref, v_ref, gp_ref, gn_ref, nk_ref, nv_ref):
    h = _mixer_norm(x_ref, mod_ref, g_ref, None)

    def proj(lo, hi):
        return _dot(h, w_ref[:, lo:hi])

    q = proj(OFF_Q, OFF_K)
    k = proj(OFF_K, OFF_V)
    p_ref[...] = proj(0, OFF_Q)
    v = proj(OFF_V, OFF_GP)
    v_ref[...] = v.astype(BF16)
    q = _scaled_query(q, qg_ref[...])
    gp_ref[...] = proj(OFF_GP, OFF_GN)
    k = _head_norm(k, kg_ref[...])
    k_ref[...] = k.astype(BF16)
    gn_ref[...] = proj(OFF_GN, D_IN)
    t = lax.broadcasted_iota(jnp.int32, (TILE, TILE), 0)
    j = lax.broadcasted_iota(jnp.int32, (TILE, TILE), 1)
    n_blocks = TILE // Q_BLOCK
    perm = jnp.where(j == (t % n_blocks) * Q_BLOCK + t // n_blocks, 1.0, 0.0).astype(BF16)
    q_ref[...] = _dot(perm, q).astype(BF16)
    for hd in range(N_HEADS):
        sl = slice(hd * HEAD_DIM, (hd + 1) * HEAD_DIM)
        nk_ref[0, 0, hd] = k[:, sl]
        nv_ref[0, 0, hd] = v[:, sl]


def _kv_kernel(*refs, steps_per_cond):
    x_refs = refs[:KV_X_SPLIT]
    mod_ref, g_ref, wkv_ref, kg_ref, k_ref, v_ref = refs[KV_X_SPLIT:]
    mod = _mod_row(mod_ref, steps_per_cond)
    x = jnp.concatenate([r[...] for r in x_refs], axis=0)
    h = _modulated_norm(x, g_ref[...], mod[:, 0:D_MODEL], mod[:, D_MODEL:2 * D_MODEL]).astype(BF16)
    k_ref[...] = _head_norm(_dot(h, wkv_ref[:, 0:D_NA]), kg_ref[...]).astype(BF16)
    v_ref[...] = _dot(h, wkv_ref[:, D_NA:2 * D_NA]).astype(BF16)


def _kv_proj(x, mod, norm_g, w_in, k_g, *, seq_len, rows):
    n_tok = x.shape[0]
    tok = lambda i: (i, 0)
    assert OFF_K % (2 * D_NA) == 0
    kv_cols = pl.BlockSpec((D_MODEL, 2 * D_NA), lambda i: (0, OFF_K // (2 * D_NA)),
                           pipeline_mode=pl.Buffered(1))
    return pl.pallas_call(
        functools.partial(_kv_kernel, steps_per_cond=seq_len // rows),
        grid=(n_tok // rows,),
        in_specs=[pl.BlockSpec((rows // KV_X_SPLIT, D_MODEL),
                               functools.partial(lambda c, i: (KV_X_SPLIT * i + c, 0), c))
                  for c in range(KV_X_SPLIT)] + [
                      _resident(mod.shape), _resident(norm_g.shape), kv_cols, _resident(k_g.shape)],
        out_specs=[pl.BlockSpec((rows, D_NA), tok)] * 2,
        out_shape=[jax.ShapeDtypeStruct((n_tok, D_NA), BF16)] * 2,
        compiler_params=_params(),
        name="kv_proj",
    )(*([x] * KV_X_SPLIT), mod, norm_g, w_in, k_g)


_POOL_SLABS = pltpu.VMEM((len(POOL_WINDOWS), TILE + 2 * POOL_HALO, LANES), F32)


def _key_row0(tile, n_rows):
    return jnp.clip(tile * TILE_ROWS - WIN_H // 2, 0, n_rows - KEY_ROWS)


def _fill_local_bias(bias_ref, toep_ref, tile, n_rows):
    key_row0 = int(np.clip(tile * TILE_ROWS - WIN_H // 2, 0, n_rows - KEY_ROWS))
    for a in range(TILE_ROWS):
        r = tile * TILE_ROWS + a
        row_start = int(np.clip(r - WIN_H // 2, 0, n_rows - WIN_H))
        for b in range(KEY_ROWS):
            kr = key_row0 + b
            dr = kr - r + (WIN_H - 1) if row_start <= kr < row_start + WIN_H else 2 * WIN_H - 1
            half = slice((b % 2) * GRID_W, (b % 2 + 1) * GRID_W)
            bias_ref[:, a * GRID_W:(a + 1) * GRID_W, b * GRID_W:(b + 1) * GRID_W] = (
                toep_ref[:, dr, :, half])


def _mixer_out_kernel(*refs, latent, tiles_per_seq):
    if latent:
        (x_ref, xprev_ref, xnext_ref, mod_ref, k_ref, v_ref, ck_ref, cv_ref,
         gmix_ref, win_ref, qg_ref, toep_ref, wpool_ref, spool_ref, wpp_ref, wnp_ref, wo_ref,
         o_ref, na_ref, pslab_ref, p_ref, q_ref, gp_ref, gn_ref, bias_ref) = refs
    else:
        (x_ref, mod_ref, p_ref, q_ref, k_ref, v_ref, gp_ref, gn_ref,
         wpool_ref, spool_ref, wpp_ref, wnp_ref, wo_ref, o_ref, na_ref, pslab_ref) = refs
    i = pl.program_id(0)
    t_in_seq = i % tiles_per_seq
    seq_len = TILE * tiles_per_seq

    if latent:
        n_rows = seq_len // GRID_W
        key0 = pl.multiple_of(_key_row0(t_in_seq, n_rows) * GRID_W, TILE)
        for tile in (0, 1, tiles_per_seq - 1):
            pl.when(t_in_seq == tile)(
                functools.partial(_fill_local_bias, bias_ref, toep_ref, tile, n_rows))

        mod = _mod_row(mod_ref, tiles_per_seq)
        x_ext = jnp.concatenate([x_ref[...], xprev_ref[...], xnext_ref[...]], axis=0)
        h_ext = _modulated_norm(x_ext, gmix_ref[...], mod[:, 0:D_MODEL],
                                mod[:, D_MODEL:2 * D_MODEL]).astype(BF16)
        h = h_ext[:TILE]
        q_ref[...] = _scaled_query(_dot(h, win_ref[:, OFF_Q:OFF_K]), qg_ref[...])
        p_ext = _dot(h_ext, win_ref[:, 0:OFF_Q])
        p_ref[...] = p_ext[:TILE]
        p_prev, p_next = p_ext[TILE:TILE + POOL_HALO], p_ext[TILE + POOL_HALO:]
        gp_ref[...] = _dot(h, win_ref[:, OFF_GP:OFF_GN])
        gn_ref[...] = _dot(h, win_ref[:, OFF_GN:D_IN])

    assert 2 * HEAD_DIM == LANES
    lane = lax.broadcasted_iota(jnp.int32, (1, LANES), 1)
    only = [jnp.where(lane < HEAD_DIM, 1.0, 0.0).astype(BF16),
            jnp.where(lane < HEAD_DIM, 0.0, 1.0).astype(BF16)]

    def pair(j):
        return slice(j * LANES, (j + 1) * LANES)

    def scores(j):
        qp = q_ref[:, pair(j)]
        out = []
        for half in range(2):
            qh = qp * only[half]
            if latent:
                s_loc = (_dot_nt(qh, k_ref[0, pl.ds(key0, N_LOCAL), pair(j)])
                         + bias_ref[2 * j + half])
                out.append((s_loc, _dot_nt(qh, ck_ref[0, j])))
            else:
                out.append((_dot_nt(qh, k_ref[0, :, pair(j)]),))
        return out

    def attend(j, halves):
        if latent:
            vals = (v_ref[0, pl.ds(key0, N_LOCAL), pair(j)], cv_ref[0, j])
        else:
            vals = (v_ref[0, :, pair(j)],)
        normed = []
        for half, parts in enumerate(halves):
            m = functools.reduce(jnp.maximum, [jnp.max(s, axis=-1, keepdims=True) for s in parts])
            es = [jnp.exp2(s - m) for s in parts]
            if latent:
                oh = sum(_dot(e.astype(BF16), v * only[half] + only[1 - half])
                         for e, v in zip(es, vals))
                normed.append(oh / pltpu.roll(oh, HEAD_DIM, axis=1))
            else:
                denom = sum(jnp.sum(e, axis=-1, keepdims=True) for e in es)
                oh = sum(_dot(e.astype(BF16), v * only[half]) for e, v in zip(es, vals))
                normed.append(oh / denom)
        na_ref[:, pair(j)] = jnp.where(lane < HEAD_DIM, normed[0], normed[1]).astype(BF16)

    n_pairs = N_HEADS // 2
    pending = {j: scores(j) for j in range(min(ATTN_DEPTH, n_pairs))}

    assert POOL_GROUP == LANES and POOL_HALO >= max(POOL_WINDOWS) // 2
    pos = t_in_seq * TILE + lax.broadcasted_iota(jnp.int32, (TILE, 1), 0)
    mixed = []
    for gi, w in enumerate(POOL_WINDOWS):
        gs = slice(gi * POOL_GROUP, (gi + 1) * POOL_GROUP)
        if latent:
            pslab_ref[gi, 0:POOL_HALO] = jnp.where(t_in_seq > 0, p_prev[:, gs], 0.0)
            pslab_ref[gi, POOL_HALO + TILE:] = jnp.where(
                t_in_seq < tiles_per_seq - 1, p_next[:, gs], 0.0)
        else:
            zeros = jnp.zeros((POOL_HALO, LANES), F32)
            pslab_ref[gi, 0:POOL_HALO] = zeros
            pslab_ref[gi, POOL_HALO + TILE:] = zeros
        pslab_ref[gi, POOL_HALO:POOL_HALO + TILE] = p_ref[:, gs]
        sums = sum(pslab_ref[gi, POOL_HALO + d:POOL_HALO + d + TILE]
                   for d in range(-(w // 2), w - w // 2))
        lo = jnp.maximum(pos - w // 2, 0)
        hi = jnp.minimum(pos + (w - w // 2), seq_len)
        pooled = sums / (hi - lo).astype(F32) - p_ref[:, gs]
        mixed.append(_dot(pooled.astype(BF16), wpool_ref[gi]))
    pool_out = jnp.concatenate(mixed, axis=-1) * spool_ref[...]
    gated_pool = _sigmoid(gp_ref[...]) * _dot(pool_out.astype(BF16), wpp_ref[...])

    for j in range(n_pairs):
        if j + ATTN_DEPTH < n_pairs:
            pending[j + ATTN_DEPTH] = scores(j + ATTN_DEPTH)
        attend(j, pending.pop(j))

    b = _dot(na_ref[...], wnp_ref[...])
    merged = gated_pool + _sigmoid(gn_ref[...]) * b
    y = _dot(merged.astype(BF16), wo_ref[...])
    gate = _mod_row(mod_ref, tiles_per_seq if latent else None)[:, 2 * D_MODEL:3 * D_MODEL]
    o_ref[...] = x_ref[...] + gate * y


def _halo_specs(n_tok, rows, halo, width):
    per_tile = rows // halo
    last = n_tok // halo - 1
    return [
        pl.BlockSpec((halo, width), lambda i: (jnp.maximum(i * per_tile - 1, 0), 0)),
        pl.BlockSpec((halo, width), lambda i: (jnp.minimum((i + 1) * per_tile, last), 0)),
    ]


def _mixer_out(x, mod, k, v, cache_k, cache_v, g_mix, w_in, q_g, toeplitz,
               w_pool, s_pool, w_pp, w_np, w_o, *, seq_len):
    n_tok = x.shape[0]
    tiles_per_seq = seq_len // TILE
    assert tiles_per_seq >= 3 and seq_len // GRID_W >= KEY_ROWS
    tok = lambda i: (i, 0)
    seq3 = lambda i: (i // tiles_per_seq, 0, 0)
    seq4 = lambda i: (i // tiles_per_seq, 0, 0, 0)
    k = k.reshape(n_tok // seq_len, seq_len, D_NA)
    v = v.reshape(n_tok // seq_len, seq_len, D_NA)
    weights = (g_mix, w_in, q_g, toeplitz, w_pool, s_pool, w_pp, w_np, w_o)
    return pl.pallas_call(
        functools.partial(_mixer_out_kernel, latent=True, tiles_per_seq=tiles_per_seq),
        grid=(n_tok // TILE,),
        in_specs=[pl.BlockSpec((TILE, D_MODEL), tok)] + _halo_specs(n_tok, TILE, POOL_HALO, D_MODEL) + [
            _resident(mod.shape),
            pl.BlockSpec((1, seq_len, D_NA), seq3),
            pl.BlockSpec((1, seq_len, D_NA), seq3),
            pl.BlockSpec((1,) + cache_k.shape[1:], seq4),
            pl.BlockSpec((1,) + cache_v.shape[1:], seq4),
        ] + [_resident(w.shape) for w in weights],
        out_specs=pl.BlockSpec((TILE, D_MODEL), tok),
        out_shape=jax.ShapeDtypeStruct((n_tok, D_MODEL), F32),
        scratch_shapes=[
            pltpu.VMEM((TILE, D_NA), BF16),
            _POOL_SLABS,
            pltpu.VMEM((TILE, D_POOL), F32),
            pltpu.VMEM((TILE, D_NA), BF16),
            pltpu.VMEM((TILE, D_MODEL), F32),
            pltpu.VMEM((TILE, D_MODEL), F32),
            pltpu.VMEM((N_HEADS, TILE, N_LOCAL), F32),
        ],
        compiler_params=_params(),
        name="mixer_out",
    )(x, x, x, mod, k, v, cache_k, cache_v, *weights)


def _ff_slabs(rows):
    return pltpu.VMEM((U_SLOTS, 2, FF_CHUNK // LANES, rows + 2 * CONV_HALO, LANES), F32)


def _conv_ffn_kernel(*refs, halo, tiles_per_seq):
    if halo:
        (x_ref, xprev_ref, xnext_ref, mod_ref, g_ref, wup_ref, cw_ref, cb_ref, wdn_ref,
         o_ref, h_ref, u_ref, acc_ref) = refs
    else:
        (x_ref, mod_ref, g_ref, wup_ref, cw_ref, cb_ref, wdn_ref,
         o_ref, h_ref, u_ref, acc_ref) = refs
    i = pl.program_id(0)
    t_in_seq = i % tiles_per_seq
    rows = x_ref.shape[0]
    mod = _mod_row(mod_ref, tiles_per_seq if halo else None)
    shift = mod[:, 3 * D_MODEL:4 * D_MODEL]
    scale = mod[:, 4 * D_MODEL:5 * D_MODEL]

    if halo:
        x_ext = jnp.concatenate([xprev_ref[...], x_ref[...], xnext_ref[...]], axis=0)
        u_rows = slice(0, rows + 2 * CONV_HALO)
    else:
        x_ext = x_ref[...]
        u_rows = slice(CONV_HALO, CONV_HALO + rows)
        zeros = jnp.zeros(u_ref.shape[:3] + (CONV_HALO, LANES), F32)
        u_ref[:, :, :, 0:CONV_HALO] = zeros
        u_ref[:, :, :, CONV_HALO + rows:] = zeros
    h_ref[...] = _modulated_norm(x_ext, g_ref[...], shift, scale).astype(BF16)

    def cols(c, part):
        lo = part * D_FF + c * FF_CHUNK
        return slice(lo, lo + FF_CHUNK)

    n_slabs = FF_CHUNK // LANES

    def up_proj(c):
        for part in range(2):
            u = _dot(h_ref[...], wup_ref[:, cols(c, part)])
            for k in range(n_slabs):
                u_ref[c % U_SLOTS, part, k, u_rows] = u[:, k * LANES:(k + 1) * LANES]
                if halo:
                    last_prev = CONV_HALO - 1
                    first_next = CONV_HALO + rows
                    u_ref[c % U_SLOTS, part, k, last_prev:last_prev + 1] = jnp.where(
                        t_in_seq == 0, 0.0, u[last_prev:last_prev + 1, k * LANES:(k + 1) * LANES])
                    u_ref[c % U_SLOTS, part, k, first_next:first_next + 1] = jnp.where(
                        t_in_seq == tiles_per_seq - 1, 0.0,
                        u[first_next:first_next + 1, k * LANES:(k + 1) * LANES])

    def conv(c, part):
        slabs = []
        for k in range(n_slabs):
            lo = cols(c, part).start + k * LANES
            sl = slice(lo, lo + LANES)
            u_prev = u_ref[c % U_SLOTS, part, k, CONV_HALO - 1:CONV_HALO - 1 + rows]
            u_here = u_ref[c % U_SLOTS, part, k, CONV_HALO:CONV_HALO + rows]
            u_next = u_ref[c % U_SLOTS, part, k, CONV_HALO + 1:CONV_HALO + 1 + rows]
            slabs.append(u_prev * cw_ref[0:1, sl] + u_here * cw_ref[1:2, sl]
                         + u_next * cw_ref[2:3, sl] + cb_ref[:, sl])
        return jnp.concatenate(slabs, axis=-1)

    def gated_down(c):
        a = conv(c, 0)
        act = (a * _sigmoid(a) * conv(c, 1)).astype(BF16)
        return _dot(act, wdn_ref[c * FF_CHUNK:(c + 1) * FF_CHUNK, :])

    for c in range(min(PIPE_DEPTH, N_FF_CHUNKS)):
        up_proj(c)
    for c in range(N_FF_CHUNKS):
        if c + PIPE_DEPTH < N_FF_CHUNKS:
            up_proj(c + PIPE_DEPTH)
        part = gated_down(c)
        if c == 0:
            acc_ref[...] = part
        elif c + 1 < N_FF_CHUNKS:
            acc_ref[...] += part
        else:
            o_ref[...] = x_ref[...] + mod[:, 5 * D_MODEL:6 * D_MODEL] * (acc_ref[...] + part)


def _conv_ffn(x, mod, norm_g, w_up, conv_w, conv_b, w_down, *, seq_len, rows):
    n_tok = x.shape[0]
    tiles_per_seq = seq_len // rows
    assert tiles_per_seq > 1
    tok = lambda i: (i, 0)
    weights = (norm_g, w_up, conv_w, conv_b, w_down)
    return pl.pallas_call(
        functools.partial(_conv_ffn_kernel, halo=True, tiles_per_seq=tiles_per_seq),
        grid=(n_tok // rows,),
        in_specs=[pl.BlockSpec((rows, D_MODEL), tok)] + _halo_specs(n_tok, rows, CONV_HALO, D_MODEL) + [
            _resident(w.shape) for w in (mod,) + weights],
        out_specs=pl.BlockSpec((rows, D_MODEL), tok),
        out_shape=jax.ShapeDtypeStruct((n_tok, D_MODEL), F32),
        scratch_shapes=[
            pltpu.VMEM((rows + 2 * CONV_HALO, D_MODEL), BF16),
            _ff_slabs(rows),
            pltpu.VMEM((rows, D_MODEL), F32),
        ],
        compiler_params=_params(),
        name="conv_ffn",
    )(x, x, x, mod, *weights)


def _block_seq_kernel(x_ref, mod_ref, gmix_ref, win_ref, qg_ref, kg_ref,
                      wpool_ref, spool_ref, wpp_ref, wnp_ref, wo_ref,
                      gffn_ref, wup_ref, cw_ref, cb_ref, wdn_ref,
                      o_ref, nk_ref, nv_ref,
                      p_ref, q_ref, k_ref, v_ref, gp_ref, gn_ref, na_ref, pslab_ref,
                      x1_ref, h_ref, u_ref, acc_ref):
    _mixer_in_kernel(x_ref, mod_ref, gmix_ref, win_ref, qg_ref, kg_ref,
                     p_ref, q_ref, k_ref.at[0], v_ref.at[0], gp_ref, gn_ref, nk_ref, nv_ref)
    _mixer_out_kernel(x_ref, mod_ref, p_ref, q_ref, k_ref, v_ref, gp_ref, gn_ref,
                      wpool_ref, spool_ref, wpp_ref, wnp_ref, wo_ref, x1_ref, na_ref, pslab_ref,
                      latent=False, tiles_per_seq=1)
    _conv_ffn_kernel(x1_ref, mod_ref, gffn_ref, wup_ref, cw_ref, cb_ref, wdn_ref,
                     o_ref, h_ref, u_ref, acc_ref, halo=False, tiles_per_seq=1)


def _block_seq(x, mod, g_mix, w_in, q_g, k_g, w_pool, s_pool, w_pp, w_np, w_o,
               g_ffn, w_up, conv_w, conv_b, w_down):
    n_tok = x.shape[0]
    n_tiles = n_tok // TILE
    tok = lambda i: (i, 0)
    kv_shape = jax.ShapeDtypeStruct((n_tiles, 1, N_HEADS, TILE, HEAD_DIM), F32)
    kv_spec = pl.BlockSpec((1, 1, N_HEADS, TILE, HEAD_DIM), lambda i: (i, 0, 0, 0, 0))
    weights = (g_mix, w_in, q_g, k_g, w_pool, s_pool, w_pp, w_np, w_o,
               g_ffn, w_up, conv_w, conv_b, w_down)
    return pl.pallas_call(
        _block_seq_kernel,
        grid=(n_tiles,),
        in_specs=[
            pl.BlockSpec((TILE, D_MODEL), tok),
        ] + [_resident(w.shape) for w in (mod,) + weights],
        out_specs=[pl.BlockSpec((TILE, D_MODEL), tok), kv_spec, kv_spec],
        out_shape=[jax.ShapeDtypeStruct((n_tok, D_MODEL), F32), kv_shape, kv_shape],
        scratch_shapes=[
            pltpu.VMEM((TILE, D_POOL), F32),
            pltpu.VMEM((TILE, D_NA), BF16),
            pltpu.VMEM((1, TILE, D_NA), BF16),
            pltpu.VMEM((1, TILE, D_NA), BF16),
            pltpu.VMEM((TILE, D_MODEL), F32),
            pltpu.VMEM((TILE, D_MODEL), F32),
            pltpu.VMEM((TILE, D_NA), BF16),
            _POOL_SLABS,
            pltpu.VMEM((TILE, D_MODEL), F32),
            pltpu.VMEM((TILE, D_MODEL), BF16),
            _ff_slabs(TILE),
            pltpu.VMEM((TILE, D_MODEL), F32),
        ],
        compiler_params=_params(),
        name="block_seq",
    )(x, mod, *weights)


def _toeplitz_bias(rpb):
    n_dr, n_dc = 2 * WIN_H - 1, 2 * WIN_W - 1
    col = np.arange(GRID_W)
    dc = col[None, :] - col[:, None] + (WIN_W - 1)
    win_start = np.clip(col - WIN_W // 2, 0, GRID_W - WIN_W)
    col_ok = (col[None, :] >= win_start[:, None]) & (col[None, :] < win_start[:, None] + WIN_W)
    col_sel = (dc[None] == np.arange(n_dc)[:, None, None]).astype(np.float32)
    col_sel = np.concatenate([col_sel, col_sel], axis=-1)
    valid = np.concatenate([col_ok, col_ok], axis=-1)[None] & (np.arange(n_dr + 1) < n_dr)[:, None, None]
    rows = jnp.pad(rpb, ((0, 0), (0, 1), (0, 0)))
    blocks = jnp.einsum("hdc,cqk->hdqk", rows, col_sel, precision=lax.Precision.HIGHEST)
    return jnp.where(valid[None], blocks * LOG2_E, NEG_INF)


def _pair_heads(t):
    return jnp.concatenate([t[:, 0::2], t[:, 1::2]], axis=-1).astype(BF16)


def kernel(x_prompt, x_sample, cache_k, cache_v, c, c_ctx, norm_mix_g, norm_ffn_g, w_mod, b_mod,
           w_in, q_norm_g, k_norm_g, pool_w, pool_scale, na_rpb, w_pool_proj, w_na_proj, w_o,
           w_up, ffn_conv_w, ffn_conv_b, w_down):
    depth = w_in.shape[0]
    assert depth == 1
    batch, seq, _ = x_prompt.shape
    dec_batch, dec_seq, _ = x_sample.shape
    assert seq == TILE and dec_seq % KV_ROWS == 0 and dec_batch + 1 <= MOD_ROWS
    l = 0

    cond = jnp.concatenate(
        [c_ctx[None], c, jnp.zeros((MOD_ROWS - 1 - dec_batch, D_MODEL), F32)], axis=0)
    mod = _modulation(cond, w_mod[l], b_mod[l])

    w_in_b = w_in[l].astype(BF16)
    w_pool_b = pool_w[l].astype(BF16)
    w_pp_b = w_pool_proj[l].astype(BF16)
    w_np_b = w_na_proj[l].astype(BF16)
    w_o_b = w_o[l].astype(BF16)
    w_up_b = w_up[l].astype(BF16)
    w_down_b = w_down[l].astype(BF16)
    conv_w = ffn_conv_w[l]
    g_mix = norm_mix_g[l].reshape(1, D_MODEL)
    g_ffn = norm_ffn_g[l].reshape(1, D_MODEL)
    q_g = jnp.tile(q_norm_g[l], N_HEADS).reshape(1, D_NA)
    k_g = jnp.tile(k_norm_g[l], N_HEADS).reshape(1, D_NA)
    s_pool = pool_scale[l].reshape(1, D_POOL)
    conv_b = ffn_conv_b[l].reshape(1, 2 * D_FF)

    xc = x_prompt.reshape(batch * seq, D_MODEL)
    xc, new_k, new_v = _block_seq(xc, mod, g_mix, w_in_b, q_g, k_g,
                                  w_pool_b, s_pool, w_pp_b, w_np_b, w_o_b,
                                  g_ffn, w_up_b, conv_w, conv_b, w_down_b)
    y_prompt = xc.reshape(batch, seq, D_MODEL)

    xs = x_sample.reshape(dec_batch * dec_seq, D_MODEL)
    k, v = _kv_proj(xs, mod, g_mix, w_in_b, k_g, seq_len=dec_seq, rows=KV_ROWS)
    xs = _mixer_out(xs, mod, k, v, _pair_heads(cache_k[:, l]), _pair_heads(cache_v[:, l]),
                    g_mix, w_in_b, q_g, _toeplitz_bias(na_rpb[l]),
                    w_pool_b, s_pool, w_pp_b, w_np_b, w_o_b, seq_len=dec_seq)
    xs = _conv_ffn(xs, mod, g_ffn, w_up_b, conv_w, conv_b, w_down_b,
                   seq_len=dec_seq, rows=CONV_FFN_ROWS)
    y_sample = xs.reshape(dec_batch, dec_seq, D_MODEL)
    return (y_prompt, y_sample, new_k, new_v)
```

```python
import functools

import jax
import jax.numpy as jnp
import numpy as np
from jax import lax
from jax.experimental import pallas as pl
from jax.experimental.pallas import tpu as pltpu

D_MODEL = 1024
GRID_W = 64
N_HEADS = 8
HEAD_DIM = 64
D_NA = N_HEADS * HEAD_DIM
D_POOL = 512
POOL_WINDOWS = (2, 4, 8, 16)
POOL_GROUP = D_POOL // len(POOL_WINDOWS)
WIN_H = 8
WIN_W = 16
D_FF = 2816
Q_BLOCK = 128
EPS = 1e-6
NEG_INF = -1e30
LOG2_E = 1.4426950408889634
OFF_Q = D_POOL
OFF_K = D_POOL + D_NA
OFF_V = D_POOL + 2 * D_NA
OFF_GP = D_POOL + 3 * D_NA
OFF_GN = OFF_GP + D_MODEL
D_IN = OFF_GN + D_MODEL

TILE = 256
TILE_ROWS = TILE // GRID_W
KEY_ROWS = TILE_ROWS + WIN_H
N_LOCAL = KEY_ROWS * GRID_W
SUBLANES = 8
LANES = 128
POOL_HALO = SUBLANES
CONV_HALO = SUBLANES
FF_CHUNK = 256
N_FF_CHUNKS = D_FF // FF_CHUNK
PIPE_DEPTH = 8
U_SLOTS = PIPE_DEPTH + 1
ATTN_DEPTH = 1
KV_ROWS = 1024
CONV_FFN_ROWS = 512
MOD_ROWS = 8
MOD_TILE = 1536
VMEM_LIMIT = 56 * 1024 * 1024

BF16 = jnp.bfloat16
F32 = jnp.float32


def _dot(a, b):
    return jnp.dot(a, b, preferred_element_type=F32)


def _dot_nt(a, b):
    return lax.dot_general(a, b, (((1,), (1,)), ((), ())), preferred_element_type=F32)


def _modulated_norm(x, g, shift, scale):
    ms = jnp.mean(x * x, axis=-1, keepdims=True)
    return (x * lax.rsqrt(ms + EPS) * g) * (1.0 + scale) + shift


def _sigmoid(x):
    return 1.0 / (1.0 + jnp.exp(-x))


def _mod_row(mod_ref, steps_per_cond):
    if steps_per_cond is None:
        return mod_ref[0:1, :]
    return mod_ref[pl.ds(1 + pl.program_id(0) // steps_per_cond, 1), :]


def _params(**kwargs):
    return pltpu.CompilerParams(
        dimension_semantics=("arbitrary",), vmem_limit_bytes=VMEM_LIMIT, **kwargs)


def _resident(shape):
    return pl.BlockSpec(shape, lambda i: (0,) * len(shape), pipeline_mode=pl.Buffered(1))


def _mod_kernel(cond_ref, w_ref, b_ref, o_ref):
    cnd = cond_ref[...]
    s = (cnd * _sigmoid(cnd)).astype(BF16)
    o_ref[...] = _dot(s, w_ref[...].astype(BF16)) + b_ref[...]


def _modulation(cond, w_mod, b_mod):
    n = w_mod.shape[1]
    return pl.pallas_call(
        _mod_kernel,
        grid=(n // MOD_TILE,),
        in_specs=[
            pl.BlockSpec((MOD_ROWS, D_MODEL), lambda j: (0, 0)),
            pl.BlockSpec((D_MODEL, MOD_TILE), lambda j: (0, j)),
            pl.BlockSpec((1, MOD_TILE), lambda j: (0, j)),
        ],
        out_specs=pl.BlockSpec((MOD_ROWS, MOD_TILE), lambda j: (0, j)),
        out_shape=jax.ShapeDtypeStruct((MOD_ROWS, n), F32),
        compiler_params=_params(),
        name="modulation",
    )(cond, w_mod, b_mod.reshape(1, n))


def _head_norm(t, g):
    assert 2 * HEAD_DIM == LANES
    first_head = lax.broadcasted_iota(jnp.int32, (1, LANES), 1) < HEAD_DIM
    tt = t * t
    blocks = []
    for j in range(D_NA // LANES):
        blk = tt[:, j * LANES:(j + 1) * LANES]
        both = jnp.sum(blk, axis=-1, keepdims=True)
        one = jnp.sum(jnp.where(first_head, blk, 0.0), axis=-1, keepdims=True)
        blocks.append(jnp.where(first_head, one, both - one))
    ss = jnp.concatenate(blocks, axis=-1)
    return t * lax.rsqrt(ss * (1.0 / HEAD_DIM) + EPS) * g


def _scaled_query(q, g):
    return (_head_norm(q, g) * (HEAD_DIM ** -0.5 * LOG2_E)).astype(BF16)


def _mixer_norm(x_ref, mod_ref, g_ref, steps_per_cond):
    mod = _mod_row(mod_ref, steps_per_cond)
    h = _modulated_norm(x_ref[...], g_ref[...], mod[:, 0:D_MODEL], mod[:, D_MODEL:2 * D_MODEL])
    return h.astype(BF16)


def _mixer_in_kernel(x_ref, mod_ref, g_ref, w_ref, qg_ref, kg_ref,
                     p_ref, q_ref, k_ref, v_ref, gp_ref, gn_ref, nk_ref, nv_ref):
    h = _mixer_norm(x_ref, mod_ref, g_ref, None)

    def proj(lo, hi):
        return _dot(h, w_ref[:, lo:hi])

    q = proj(OFF_Q, OFF_K)
    k = proj(OFF_K, OFF_V)
    p_ref[...] = proj(0, OFF_Q)
    v = proj(OFF_V, OFF_GP)
    v_ref[...] = v.astype(BF16)
    q = _scaled_query(q, qg_ref[...])
    gp_ref[...] = proj(OFF_GP, OFF_GN)
    k = _head_norm(k, kg_ref[...])
    k_ref[...] = k.astype(BF16)
    gn_ref[...] = proj(OFF_GN, D_IN)
    t = lax.broadcasted_iota(jnp.int32, (TILE, TILE), 0)
    j = lax.broadcasted_iota(jnp.int32, (TILE, TILE), 1)
    n_blocks = TILE // Q_BLOCK
    perm = jnp.where(j == (t % n_blocks) * Q_BLOCK + t // n_blocks, 1.0, 0.0).astype(BF16)
    q_ref[...] = _dot(perm, q).astype(BF16)
    for hd in range(N_HEADS):
        sl = slice(hd * HEAD_DIM, (hd + 1) * HEAD_DIM)
        nk_ref[0, 0, hd] = k[:, sl]
        nv_ref[0, 0, hd] = v[:, sl]


def _kv_kernel(x_ref, mod_ref, g_ref, wkv_ref, kg_ref, k_ref, v_ref, *, steps_per_cond):
    h = _mixer_norm(x_ref, mod_ref, g_ref, steps_per_cond)
    k_ref[...] = _head_norm(_dot(h, wkv_ref[:, 0:D_NA]), kg_ref[...]).astype(BF16)
    v_ref[...] = _dot(h, wkv_ref[:, D_NA:2 * D_NA]).astype(BF16)


def _kv_proj(x, mod, norm_g, w_in, k_g, *, seq_len, rows):
    n_tok = x.shape[0]
    tok = lambda i: (i, 0)
    assert OFF_K % (2 * D_NA) == 0
    kv_cols = pl.BlockSpec((D_MODEL, 2 * D_NA), lambda i: (0, OFF_K // (2 * D_NA)),
                           pipeline_mode=pl.Buffered(1))
    return pl.pallas_call(
        functools.partial(_kv_kernel, steps_per_cond=seq_len // rows),
        grid=(n_tok // rows,),
        in_specs=[pl.BlockSpec((rows, D_MODEL), tok), _resident(mod.shape), _resident(norm_g.shape),
                  kv_cols, _resident(k_g.shape)],
        out_specs=[pl.BlockSpec((rows, D_NA), tok)] * 2,
        out_shape=[jax.ShapeDtypeStruct((n_tok, D_NA), BF16)] * 2,
        compiler_params=_params(),
        name="kv_proj",
    )(x, mod, norm_g, w_in, k_g)


_POOL_SLABS = pltpu.VMEM((len(POOL_WINDOWS), TILE + 2 * POOL_HALO, LANES), F32)


def _key_row0(tile, n_rows):
    return jnp.clip(tile * TILE_ROWS - WIN_H // 2, 0, n_rows - KEY_ROWS)


def _fill_local_bias(bias_ref, toep_ref, tile, n_rows):
    key_row0 = int(np.clip(tile * TILE_ROWS - WIN_H // 2, 0, n_rows - KEY_ROWS))
    for a in range(TILE_ROWS):
        r = tile * TILE_ROWS + a
        row_start = int(np.clip(r - WIN_H // 2, 0, n_rows - WIN_H))
        for b in range(KEY_ROWS):
            kr = key_row0 + b
            dr = kr - r + (WIN_H - 1) if row_start <= kr < row_start + WIN_H else 2 * WIN_H - 1
            half = slice((b % 2) * GRID_W, (b % 2 + 1) * GRID_W)
            bias_ref[:, a * GRID_W:(a + 1) * GRID_W, b * GRID_W:(b + 1) * GRID_W] = (
                toep_ref[:, dr, :, half])


def _mixer_out_kernel(*refs, latent, tiles_per_seq):
    if latent:
        (x_ref, xprev_ref, xnext_ref, mod_ref, k_ref, v_ref, ck_ref, cv_ref,
         gmix_ref, win_ref, qg_ref, toep_ref, wpool_ref, spool_ref, wpp_ref, wnp_ref, wo_ref,
         o_ref, na_ref, pslab_ref, p_ref, q_ref, gp_ref, gn_ref, bias_ref) = refs
    else:
        (x_ref, mod_ref, p_ref, q_ref, k_ref, v_ref, gp_ref, gn_ref,
         wpool_ref, spool_ref, wpp_ref, wnp_ref, wo_ref, o_ref, na_ref, pslab_ref) = refs
    i = pl.program_id(0)
    t_in_seq = i % tiles_per_seq
    seq_len = TILE * tiles_per_seq

    if latent:
        n_rows = seq_len // GRID_W
        key0 = pl.multiple_of(_key_row0(t_in_seq, n_rows) * GRID_W, TILE)
        for tile in (0, 1, tiles_per_seq - 1):
            pl.when(t_in_seq == tile)(
                functools.partial(_fill_local_bias, bias_ref, toep_ref, tile, n_rows))

        mod = _mod_row(mod_ref, tiles_per_seq)
        x_ext = jnp.concatenate([x_ref[...], xprev_ref[...], xnext_ref[...]], axis=0)
        h_ext = _modulated_norm(x_ext, gmix_ref[...], mod[:, 0:D_MODEL],
                                mod[:, D_MODEL:2 * D_MODEL]).astype(BF16)
        h = h_ext[:TILE]
        q_ref[...] = _scaled_query(_dot(h, win_ref[:, OFF_Q:OFF_K]), qg_ref[...])
        p_ext = _dot(h_ext, win_ref[:, 0:OFF_Q])
        p_ref[...] = p_ext[:TILE]
        p_prev, p_next = p_ext[TILE:TILE + POOL_HALO], p_ext[TILE + POOL_HALO:]
        gp_ref[...] = _dot(h, win_ref[:, OFF_GP:OFF_GN])
        gn_ref[...] = _dot(h, win_ref[:, OFF_GN:D_IN])

    assert 2 * HEAD_DIM == LANES
    lane = lax.broadcasted_iota(jnp.int32, (1, LANES), 1)
    only = [jnp.where(lane < HEAD_DIM, 1.0, 0.0).astype(BF16),
            jnp.where(lane < HEAD_DIM, 0.0, 1.0).astype(BF16)]

    def pair(j):
        return slice(j * LANES, (j + 1) * LANES)

    def scores(j):
        qp = q_ref[:, pair(j)]
        out = []
        for half in range(2):
            qh = qp * only[half]
            if latent:
                s_loc = (_dot_nt(qh, k_ref[0, pl.ds(key0, N_LOCAL), pair(j)])
                         + bias_ref[2 * j + half])
                out.append((s_loc, _dot_nt(qh, ck_ref[0, j])))
            else:
                out.append((_dot_nt(qh, k_ref[0, :, pair(j)]),))
        return out

    def attend(j, halves):
        if latent:
            vals = (v_ref[0, pl.ds(key0, N_LOCAL), pair(j)], cv_ref[0, j])
        else:
            vals = (v_ref[0, :, pair(j)],)
        normed = []
        for half, parts in enumerate(halves):
            m = functools.reduce(jnp.maximum, [jnp.max(s, axis=-1, keepdims=True) for s in parts])
            es = [jnp.exp2(s - m) for s in parts]
            if latent:
                oh = sum(_dot(e.astype(BF16), v * only[half] + only[1 - half])
                         for e, v in zip(es, vals))
                normed.append(oh / pltpu.roll(oh, HEAD_DIM, axis=1))
            else:
                denom = sum(jnp.sum(e, axis=-1, keepdims=True) for e in es)
                oh = sum(_dot(e.astype(BF16), v * only[half]) for e, v in zip(es, vals))
                normed.append(oh / denom)
        na_ref[:, pair(j)] = jnp.where(lane < HEAD_DIM, normed[0], normed[1]).astype(BF16)

    n_pairs = N_HEADS // 2
    pending = {j: scores(j) for j in range(min(ATTN_DEPTH, n_pairs))}

    assert POOL_GROUP == LANES and POOL_HALO >= max(POOL_WINDOWS) // 2
    pos = t_in_seq * TILE + lax.broadcasted_iota(jnp.int32, (TILE, 1), 0)
    mixed = []
    for gi, w in enumerate(POOL_WINDOWS):
        gs = slice(gi * POOL_GROUP, (gi + 1) * POOL_GROUP)
        if latent:
            pslab_ref[gi, 0:POOL_HALO] = jnp.where(t_in_seq > 0, p_prev[:, gs], 0.0)
            pslab_ref[gi, POOL_HALO + TILE:] = jnp.where(
                t_in_seq < tiles_per_seq - 1, p_next[:, gs], 0.0)
        else:
            zeros = jnp.zeros((POOL_HALO, LANES), F32)
            pslab_ref[gi, 0:POOL_HALO] = zeros
            pslab_ref[gi, POOL_HALO + TILE:] = zeros
        pslab_ref[gi, POOL_HALO:POOL_HALO + TILE] = p_ref[:, gs]
        sums = sum(pslab_ref[gi, POOL_HALO + d:POOL_HALO + d + TILE]
                   for d in range(-(w // 2), w - w // 2))
        lo = jnp.maximum(pos - w // 2, 0)
        hi = jnp.minimum(pos + (w - w // 2), seq_len)
        pooled = sums / (hi - lo).astype(F32) - p_ref[:, gs]
        mixed.append(_dot(pooled.astype(BF16), wpool_ref[gi]))
    pool_out = jnp.concatenate(mixed, axis=-1) * spool_ref[...]
    gated_pool = _sigmoid(gp_ref[...]) * _dot(pool_out.astype(BF16), wpp_ref[...])

    for j in range(n_pairs):
        if j + ATTN_DEPTH < n_pairs:
            pending[j + ATTN_DEPTH] = scores(j + ATTN_DEPTH)
        attend(j, pending.pop(j))

    b = _dot(na_ref[...], wnp_ref[...])
    merged = gated_pool + _sigmoid(gn_ref[...]) * b
    y = _dot(merged.astype(BF16), wo_ref[...])
    gate = _mod_row(mod_ref, tiles_per_seq if latent else None)[:, 2 * D_MODEL:3 * D_MODEL]
    o_ref[...] = x_ref[...] + gate * y


def _halo_specs(n_tok, rows, halo, width):
    per_tile = rows // halo
    last = n_tok // halo - 1
    return [
        pl.BlockSpec((halo, width), lambda i: (jnp.maximum(i * per_tile - 1, 0), 0)),
        pl.BlockSpec((halo, width), lambda i: (jnp.minimum((i + 1) * per_tile, last), 0)),
    ]


def _mixer_out(x, mod, k, v, cache_k, cache_v, g_mix, w_in, q_g, toeplitz,
               w_pool, s_pool, w_pp, w_np, w_o, *, seq_len):
    n_tok = x.shape[0]
    tiles_per_seq = seq_len // TILE
    assert tiles_per_seq >= 3 and seq_len // GRID_W >= KEY_ROWS
    tok = lambda i: (i, 0)
    seq3 = lambda i: (i // tiles_per_seq, 0, 0)
    seq4 = lambda i: (i // tiles_per_seq, 0, 0, 0)
    k = k.reshape(n_tok // seq_len, seq_len, D_NA)
    v = v.reshape(n_tok // seq_len, seq_len, D_NA)
    weights = (g_mix, w_in, q_g, toeplitz, w_pool, s_pool, w_pp, w_np, w_o)
    return pl.pallas_call(
        functools.partial(_mixer_out_kernel, latent=True, tiles_per_seq=tiles_per_seq),
        grid=(n_tok // TILE,),
        in_specs=[pl.BlockSpec((TILE, D_MODEL), tok)] + _halo_specs(n_tok, TILE, POOL_HALO, D_MODEL) + [
            _resident(mod.shape),
            pl.BlockSpec((1, seq_len, D_NA), seq3),
            pl.BlockSpec((1, seq_len, D_NA), seq3),
            pl.BlockSpec((1,) + cache_k.shape[1:], seq4),
            pl.BlockSpec((1,) + cache_v.shape[1:], seq4),
        ] + [_resident(w.shape) for w in weights],
        out_specs=pl.BlockSpec((TILE, D_MODEL), tok),
        out_shape=jax.ShapeDtypeStruct((n_tok, D_MODEL), F32),
        scratch_shapes=[
            pltpu.VMEM((TILE, D_NA), BF16),
            _POOL_SLABS,
            pltpu.VMEM((TILE, D_POOL), F32),
            pltpu.VMEM((TILE, D_NA), BF16),
            pltpu.VMEM((TILE, D_MODEL), F32),
            pltpu.VMEM((TILE, D_MODEL), F32),
            pltpu.VMEM((N_HEADS, TILE, N_LOCAL), F32),
        ],
        compiler_params=_params(),
        name="mixer_out",
    )(x, x, x, mod, k, v, cache_k, cache_v, *weights)


def _ff_slabs(rows):
    return pltpu.VMEM((U_SLOTS, 2, FF_CHUNK // LANES, rows + 2 * CONV_HALO, LANES), F32)


def _conv_ffn_kernel(*refs, halo, tiles_per_seq):
    if halo:
        (x_ref, xprev_ref, xnext_ref, mod_ref, g_ref, wup_ref, cw_ref, cb_ref, wdn_ref,
         o_ref, h_ref, u_ref, acc_ref) = refs
    else:
        (x_ref, mod_ref, g_ref, wup_ref, cw_ref, cb_ref, wdn_ref,
         o_ref, h_ref, u_ref, acc_ref) = refs
    i = pl.program_id(0)
    t_in_seq = i % tiles_per_seq
    rows = x_ref.shape[0]
    mod = _mod_row(mod_ref, tiles_per_seq if halo else None)
    shift = mod[:, 3 * D_MODEL:4 * D_MODEL]
    scale = mod[:, 4 * D_MODEL:5 * D_MODEL]

    if halo:
        x_ext = jnp.concatenate([xprev_ref[...], x_ref[...], xnext_ref[...]], axis=0)
        u_rows = slice(0, rows + 2 * CONV_HALO)
    else:
        x_ext = x_ref[...]
        u_rows = slice(CONV_HALO, CONV_HALO + rows)
        zeros = jnp.zeros(u_ref.shape[:3] + (CONV_HALO, LANES), F32)
        u_ref[:, :, :, 0:CONV_HALO] = zeros
        u_ref[:, :, :, CONV_HALO + rows:] = zeros
    h_ref[...] = _modulated_norm(x_ext, g_ref[...], shift, scale).astype(BF16)

    def cols(c, part):
        lo = part * D_FF + c * FF_CHUNK
        return slice(lo, lo + FF_CHUNK)

    n_slabs = FF_CHUNK // LANES

    def up_proj(c):
        for part in range(2):
            u = _dot(h_ref[...], wup_ref[:, cols(c, part)])
            for k in range(n_slabs):
                u_ref[c % U_SLOTS, part, k, u_rows] = u[:, k * LANES:(k + 1) * LANES]
                if halo:
                    last_prev = CONV_HALO - 1
                    first_next = CONV_HALO + rows
                    u_ref[c % U_SLOTS, part, k, last_prev:last_prev + 1] = jnp.where(
                        t_in_seq == 0, 0.0, u[last_prev:last_prev + 1, k * LANES:(k + 1) * LANES])
                    u_ref[c % U_SLOTS, part, k, first_next:first_next + 1] = jnp.where(
                        t_in_seq == tiles_per_seq - 1, 0.0,
                        u[first_next:first_next + 1, k * LANES:(k + 1) * LANES])

    def conv(c, part):
        slabs = []
        for k in range(n_slabs):
            lo = cols(c, part).start + k * LANES
            sl = slice(lo, lo + LANES)
            u_prev = u_ref[c % U_SLOTS, part, k, CONV_HALO - 1:CONV_HALO - 1 + rows]
            u_here = u_ref[c % U_SLOTS, part, k, CONV_HALO:CONV_HALO + rows]
            u_next = u_ref[c % U_SLOTS, part, k, CONV_HALO + 1:CONV_HALO + 1 + rows]
            slabs.append(u_prev * cw_ref[0:1, sl] + u_here * cw_ref[1:2, sl]
                         + u_next * cw_ref[2:3, sl] + cb_ref[:, sl])
        return jnp.concatenate(slabs, axis=-1)

    def gated_down(c):
        a = conv(c, 0)
        act = (a * _sigmoid(a) * conv(c, 1)).astype(BF16)
        return _dot(act, wdn_ref[c * FF_CHUNK:(c + 1) * FF_CHUNK, :])

    for c in range(min(PIPE_DEPTH, N_FF_CHUNKS)):
        up_proj(c)
    for c in range(N_FF_CHUNKS):
        if c + PIPE_DEPTH < N_FF_CHUNKS:
            up_proj(c + PIPE_DEPTH)
        part = gated_down(c)
        if c == 0:
            acc_ref[...] = part
        elif c + 1 < N_FF_CHUNKS:
            acc_ref[...] += part
        else:
            o_ref[...] = x_ref[...] + mod[:, 5 * D_MODEL:6 * D_MODEL] * (acc_ref[...] + part)


def _conv_ffn(x, mod, norm_g, w_up, conv_w, conv_b, w_down, *, seq_len, rows):
    n_tok = x.shape[0]
    tiles_per_seq = seq_len // rows
    assert tiles_per_seq > 1
    tok = lambda i: (i, 0)
    weights = (norm_g, w_up, conv_w, conv_b, w_down)
    return pl.pallas_call(
        functools.partial(_conv_ffn_kernel, halo=True, tiles_per_seq=tiles_per_seq),
        grid=(n_tok // rows,),
        in_specs=[pl.BlockSpec((rows, D_MODEL), tok)] + _halo_specs(n_tok, rows, CONV_HALO, D_MODEL) + [
            _resident(w.shape) for w in (mod,) + weights],
        out_specs=pl.BlockSpec((rows, D_MODEL), tok),
        out_shape=jax.ShapeDtypeStruct((n_tok, D_MODEL), F32),
        scratch_shapes=[
            pltpu.VMEM((rows + 2 * CONV_HALO, D_MODEL), BF16),
            _ff_slabs(rows),
            pltpu.VMEM((rows, D_MODEL), F32),
        ],
        compiler_params=_params(),
        name="conv_ffn",
    )(x, x, x, mod, *weights)


def _block_seq_kernel(x_ref, mod_ref, gmix_ref, win_ref, qg_ref, kg_ref,
                      wpool_ref, spool_ref, wpp_ref, wnp_ref, wo_ref,
                      gffn_ref, wup_ref, cw_ref, cb_ref, wdn_ref,
                      o_ref, nk_ref, nv_ref,
                      p_ref, q_ref, k_ref, v_ref, gp_ref, gn_ref, na_ref, pslab_ref,
                      x1_ref, h_ref, u_ref, acc_ref):
    _mixer_in_kernel(x_ref, mod_ref, gmix_ref, win_ref, qg_ref, kg_ref,
                     p_ref, q_ref, k_ref.at[0], v_ref.at[0], gp_ref, gn_ref, nk_ref, nv_ref)
    _mixer_out_kernel(x_ref, mod_ref, p_ref, q_ref, k_ref, v_ref, gp_ref, gn_ref,
                      wpool_ref, spool_ref, wpp_ref, wnp_ref, wo_ref, x1_ref, na_ref, pslab_ref,
                      latent=False, tiles_per_seq=1)
    _conv_ffn_kernel(x1_ref, mod_ref, gffn_ref, wup_ref, cw_ref, cb_ref, wdn_ref,
                     o_ref, h_ref, u_ref, acc_ref, halo=False, tiles_per_seq=1)


def _block_seq(x, mod, g_mix, w_in, q_g, k_g, w_pool, s_pool, w_pp, w_np, w_o,
               g_ffn, w_up, conv_w, conv_b, w_down):
    n_tok = x.shape[0]
    n_tiles = n_tok // TILE
    tok = lambda i: (i, 0)
    kv_shape = jax.ShapeDtypeStruct((n_tiles, 1, N_HEADS, TILE, HEAD_DIM), F32)
    kv_spec = pl.BlockSpec((1, 1, N_HEADS, TILE, HEAD_DIM), lambda i: (i, 0, 0, 0, 0))
    weights = (g_mix, w_in, q_g, k_g, w_pool, s_pool, w_pp, w_np, w_o,
               g_ffn, w_up, conv_w, conv_b, w_down)
    return pl.pallas_call(
        _block_seq_kernel,
        grid=(n_tiles,),
        in_specs=[
            pl.BlockSpec((TILE, D_MODEL), tok),
        ] + [_resident(w.shape) for w in (mod,) + weights],
        out_specs=[pl.BlockSpec((TILE, D_MODEL), tok), kv_spec, kv_spec],
        out_shape=[jax.ShapeDtypeStruct((n_tok, D_MODEL), F32), kv_shape, kv_shape],
        scratch_shapes=[
            pltpu.VMEM((TILE, D_POOL), F32),
            pltpu.VMEM((TILE, D_NA), BF16),
            pltpu.VMEM((1, TILE, D_NA), BF16),
            pltpu.VMEM((1, TILE, D_NA), BF16),
            pltpu.VMEM((TILE, D_MODEL), F32),
            pltpu.VMEM((TILE, D_MODEL), F32),
            pltpu.VMEM((TILE, D_NA), BF16),
            _POOL_SLABS,
            pltpu.VMEM((TILE, D_MODEL), F32),
            pltpu.VMEM((TILE, D_MODEL), BF16),
            _ff_slabs(TILE),
            pltpu.VMEM((TILE, D_MODEL), F32),
        ],
        compiler_params=_params(),
        name="block_seq",
    )(x, mod, *weights)


def _toeplitz_bias(rpb):
    n_dr, n_dc = 2 * WIN_H - 1, 2 * WIN_W - 1
    col = np.arange(GRID_W)
    dc = col[None, :] - col[:, None] + (WIN_W - 1)
    win_start = np.clip(col - WIN_W // 2, 0, GRID_W - WIN_W)
    col_ok = (col[None, :] >= win_start[:, None]) & (col[None, :] < win_start[:, None] + WIN_W)
    col_sel = (dc[None] == np.arange(n_dc)[:, None, None]).astype(np.float32)
    col_sel = np.concatenate([col_sel, col_sel], axis=-1)
    valid = np.concatenate([col_ok, col_ok], axis=-1)[None] & (np.arange(n_dr + 1) < n_dr)[:, None, None]
    rows = jnp.pad(rpb, ((0, 0), (0, 1), (0, 0)))
    blocks = jnp.einsum("hdc,cqk->hdqk", rows, col_sel, precision=lax.Precision.HIGHEST)
    return jnp.where(valid[None], blocks * LOG2_E, NEG_INF)


def _pair_heads(t):
    return jnp.concatenate([t[:, 0::2], t[:, 1::2]], axis=-1).astype(BF16)


def kernel(x_prompt, x_sample, cache_k, cache_v, c, c_ctx, norm_mix_g, norm_ffn_g, w_mod, b_mod,
           w_in, q_norm_g, k_norm_g, pool_w, pool_scale, na_rpb, w_pool_proj, w_na_proj, w_o,
           w_up, ffn_conv_w, ffn_conv_b, w_down):
    depth = w_in.shape[0]
    assert depth == 1
    batch, seq, _ = x_prompt.shape
    dec_batch, dec_seq, _ = x_sample.shape
    assert seq == TILE and dec_seq % KV_ROWS == 0 and dec_batch + 1 <= MOD_ROWS
    l = 0

    cond = jnp.concatenate(
        [c_ctx[None], c, jnp.zeros((MOD_ROWS - 1 - dec_batch, D_MODEL), F32)], axis=0)
    mod = _modulation(cond, w_mod[l], b_mod[l])

    w_in_b = w_in[l].astype(BF16)
    w_pool_b = pool_w[l].astype(BF16)
    w_pp_b = w_pool_proj[l].astype(BF16)
    w_np_b = w_na_proj[l].astype(BF16)
    w_o_b = w_o[l].astype(BF16)
    w_up_b = w_up[l].astype(BF16)
    w_down_b = w_down[l].astype(BF16)
    conv_w = ffn_conv_w[l]
    g_mix = norm_mix_g[l].reshape(1, D_MODEL)
    g_ffn = norm_ffn_g[l].reshape(1, D_MODEL)
    q_g = jnp.tile(q_norm_g[l], N_HEADS).reshape(1, D_NA)
    k_g = jnp.tile(k_norm_g[l], N_HEADS).reshape(1, D_NA)
    s_pool = pool_scale[l].reshape(1, D_POOL)
    conv_b = ffn_conv_b[l].reshape(1, 2 * D_FF)

    xc = x_prompt.reshape(batch * seq, D_MODEL)
    xc, new_k, new_v = _block_seq(xc, mod, g_mix, w_in_b, q_g, k_g,
                                  w_pool_b, s_pool, w_pp_b, w_np_b, w_o_b,
                                  g_ffn, w_up_b, conv_w, conv_b, w_down_b)
    y_prompt = xc.reshape(batch, seq, D_MODEL)

    xs = x_sample.reshape(dec_batch * dec_seq, D_MODEL)
    k, v = _kv_proj(xs, mod, g_mix, w_in_b, k_g, seq_len=dec_seq, rows=KV_ROWS)
    xs = _mixer_out(xs, mod, k, v, _pair_heads(cache_k[:, l]), _pair_heads(cache_v[:, l]),
                    g_mix, w_in_b, q_g, _toeplitz_bias(na_rpb[l]),
                    w_pool_b, s_pool, w_pp_b, w_np_b, w_o_b, seq_len=dec_seq)
    xs = _conv_ffn(xs, mod, g_ffn, w_up_b, conv_w, conv_b, w_down_b,
                   seq_len=dec_seq, rows=CONV_FFN_ROWS)
    y_sample = xs.reshape(dec_batch, dec_seq, D_MODEL)
    return (y_prompt, y_sample, new_k, new_v)
```

```python
import functools

import jax
import jax.numpy as jnp
import numpy as np
from jax import lax
from jax.experimental import pallas as pl
from jax.experimental.pallas import tpu as pltpu

D_MODEL = 1024
GRID_W = 64
N_HEADS = 8
HEAD_DIM = 64
D_NA = N_HEADS * HEAD_DIM
D_POOL = 512
POOL_WINDOWS = (2, 4, 8, 16)
POOL_GROUP = D_POOL // len(POOL_WINDOWS)
WIN_H = 8
WIN_W = 16
D_FF = 2816
Q_BLOCK = 128
EPS = 1e-6
NEG_INF = -1e30
LOG2_E = 1.4426950408889634
OFF_Q = D_POOL
OFF_K = D_POOL + D_NA
OFF_V = D_POOL + 2 * D_NA
OFF_GP = D_POOL + 3 * D_NA
OFF_GN = OFF_GP + D_MODEL
D_IN = OFF_GN + D_MODEL

TILE = 256
TILE_ROWS = TILE // GRID_W
KEY_ROWS = TILE_ROWS + WIN_H
N_LOCAL = KEY_ROWS * GRID_W
SUBLANES = 8
LANES = 128
POOL_HALO = SUBLANES
CONV_HALO = SUBLANES
FF_CHUNK = 256
N_FF_CHUNKS = D_FF // FF_CHUNK
PIPE_DEPTH = 8
U_SLOTS = PIPE_DEPTH + 1
ATTN_DEPTH = 1
KV_ROWS = 1024
MOD_ROWS = 8
MOD_TILE = 1536
VMEM_LIMIT = 56 * 1024 * 1024

BF16 = jnp.bfloat16
F32 = jnp.float32


def _dot(a, b):
    return jnp.dot(a, b, preferred_element_type=F32)


def _dot_nt(a, b):
    return lax.dot_general(a, b, (((1,), (1,)), ((), ())), preferred_element_type=F32)


def _modulated_norm(x, g, shift, scale):
    ms = jnp.mean(x * x, axis=-1, keepdims=True)
    return (x * lax.rsqrt(ms + EPS) * g) * (1.0 + scale) + shift


def _sigmoid(x):
    return 1.0 / (1.0 + jnp.exp(-x))


def _mod_row(mod_ref, steps_per_cond, first_step=0):
    if steps_per_cond is None:
        return mod_ref[0:1, :]
    return mod_ref[pl.ds(1 + (pl.program_id(0) - first_step) // steps_per_cond, 1), :]


def _params(**kwargs):
    return pltpu.CompilerParams(
        dimension_semantics=("arbitrary",), vmem_limit_bytes=VMEM_LIMIT, **kwargs)


def _resident(shape):
    return pl.BlockSpec(shape, lambda i: (0,) * len(shape), pipeline_mode=pl.Buffered(1))


def _mod_kernel(cond_ref, w_ref, b_ref, o_ref):
    cnd = cond_ref[...]
    s = (cnd * _sigmoid(cnd)).astype(BF16)
    o_ref[...] = _dot(s, w_ref[...].astype(BF16)) + b_ref[...]


def _modulation(cond, w_mod, b_mod):
    n = w_mod.shape[1]
    return pl.pallas_call(
        _mod_kernel,
        grid=(n // MOD_TILE,),
        in_specs=[
            pl.BlockSpec((MOD_ROWS, D_MODEL), lambda j: (0, 0)),
            pl.BlockSpec((D_MODEL, MOD_TILE), lambda j: (0, j)),
            pl.BlockSpec((1, MOD_TILE), lambda j: (0, j)),
        ],
        out_specs=pl.BlockSpec((MOD_ROWS, MOD_TILE), lambda j: (0, j)),
        out_shape=jax.ShapeDtypeStruct((MOD_ROWS, n), F32),
        compiler_params=_params(),
        name="modulation",
    )(cond, w_mod, b_mod.reshape(1, n))


def _head_norm(t, g):
    assert 2 * HEAD_DIM == LANES
    first_head = lax.broadcasted_iota(jnp.int32, (1, LANES), 1) < HEAD_DIM
    tt = t * t
    blocks = []
    for j in range(D_NA // LANES):
        blk = tt[:, j * LANES:(j + 1) * LANES]
        both = jnp.sum(blk, axis=-1, keepdims=True)
        one = jnp.sum(jnp.where(first_head, blk, 0.0), axis=-1, keepdims=True)
        blocks.append(jnp.where(first_head, one, both - one))
    ss = jnp.concatenate(blocks, axis=-1)
    return t * lax.rsqrt(ss * (1.0 / HEAD_DIM) + EPS) * g


def _scaled_query(q, g):
    return (_head_norm(q, g) * (HEAD_DIM ** -0.5 * LOG2_E)).astype(BF16)


def _mixer_norm(x_ref, mod_ref, g_ref, steps_per_cond):
    mod = _mod_row(mod_ref, steps_per_cond)
    h = _modulated_norm(x_ref[...], g_ref[...], mod[:, 0:D_MODEL], mod[:, D_MODEL:2 * D_MODEL])
    return h.astype(BF16)


def _mixer_in_kernel(x_ref, mod_ref, g_ref, w_ref, qg_ref, kg_ref,
                     p_ref, q_ref, k_ref, v_ref, gp_ref, gn_ref, nk_ref, nv_ref):
    h = _mixer_norm(x_ref, mod_ref, g_ref, None)

    def proj(lo, hi):
        return _dot(h, w_ref[:, lo:hi])

    q = proj(OFF_Q, OFF_K)
    k = proj(OFF_K, OFF_V)
    p_ref[...] = proj(0, OFF_Q)
    v = proj(OFF_V, OFF_GP)
    v_ref[...] = v.astype(BF16)
    q = _scaled_query(q, qg_ref[...])
    gp_ref[...] = proj(OFF_GP, OFF_GN)
    k = _head_norm(k, kg_ref[...])
    k_ref[...] = k.astype(BF16)
    gn_ref[...] = proj(OFF_GN, D_IN)
    t = lax.broadcasted_iota(jnp.int32, (TILE, TILE), 0)
    j = lax.broadcasted_iota(jnp.int32, (TILE, TILE), 1)
    n_blocks = TILE // Q_BLOCK
    perm = jnp.where(j == (t % n_blocks) * Q_BLOCK + t // n_blocks, 1.0, 0.0).astype(BF16)
    q_ref[...] = _dot(perm, q).astype(BF16)
    for hd in range(N_HEADS):
        sl = slice(hd * HEAD_DIM, (hd + 1) * HEAD_DIM)
        nk_ref[0, 0, hd] = k[:, sl]
        nv_ref[0, 0, hd] = v[:, sl]


def _kv_kernel(x_ref, mod_ref, g_ref, wkv_ref, kg_ref, k_ref, v_ref, *, steps_per_cond):
    h = _mixer_norm(x_ref, mod_ref, g_ref, steps_per_cond)
    k_ref[...] = _head_norm(_dot(h, wkv_ref[:, 0:D_NA]), kg_ref[...]).astype(BF16)
    v_ref[...] = _dot(h, wkv_ref[:, D_NA:2 * D_NA]).astype(BF16)


def _kv_proj(x, mod, norm_g, w_in, k_g, *, seq_len, rows):
    n_tok = x.shape[0]
    tok = lambda i: (i, 0)
    assert OFF_K % (2 * D_NA) == 0
    kv_cols = pl.BlockSpec((D_MODEL, 2 * D_NA), lambda i: (0, OFF_K // (2 * D_NA)),
                           pipeline_mode=pl.Buffered(1))
    return pl.pallas_call(
        functools.partial(_kv_kernel, steps_per_cond=seq_len // rows),
        grid=(n_tok // rows,),
        in_specs=[pl.BlockSpec((rows, D_MODEL), tok), _resident(mod.shape), _resident(norm_g.shape),
                  kv_cols, _resident(k_g.shape)],
        out_specs=[pl.BlockSpec((rows, D_NA), tok)] * 2,
        out_shape=[jax.ShapeDtypeStruct((n_tok, D_NA), BF16)] * 2,
        compiler_params=_params(),
        name="kv_proj",
    )(x, mod, norm_g, w_in, k_g)


_POOL_SLABS = pltpu.VMEM((len(POOL_WINDOWS), TILE + 2 * POOL_HALO, LANES), F32)


def _key_row0(tile, n_rows):
    return jnp.clip(tile * TILE_ROWS - WIN_H // 2, 0, n_rows - KEY_ROWS)


def _fill_local_bias(bias_ref, toep_ref, tile, n_rows):
    key_row0 = int(np.clip(tile * TILE_ROWS - WIN_H // 2, 0, n_rows - KEY_ROWS))
    for a in range(TILE_ROWS):
        r = tile * TILE_ROWS + a
        row_start = int(np.clip(r - WIN_H // 2, 0, n_rows - WIN_H))
        for b in range(KEY_ROWS):
            kr = key_row0 + b
            dr = kr - r + (WIN_H - 1) if row_start <= kr < row_start + WIN_H else 2 * WIN_H - 1
            half = slice((b % 2) * GRID_W, (b % 2 + 1) * GRID_W)
            bias_ref[:, a * GRID_W:(a + 1) * GRID_W, b * GRID_W:(b + 1) * GRID_W] = (
                toep_ref[:, dr, :, half])


def _mixer_out_kernel(*refs, latent, tiles_per_seq):
    if latent:
        (x_ref, xprev_ref, xnext_ref, mod_ref, k_ref, v_ref, ck_ref, cv_ref,
         gmix_ref, win_ref, qg_ref, toep_ref, wpool_ref, spool_ref, wpp_ref, wnp_ref, wo_ref,
         o_ref, na_ref, pslab_ref, p_ref, q_ref, gp_ref, gn_ref, bias_ref) = refs
    else:
        (x_ref, mod_ref, p_ref, q_ref, k_ref, v_ref, gp_ref, gn_ref,
         wpool_ref, spool_ref, wpp_ref, wnp_ref, wo_ref, o_ref, na_ref, pslab_ref) = refs
    i = pl.program_id(0)
    t_in_seq = i % tiles_per_seq
    seq_len = TILE * tiles_per_seq

    if latent:
        n_rows = seq_len // GRID_W
        key0 = pl.multiple_of(_key_row0(t_in_seq, n_rows) * GRID_W, TILE)
        for tile in (0, 1, tiles_per_seq - 1):
            pl.when(t_in_seq == tile)(
                functools.partial(_fill_local_bias, bias_ref, toep_ref, tile, n_rows))

        mod = _mod_row(mod_ref, tiles_per_seq)
        x_ext = jnp.concatenate([x_ref[...], xprev_ref[...], xnext_ref[...]], axis=0)
        h_ext = _modulated_norm(x_ext, gmix_ref[...], mod[:, 0:D_MODEL],
                                mod[:, D_MODEL:2 * D_MODEL]).astype(BF16)
        h = h_ext[:TILE]
        q_ref[...] = _scaled_query(_dot(h, win_ref[:, OFF_Q:OFF_K]), qg_ref[...])
        p_ext = _dot(h_ext, win_ref[:, 0:OFF_Q])
        p_ref[...] = p_ext[:TILE]
        p_prev, p_next = p_ext[TILE:TILE + POOL_HALO], p_ext[TILE + POOL_HALO:]
        gp_ref[...] = _dot(h, win_ref[:, OFF_GP:OFF_GN])
        gn_ref[...] = _dot(h, win_ref[:, OFF_GN:D_IN])

    assert 2 * HEAD_DIM == LANES
    lane = lax.broadcasted_iota(jnp.int32, (1, LANES), 1)
    only = [jnp.where(lane < HEAD_DIM, 1.0, 0.0).astype(BF16),
            jnp.where(lane < HEAD_DIM, 0.0, 1.0).astype(BF16)]

    def pair(j):
        return slice(j * LANES, (j + 1) * LANES)

    def scores(j):
        qp = q_ref[:, pair(j)]
        out = []
        for half in range(2):
            qh = qp * only[half]
            if latent:
                s_loc = (_dot_nt(qh, k_ref[0, pl.ds(key0, N_LOCAL), pair(j)])
                         + bias_ref[2 * j + half])
                out.append((s_loc, _dot_nt(qh, ck_ref[0, j])))
            else:
                out.append((_dot_nt(qh, k_ref[0, :, pair(j)]),))
        return out

    def attend(j, halves):
        if latent:
            vals = (v_ref[0, pl.ds(key0, N_LOCAL), pair(j)], cv_ref[0, j])
        else:
            vals = (v_ref[0, :, pair(j)],)
        normed = []
        for half, parts in enumerate(halves):
            m = functools.reduce(jnp.maximum, [jnp.max(s, axis=-1, keepdims=True) for s in parts])
            es = [jnp.exp2(s - m) for s in parts]
            if latent:
                oh = sum(_dot(e.astype(BF16), v * only[half] + only[1 - half])
                         for e, v in zip(es, vals))
                normed.append(oh / pltpu.roll(oh, HEAD_DIM, axis=1))
            else:
                denom = sum(jnp.sum(e, axis=-1, keepdims=True) for e in es)
                oh = sum(_dot(e.astype(BF16), v * only[half]) for e, v in zip(es, vals))
                normed.append(oh / denom)
        na_ref[:, pair(j)] = jnp.where(lane < HEAD_DIM, normed[0], normed[1]).astype(BF16)

    n_pairs = N_HEADS // 2
    pending = {j: scores(j) for j in range(min(ATTN_DEPTH, n_pairs))}

    assert POOL_GROUP == LANES and POOL_HALO >= max(POOL_WINDOWS) // 2
    pos = t_in_seq * TILE + lax.broadcasted_iota(jnp.int32, (TILE, 1), 0)
    mixed = []
    for gi, w in enumerate(POOL_WINDOWS):
        gs = slice(gi * POOL_GROUP, (gi + 1) * POOL_GROUP)
        if latent:
            pslab_ref[gi, 0:POOL_HALO] = jnp.where(t_in_seq > 0, p_prev[:, gs], 0.0)
            pslab_ref[gi, POOL_HALO + TILE:] = jnp.where(
                t_in_seq < tiles_per_seq - 1, p_next[:, gs], 0.0)
        else:
            zeros = jnp.zeros((POOL_HALO, LANES), F32)
            pslab_ref[gi, 0:POOL_HALO] = zeros
            pslab_ref[gi, POOL_HALO + TILE:] = zeros
        pslab_ref[gi, POOL_HALO:POOL_HALO + TILE] = p_ref[:, gs]
        sums = sum(pslab_ref[gi, POOL_HALO + d:POOL_HALO + d + TILE]
                   for d in range(-(w // 2), w - w // 2))
        lo = jnp.maximum(pos - w // 2, 0)
        hi = jnp.minimum(pos + (w - w // 2), seq_len)
        pooled = sums / (hi - lo).astype(F32) - p_ref[:, gs]
        mixed.append(_dot(pooled.astype(BF16), wpool_ref[gi]))
    pool_out = jnp.concatenate(mixed, axis=-1) * spool_ref[...]
    gated_pool = _sigmoid(gp_ref[...]) * _dot(pool_out.astype(BF16), wpp_ref[...])

    for j in range(n_pairs):
        if j + ATTN_DEPTH < n_pairs:
            pending[j + ATTN_DEPTH] = scores(j + ATTN_DEPTH)
        attend(j, pending.pop(j))

    b = _dot(na_ref[...], wnp_ref[...])
    merged = gated_pool + _sigmoid(gn_ref[...]) * b
    y = _dot(merged.astype(BF16), wo_ref[...])
    gate = _mod_row(mod_ref, tiles_per_seq if latent else None)[:, 2 * D_MODEL:3 * D_MODEL]
    o_ref[...] = x_ref[...] + gate * y


def _halo_specs(n_tok, rows, halo, width):
    per_tile = rows // halo
    last = n_tok // halo - 1
    return [
        pl.BlockSpec((halo, width), lambda i: (jnp.maximum(i * per_tile - 1, 0), 0)),
        pl.BlockSpec((halo, width), lambda i: (jnp.minimum((i + 1) * per_tile, last), 0)),
    ]


def _mixer_out(x, mod, k, v, cache_k, cache_v, g_mix, w_in, q_g, toeplitz,
               w_pool, s_pool, w_pp, w_np, w_o, *, seq_len):
    n_tok = x.shape[0]
    tiles_per_seq = seq_len // TILE
    assert tiles_per_seq >= 3 and seq_len // GRID_W >= KEY_ROWS
    tok = lambda i: (i, 0)
    seq3 = lambda i: (i // tiles_per_seq, 0, 0)
    seq4 = lambda i: (i // tiles_per_seq, 0, 0, 0)
    k = k.reshape(n_tok // seq_len, seq_len, D_NA)
    v = v.reshape(n_tok // seq_len, seq_len, D_NA)
    weights = (g_mix, w_in, q_g, toeplitz, w_pool, s_pool, w_pp, w_np, w_o)
    return pl.pallas_call(
        functools.partial(_mixer_out_kernel, latent=True, tiles_per_seq=tiles_per_seq),
        grid=(n_tok // TILE,),
        in_specs=[pl.BlockSpec((TILE, D_MODEL), tok)] + _halo_specs(n_tok, TILE, POOL_HALO, D_MODEL) + [
            _resident(mod.shape),
            pl.BlockSpec((1, seq_len, D_NA), seq3),
            pl.BlockSpec((1, seq_len, D_NA), seq3),
            pl.BlockSpec((1,) + cache_k.shape[1:], seq4),
            pl.BlockSpec((1,) + cache_v.shape[1:], seq4),
        ] + [_resident(w.shape) for w in weights],
        out_specs=pl.BlockSpec((TILE, D_MODEL), tok),
        out_shape=jax.ShapeDtypeStruct((n_tok, D_MODEL), F32),
        scratch_shapes=[
            pltpu.VMEM((TILE, D_NA), BF16),
            _POOL_SLABS,
            pltpu.VMEM((TILE, D_POOL), F32),
            pltpu.VMEM((TILE, D_NA), BF16),
            pltpu.VMEM((TILE, D_MODEL), F32),
            pltpu.VMEM((TILE, D_MODEL), F32),
            pltpu.VMEM((N_HEADS, TILE, N_LOCAL), F32),
        ],
        compiler_params=_params(),
        name="mixer_out",
    )(x, x, x, mod, k, v, cache_k, cache_v, *weights)


def _ff_slabs(rows):
    return pltpu.VMEM((U_SLOTS, 2, FF_CHUNK // LANES, rows + 2 * CONV_HALO, LANES), F32)


def _conv_ffn_kernel(*refs, halo, tiles_per_seq, first_step=0):
    if halo:
        (x_ref, xprev_ref, xnext_ref, mod_ref, g_ref, wup_ref, cw_ref, cb_ref, wdn_ref,
         o_ref, h_ref, u_ref, acc_ref) = refs
    else:
        (x_ref, mod_ref, g_ref, wup_ref, cw_ref, cb_ref, wdn_ref,
         o_ref, h_ref, u_ref, acc_ref) = refs
    t_in_seq = (pl.program_id(0) - first_step) % tiles_per_seq
    rows = x_ref.shape[0]
    mod = _mod_row(mod_ref, tiles_per_seq if halo else None, first_step)
    shift = mod[:, 3 * D_MODEL:4 * D_MODEL]
    scale = mod[:, 4 * D_MODEL:5 * D_MODEL]

    if halo:
        x_ext = jnp.concatenate([xprev_ref[...], x_ref[...], xnext_ref[...]], axis=0)
        u_rows = slice(0, rows + 2 * CONV_HALO)
    else:
        x_ext = x_ref[...]
        u_rows = slice(CONV_HALO, CONV_HALO + rows)
        zeros = jnp.zeros(u_ref.shape[:3] + (CONV_HALO, LANES), F32)
        u_ref[:, :, :, 0:CONV_HALO] = zeros
        u_ref[:, :, :, CONV_HALO + rows:] = zeros
    h_ref[...] = _modulated_norm(x_ext, g_ref[...], shift, scale).astype(BF16)

    def cols(c, part):
        lo = part * D_FF + c * FF_CHUNK
        return slice(lo, lo + FF_CHUNK)

    n_slabs = FF_CHUNK // LANES

    def up_proj(c):
        for part in range(2):
            u = _dot(h_ref[...], wup_ref[:, cols(c, part)])
            for k in range(n_slabs):
                u_ref[c % U_SLOTS, part, k, u_rows] = u[:, k * LANES:(k + 1) * LANES]
                if halo:
                    last_prev = CONV_HALO - 1
                    first_next = CONV_HALO + rows
                    u_ref[c % U_SLOTS, part, k, last_prev:last_prev + 1] = jnp.where(
                        t_in_seq == 0, 0.0, u[last_prev:last_prev + 1, k * LANES:(k + 1) * LANES])
                    u_ref[c % U_SLOTS, part, k, first_next:first_next + 1] = jnp.where(
                        t_in_seq == tiles_per_seq - 1, 0.0,
                        u[first_next:first_next + 1, k * LANES:(k + 1) * LANES])

    def conv(c, part):
        slabs = []
        for k in range(n_slabs):
            lo = cols(c, part).start + k * LANES
            sl = slice(lo, lo + LANES)
            u_prev = u_ref[c % U_SLOTS, part, k, CONV_HALO - 1:CONV_HALO - 1 + rows]
            u_here = u_ref[c % U_SLOTS, part, k, CONV_HALO:CONV_HALO + rows]
            u_next = u_ref[c % U_SLOTS, part, k, CONV_HALO + 1:CONV_HALO + 1 + rows]
            slabs.append(u_prev * cw_ref[0:1, sl] + u_here * cw_ref[1:2, sl]
                         + u_next * cw_ref[2:3, sl] + cb_ref[:, sl])
        return jnp.concatenate(slabs, axis=-1)

    def gated_down(c):
        a = conv(c, 0)
        act = (a * _sigmoid(a) * conv(c, 1)).astype(BF16)
        return _dot(act, wdn_ref[c * FF_CHUNK:(c + 1) * FF_CHUNK, :])

    for c in range(min(PIPE_DEPTH, N_FF_CHUNKS)):
        up_proj(c)
    for c in range(N_FF_CHUNKS):
        if c + PIPE_DEPTH < N_FF_CHUNKS:
            up_proj(c + PIPE_DEPTH)
        part = gated_down(c)
        if c == 0:
            acc_ref[...] = part
        elif c + 1 < N_FF_CHUNKS:
            acc_ref[...] += part
        else:
            o_ref[...] = x_ref[...] + mod[:, 5 * D_MODEL:6 * D_MODEL] * (acc_ref[...] + part)


def _block_seq_kernel(x_ref, xl_ref, xlprev_ref, xlnext_ref, mod_ref,
                      gmix_ref, win_ref, qg_ref, kg_ref,
                      wpool_ref, spool_ref, wpp_ref, wnp_ref, wo_ref,
                      gffn_ref, wup_ref, cw_ref, cb_ref, wdn_ref,
                      o_ref, nk_ref, nv_ref, ol_ref,
                      p_ref, q_ref, k_ref, v_ref, gp_ref, gn_ref, na_ref, pslab_ref,
                      x1_ref, h_ref, hl_ref, u_ref, acc_ref, *, n_ctx_tiles, tiles_per_latent_seq):
    @pl.when(pl.program_id(0) < n_ctx_tiles)
    def _():
        _mixer_in_kernel(x_ref, mod_ref, gmix_ref, win_ref, qg_ref, kg_ref,
                         p_ref, q_ref, k_ref.at[0], v_ref.at[0], gp_ref, gn_ref, nk_ref, nv_ref)
        _mixer_out_kernel(x_ref, mod_ref, p_ref, q_ref, k_ref, v_ref, gp_ref, gn_ref,
                          wpool_ref, spool_ref, wpp_ref, wnp_ref, wo_ref, x1_ref, na_ref,
                          pslab_ref, latent=False, tiles_per_seq=1)
        _conv_ffn_kernel(x1_ref, mod_ref, gffn_ref, wup_ref, cw_ref, cb_ref, wdn_ref,
                         o_ref, h_ref, u_ref, acc_ref, halo=False, tiles_per_seq=1)

    @pl.when(pl.program_id(0) >= n_ctx_tiles)
    def _():
        _conv_ffn_kernel(xl_ref, xlprev_ref, xlnext_ref, mod_ref, gffn_ref, wup_ref, cw_ref, cb_ref,
                         wdn_ref, ol_ref, hl_ref, u_ref, acc_ref, halo=True,
                         tiles_per_seq=tiles_per_latent_seq, first_step=n_ctx_tiles)


def _block_seq(x, xl, mod, g_mix, w_in, q_g, k_g, w_pool, s_pool, w_pp, w_np, w_o,
               g_ffn, w_up, conv_w, conv_b, w_down, *, latent_seq_len):
    n_tok, n_lat = x.shape[0], xl.shape[0]
    n_tiles, n_lat_tiles = n_tok // TILE, n_lat // TILE
    per_tile = TILE // CONV_HALO
    last_halo = n_lat // CONV_HALO - 1
    lat = lambda i: jnp.clip(i - n_tiles, 0, n_lat_tiles - 1)
    ctx_tile = lambda i: (jnp.minimum(i, n_tiles - 1), 0)
    lat_tile = lambda i: (lat(i), 0)
    kv_shape = jax.ShapeDtypeStruct((n_tiles, 1, N_HEADS, TILE, HEAD_DIM), F32)
    kv_spec = pl.BlockSpec((1, 1, N_HEADS, TILE, HEAD_DIM),
                           lambda i: (jnp.minimum(i, n_tiles - 1), 0, 0, 0, 0))
    weights = (g_mix, w_in, q_g, k_g, w_pool, s_pool, w_pp, w_np, w_o,
               g_ffn, w_up, conv_w, conv_b, w_down)
    return pl.pallas_call(
        functools.partial(_block_seq_kernel, n_ctx_tiles=n_tiles,
                          tiles_per_latent_seq=latent_seq_len // TILE),
        grid=(n_tiles + n_lat_tiles,),
        in_specs=[
            pl.BlockSpec((TILE, D_MODEL), ctx_tile),
            pl.BlockSpec((TILE, D_MODEL), lat_tile),
            pl.BlockSpec((CONV_HALO, D_MODEL),
                         lambda i: (jnp.maximum(lat(i) * per_tile - 1, 0), 0)),
            pl.BlockSpec((CONV_HALO, D_MODEL),
                         lambda i: (jnp.minimum((lat(i) + 1) * per_tile, last_halo), 0)),
        ] + [_resident(w.shape) for w in (mod,) + weights],
        out_specs=[pl.BlockSpec((TILE, D_MODEL), ctx_tile), kv_spec, kv_spec,
                   pl.BlockSpec((TILE, D_MODEL), lat_tile)],
        out_shape=[jax.ShapeDtypeStruct((n_tok, D_MODEL), F32), kv_shape, kv_shape,
                   jax.ShapeDtypeStruct((n_lat, D_MODEL), F32)],
        scratch_shapes=[
            pltpu.VMEM((TILE, D_POOL), F32),
            pltpu.VMEM((TILE, D_NA), BF16),
            pltpu.VMEM((1, TILE, D_NA), BF16),
            pltpu.VMEM((1, TILE, D_NA), BF16),
            pltpu.VMEM((TILE, D_MODEL), F32),
            pltpu.VMEM((TILE, D_MODEL), F32),
            pltpu.VMEM((TILE, D_NA), BF16),
            _POOL_SLABS,
            pltpu.VMEM((TILE, D_MODEL), F32),
            pltpu.VMEM((TILE, D_MODEL), BF16),
            pltpu.VMEM((TILE + 2 * CONV_HALO, D_MODEL), BF16),
            _ff_slabs(TILE),
            pltpu.VMEM((TILE, D_MODEL), F32),
        ],
        compiler_params=_params(),
        name="block_seq",
    )(x, xl, xl, xl, mod, *weights)


def _toeplitz_bias(rpb):
    n_dr, n_dc = 2 * WIN_H - 1, 2 * WIN_W - 1
    col = np.arange(GRID_W)
    dc = col[None, :] - col[:, None] + (WIN_W - 1)
    win_start = np.clip(col - WIN_W // 2, 0, GRID_W - WIN_W)
    col_ok = (col[None, :] >= win_start[:, None]) & (col[None, :] < win_start[:, None] + WIN_W)
    col_sel = (dc[None] == np.arange(n_dc)[:, None, None]).astype(np.float32)
    col_sel = np.concatenate([col_sel, col_sel], axis=-1)
    valid = np.concatenate([col_ok, col_ok], axis=-1)[None] & (np.arange(n_dr + 1) < n_dr)[:, None, None]
    rows = jnp.pad(rpb, ((0, 0), (0, 1), (0, 0)))
    blocks = jnp.einsum("hdc,cqk->hdqk", rows, col_sel, precision=lax.Precision.HIGHEST)
    return jnp.where(valid[None], blocks * LOG2_E, NEG_INF)


def _pair_heads(t):
    return jnp.concatenate([t[:, 0::2], t[:, 1::2]], axis=-1).astype(BF16)


def kernel(x_prompt, x_sample, cache_k, cache_v, c, c_ctx, norm_mix_g, norm_ffn_g, w_mod, b_mod,
           w_in, q_norm_g, k_norm_g, pool_w, pool_scale, na_rpb, w_pool_proj, w_na_proj, w_o,
           w_up, ffn_conv_w, ffn_conv_b, w_down):
    depth = w_in.shape[0]
    assert depth == 1
    batch, seq, _ = x_prompt.shape
    dec_batch, dec_seq, _ = x_sample.shape
    assert seq == TILE and dec_seq % KV_ROWS == 0 and dec_batch + 1 <= MOD_ROWS
    l = 0

    cond = jnp.concatenate(
        [c_ctx[None], c, jnp.zeros((MOD_ROWS - 1 - dec_batch, D_MODEL), F32)], axis=0)
    mod = _modulation(cond, w_mod[l], b_mod[l])

    w_in_b = w_in[l].astype(BF16)
    w_pool_b = pool_w[l].astype(BF16)
    w_pp_b = w_pool_proj[l].astype(BF16)
    w_np_b = w_na_proj[l].astype(BF16)
    w_o_b = w_o[l].astype(BF16)
    w_up_b = w_up[l].astype(BF16)
    w_down_b = w_down[l].astype(BF16)
    conv_w = ffn_conv_w[l]
    g_mix = norm_mix_g[l].reshape(1, D_MODEL)
    g_ffn = norm_ffn_g[l].reshape(1, D_MODEL)
    q_g = jnp.tile(q_norm_g[l], N_HEADS).reshape(1, D_NA)
    k_g = jnp.tile(k_norm_g[l], N_HEADS).reshape(1, D_NA)
    s_pool = pool_scale[l].reshape(1, D_POOL)
    conv_b = ffn_conv_b[l].reshape(1, 2 * D_FF)

    xs = x_sample.reshape(dec_batch * dec_seq, D_MODEL)
    k, v = _kv_proj(xs, mod, g_mix, w_in_b, k_g, seq_len=dec_seq, rows=KV_ROWS)
    xs = _mixer_out(xs, mod, k, v, _pair_heads(cache_k[:, l]), _pair_heads(cache_v[:, l]),
                    g_mix, w_in_b, q_g, _toeplitz_bias(na_rpb[l]),
                    w_pool_b, s_pool, w_pp_b, w_np_b, w_o_b, seq_len=dec_seq)

    xc = x_prompt.reshape(batch * seq, D_MODEL)
    xc, new_k, new_v, xs = _block_seq(xc, xs, mod, g_mix, w_in_b, q_g, k_g,
                                      w_pool_b, s_pool, w_pp_b, w_np_b, w_o_b,
                                      g_ffn, w_up_b, conv_w, conv_b, w_down_b,
                                      latent_seq_len=dec_seq)
    y_prompt = xc.reshape(batch, seq, D_MODEL)
    y_sample = xs.reshape(dec_batch, dec_seq, D_MODEL)
    return (y_prompt, y_sample, new_k, new_v)
```

```python
import functools

import jax
import jax.numpy as jnp
import numpy as np
from jax import lax
from jax.experimental import pallas as pl
from jax.experimental.pallas import tpu as pltpu

D_MODEL = 1024
GRID_W = 64
N_HEADS = 8
HEAD_DIM = 64
D_NA = N_HEADS * HEAD_DIM
D_POOL = 512
POOL_WINDOWS = (2, 4, 8, 16)
POOL_GROUP = D_POOL // len(POOL_WINDOWS)
WIN_H = 8
WIN_W = 16
D_FF = 2816
Q_BLOCK = 128
EPS = 1e-6
NEG_INF = -1e30
LOG2_E = 1.4426950408889634
OFF_Q = D_POOL
OFF_K = D_POOL + D_NA
OFF_V = D_POOL + 2 * D_NA
OFF_GP = D_POOL + 3 * D_NA
OFF_GN = OFF_GP + D_MODEL
D_IN = OFF_GN + D_MODEL

TILE = 256
TILE_ROWS = TILE // GRID_W
KEY_ROWS = TILE_ROWS + WIN_H
N_LOCAL = KEY_ROWS * GRID_W
SUBLANES = 8
LANES = 128
POOL_HALO = SUBLANES
CONV_HALO = SUBLANES
FF_CHUNK = 256
N_FF_CHUNKS = D_FF // FF_CHUNK
PIPE_DEPTH = 8
U_SLOTS = PIPE_DEPTH + 1
ATTN_DEPTH = 1
KV_ROWS = 1024
CONV_FFN_ROWS = 512
MOD_ROWS = 8
MOD_TILE = 1536
VMEM_LIMIT = 56 * 1024 * 1024

BF16 = jnp.bfloat16
F32 = jnp.float32


def _dot(a, b):
    return jnp.dot(a, b, preferred_element_type=F32)


def _dot_nt(a, b):
    return lax.dot_general(a, b, (((1,), (1,)), ((), ())), preferred_element_type=F32)


def _modulated_norm(x, g, shift, scale):
    ms = jnp.mean(x * x, axis=-1, keepdims=True)
    return (x * lax.rsqrt(ms + EPS) * g) * (1.0 + scale) + shift


def _sigmoid(x):
    return 1.0 / (1.0 + jnp.exp(-x))


def _mod_row(mod_ref, steps_per_cond):
    if steps_per_cond is None:
        return mod_ref[0:1, :]
    return mod_ref[pl.ds(1 + pl.program_id(0) // steps_per_cond, 1), :]


def _params(**kwargs):
    return pltpu.CompilerParams(
        dimension_semantics=("arbitrary",), vmem_limit_bytes=VMEM_LIMIT, **kwargs)


def _resident(shape):
    return pl.BlockSpec(shape, lambda i: (0,) * len(shape), pipeline_mode=pl.Buffered(1))


def _mod_kernel(cond_ref, w_ref, b_ref, o_ref):
    cnd = cond_ref[...]
    s = (cnd * _sigmoid(cnd)).astype(BF16)
    o_ref[...] = _dot(s, w_ref[...].astype(BF16)) + b_ref[...]


def _modulation(cond, w_mod, b_mod):
    n = w_mod.shape[1]
    return pl.pallas_call(
        _mod_kernel,
        grid=(n // MOD_TILE,),
        in_specs=[
            pl.BlockSpec((MOD_ROWS, D_MODEL), lambda j: (0, 0)),
            pl.BlockSpec((D_MODEL, MOD_TILE), lambda j: (0, j)),
            pl.BlockSpec((1, MOD_TILE), lambda j: (0, j)),
        ],
        out_specs=pl.BlockSpec((MOD_ROWS, MOD_TILE), lambda j: (0, j)),
        out_shape=jax.ShapeDtypeStruct((MOD_ROWS, n), F32),
        compiler_params=_params(),
        name="modulation",
    )(cond, w_mod, b_mod.reshape(1, n))


def _head_norm(t, g):
    assert 2 * HEAD_DIM == LANES
    first_head = lax.broadcasted_iota(jnp.int32, (1, LANES), 1) < HEAD_DIM
    tt = t * t
    blocks = []
    for j in range(D_NA // LANES):
        blk = tt[:, j * LANES:(j + 1) * LANES]
        both = jnp.sum(blk, axis=-1, keepdims=True)
        one = jnp.sum(jnp.where(first_head, blk, 0.0), axis=-1, keepdims=True)
        blocks.append(jnp.where(first_head, one, both - one))
    ss = jnp.concatenate(blocks, axis=-1)
    return t * lax.rsqrt(ss * (1.0 / HEAD_DIM) + EPS) * g


def _scaled_query(q, g):
    return (_head_norm(q, g) * (HEAD_DIM ** -0.5 * LOG2_E)).astype(BF16)


def _mixer_norm(x_ref, mod_ref, g_ref, steps_per_cond):
    mod = _mod_row(mod_ref, steps_per_cond)
    h = _modulated_norm(x_ref[...], g_ref[...], mod[:, 0:D_MODEL], mod[:, D_MODEL:2 * D_MODEL])
    return h.astype(BF16)


def _mixer_in_kernel(x_ref, mod_ref, g_ref, w_ref, qg_ref, kg_ref,
                     p_ref, q_ref, k_ref, v_ref, gp_ref, gn_ref, nk_ref, nv_ref):
    h = _mixer_norm(x_ref, mod_ref, g_ref, None)

    def proj(lo, hi):
        return _dot(h, w_ref[:, lo:hi])

    q = proj(OFF_Q, OFF_K)
    k = proj(OFF_K, OFF_V)
    p_ref[...] = proj(0, OFF_Q)
    v = proj(OFF_V, OFF_GP)
    v_ref[...] = v.astype(BF16)
    q = _scaled_query(q, qg_ref[...])
    gp_ref[...] = proj(OFF_GP, OFF_GN)
    k = _head_norm(k, kg_ref[...])
    k_ref[...] = k.astype(BF16)
    gn_ref[...] = proj(OFF_GN, D_IN)
    t = lax.broadcasted_iota(jnp.int32, (TILE, TILE), 0)
    j = lax.broadcasted_iota(jnp.int32, (TILE, TILE), 1)
    n_blocks = TILE // Q_BLOCK
    perm = jnp.where(j == (t % n_blocks) * Q_BLOCK + t // n_blocks, 1.0, 0.0).astype(BF16)
    q_ref[...] = _dot(perm, q).astype(BF16)
    for hd in range(N_HEADS):
        sl = slice(hd * HEAD_DIM, (hd + 1) * HEAD_DIM)
        nk_ref[0, 0, hd] = k[:, sl]
        nv_ref[0, 0, hd] = v[:, sl]


def _kv_kernel(x_ref, mod_ref, g_ref, wkv_ref, kg_ref, k_ref, v_ref, *, steps_per_cond):
    h = _mixer_norm(x_ref, mod_ref, g_ref, steps_per_cond)
    k_ref[...] = _head_norm(_dot(h, wkv_ref[:, 0:D_NA]), kg_ref[...]).astype(BF16)
    v_ref[...] = _dot(h, wkv_ref[:, D_NA:2 * D_NA]).astype(BF16)


def _kv_proj(x, mod, norm_g, w_in, k_g, *, seq_len, rows):
    n_tok = x.shape[0]
    tok = lambda i: (i, 0)
    assert OFF_K % (2 * D_NA) == 0
    kv_cols = pl.BlockSpec((D_MODEL, 2 * D_NA), lambda i: (0, OFF_K // (2 * D_NA)),
                           pipeline_mode=pl.Buffered(1))
    return pl.pallas_call(
        functools.partial(_kv_kernel, steps_per_cond=seq_len // rows),
        grid=(n_tok // rows,),
        in_specs=[pl.BlockSpec((rows, D_MODEL), tok), _resident(mod.shape), _resident(norm_g.shape),
                  kv_cols, _resident(k_g.shape)],
        out_specs=[pl.BlockSpec((rows, D_NA), tok)] * 2,
        out_shape=[jax.ShapeDtypeStruct((n_tok, D_NA), BF16)] * 2,
        compiler_params=_params(),
        name="kv_proj",
    )(x, mod, norm_g, w_in, k_g)


_POOL_SLABS = pltpu.VMEM((len(POOL_WINDOWS), TILE + 2 * POOL_HALO, LANES), F32)


def _key_row0(tile, n_rows):
    return jnp.clip(tile * TILE_ROWS - WIN_H // 2, 0, n_rows - KEY_ROWS)


def _fill_local_bias(bias_ref, toep_ref, tile, n_rows):
    key_row0 = int(np.clip(tile * TILE_ROWS - WIN_H // 2, 0, n_rows - KEY_ROWS))
    for a in range(TILE_ROWS):
        r = tile * TILE_ROWS + a
        row_start = int(np.clip(r - WIN_H // 2, 0, n_rows - WIN_H))
        for b in range(KEY_ROWS):
            kr = key_row0 + b
            dr = kr - r + (WIN_H - 1) if row_start <= kr < row_start + WIN_H else 2 * WIN_H - 1
            half = slice((b % 2) * GRID_W, (b % 2 + 1) * GRID_W)
            bias_ref[:, a * GRID_W:(a + 1) * GRID_W, b * GRID_W:(b + 1) * GRID_W] = (
                toep_ref[:, dr, :, half])


def _mixer_out_kernel(*refs, latent, tiles_per_seq):
    if latent:
        (x_ref, xprev_ref, xnext_ref, mod_ref, k_ref, v_ref, ck_ref, cv_ref,
         gmix_ref, win_ref, qg_ref, toep_ref, wpool_ref, spool_ref, wpp_ref, wnp_ref, wo_ref,
         o_ref, na_ref, pslab_ref, p_ref, q_ref, gp_ref, gn_ref, bias_ref) = refs
    else:
        (x_ref, mod_ref, p_ref, q_ref, k_ref, v_ref, gp_ref, gn_ref,
         wpool_ref, spool_ref, wpp_ref, wnp_ref, wo_ref, o_ref, na_ref, pslab_ref) = refs
    i = pl.program_id(0)
    t_in_seq = i % tiles_per_seq
    seq_len = TILE * tiles_per_seq

    if latent:
        n_rows = seq_len // GRID_W
        key0 = pl.multiple_of(_key_row0(t_in_seq, n_rows) * GRID_W, TILE)
        for tile in (0, 1, tiles_per_seq - 1):
            pl.when(t_in_seq == tile)(
                functools.partial(_fill_local_bias, bias_ref, toep_ref, tile, n_rows))

        mod = _mod_row(mod_ref, tiles_per_seq)
        x_ext = jnp.concatenate([x_ref[...], xprev_ref[...], xnext_ref[...]], axis=0)
        h_ext = _modulated_norm(x_ext, gmix_ref[...], mod[:, 0:D_MODEL],
                                mod[:, D_MODEL:2 * D_MODEL]).astype(BF16)
        h = h_ext[:TILE]
        q_ref[...] = _scaled_query(_dot(h, win_ref[:, OFF_Q:OFF_K]), qg_ref[...])
        p_ext = _dot(h_ext, win_ref[:, 0:OFF_Q])
        p_ref[...] = p_ext[:TILE]
        p_prev, p_next = p_ext[TILE:TILE + POOL_HALO], p_ext[TILE + POOL_HALO:]
        gp_ref[...] = _dot(h, win_ref[:, OFF_GP:OFF_GN])
        gn_ref[...] = _dot(h, win_ref[:, OFF_GN:D_IN])

    assert 2 * HEAD_DIM == LANES
    lane = lax.broadcasted_iota(jnp.int32, (1, LANES), 1)
    only = [jnp.where(lane < HEAD_DIM, 1.0, 0.0).astype(BF16),
            jnp.where(lane < HEAD_DIM, 0.0, 1.0).astype(BF16)]

    def pair(j):
        return slice(j * LANES, (j + 1) * LANES)

    def scores(j):
        qp = q_ref[:, pair(j)]
        out = []
        for half in range(2):
            qh = qp * only[half]
            if latent:
                s_loc = (_dot_nt(qh, k_ref[0, pl.ds(key0, N_LOCAL), pair(j)])
                         + bias_ref[2 * j + half])
                out.append((s_loc, _dot_nt(qh, ck_ref[0, j])))
            else:
                out.append((_dot_nt(qh, k_ref[0, :, pair(j)]),))
        return out

    def attend(j, halves):
        if latent:
            vals = (v_ref[0, pl.ds(key0, N_LOCAL), pair(j)], cv_ref[0, j])
        else:
            vals = (v_ref[0, :, pair(j)],)
        normed = []
        for half, parts in enumerate(halves):
            m = functools.reduce(jnp.maximum, [jnp.max(s, axis=-1, keepdims=True) for s in parts])
            es = [jnp.exp2(s - m) for s in parts]
            if latent:
                oh = sum(_dot(e.astype(BF16), v * only[half] + only[1 - half])
                         for e, v in zip(es, vals))
                normed.append(oh / pltpu.roll(oh, HEAD_DIM, axis=1))
            else:
                denom = sum(jnp.sum(e, axis=-1, keepdims=True) for e in es)
                oh = sum(_dot(e.astype(BF16), v * only[half]) for e, v in zip(es, vals))
                normed.append(oh / denom)
        na_ref[:, pair(j)] = jnp.where(lane < HEAD_DIM, normed[0], normed[1]).astype(BF16)

    n_pairs = N_HEADS // 2
    pending = {j: scores(j) for j in range(min(ATTN_DEPTH, n_pairs))}

    assert POOL_GROUP == LANES and POOL_HALO >= max(POOL_WINDOWS) // 2
    pos = t_in_seq * TILE + lax.broadcasted_iota(jnp.int32, (TILE, 1), 0)
    mixed = []
    for gi, w in enumerate(POOL_WINDOWS):
        gs = slice(gi * POOL_GROUP, (gi + 1) * POOL_GROUP)
        if latent:
            pslab_ref[gi, 0:POOL_HALO] = jnp.where(t_in_seq > 0, p_prev[:, gs], 0.0)
            pslab_ref[gi, POOL_HALO + TILE:] = jnp.where(
                t_in_seq < tiles_per_seq - 1, p_next[:, gs], 0.0)
        else:
            zeros = jnp.zeros((POOL_HALO, LANES), F32)
            pslab_ref[gi, 0:POOL_HALO] = zeros
            pslab_ref[gi, POOL_HALO + TILE:] = zeros
        pslab_ref[gi, POOL_HALO:POOL_HALO + TILE] = p_ref[:, gs]
        sums = sum(pslab_ref[gi, POOL_HALO + d:POOL_HALO + d + TILE]
                   for d in range(-(w // 2), w - w // 2))
        lo = jnp.maximum(pos - w // 2, 0)
        hi = jnp.minimum(pos + (w - w // 2), seq_len)
        pooled = sums / (hi - lo).astype(F32) - p_ref[:, gs]
        mixed.append(_dot(pooled.astype(BF16), wpool_ref[gi]))
    pool_out = jnp.concatenate(mixed, axis=-1) * spool_ref[...]
    gated_pool = _sigmoid(gp_ref[...]) * _dot(pool_out.astype(BF16), wpp_ref[...])

    for j in range(n_pairs):
        if j + ATTN_DEPTH < n_pairs:
            pending[j + ATTN_DEPTH] = scores(j + ATTN_DEPTH)
        attend(j, pending.pop(j))

    b = _dot(na_ref[...], wnp_ref[...])
    merged = gated_pool + _sigmoid(gn_ref[...]) * b
    y = _dot(merged.astype(BF16), wo_ref[...])
    gate = _mod_row(mod_ref, tiles_per_seq if latent else None)[:, 2 * D_MODEL:3 * D_MODEL]
    o_ref[...] = x_ref[...] + gate * y


def _halo_specs(n_tok, rows, halo, width):
    per_tile = rows // halo
    last = n_tok // halo - 1
    return [
        pl.BlockSpec((halo, width), lambda i: (jnp.maximum(i * per_tile - 1, 0), 0)),
        pl.BlockSpec((halo, width), lambda i: (jnp.minimum((i + 1) * per_tile, last), 0)),
    ]


def _mixer_out(x, mod, k, v, cache_k, cache_v, g_mix, w_in, q_g, toeplitz,
               w_pool, s_pool, w_pp, w_np, w_o, *, seq_len):
    n_tok = x.shape[0]
    tiles_per_seq = seq_len // TILE
    assert tiles_per_seq >= 3 and seq_len // GRID_W >= KEY_ROWS
    tok = lambda i: (i, 0)
    seq3 = lambda i: (i // tiles_per_seq, 0, 0)
    seq4 = lambda i: (i // tiles_per_seq, 0, 0, 0)
    k = k.reshape(n_tok // seq_len, seq_len, D_NA)
    v = v.reshape(n_tok // seq_len, seq_len, D_NA)
    weights = (g_mix, w_in, q_g, toeplitz, w_pool, s_pool, w_pp, w_np, w_o)
    return pl.pallas_call(
        functools.partial(_mixer_out_kernel, latent=True, tiles_per_seq=tiles_per_seq),
        grid=(n_tok // TILE,),
        in_specs=[pl.BlockSpec((TILE, D_MODEL), tok)] + _halo_specs(n_tok, TILE, POOL_HALO, D_MODEL) + [
            _resident(mod.shape),
            pl.BlockSpec((1, seq_len, D_NA), seq3),
            pl.BlockSpec((1, seq_len, D_NA), seq3),
            pl.BlockSpec((1,) + cache_k.shape[1:], seq4),
            pl.BlockSpec((1,) + cache_v.shape[1:], seq4),
        ] + [_resident(w.shape) for w in weights],
        out_specs=pl.BlockSpec((TILE, D_MODEL), tok),
        out_shape=jax.ShapeDtypeStruct((n_tok, D_MODEL), F32),
        scratch_shapes=[
            pltpu.VMEM((TILE, D_NA), BF16),
            _POOL_SLABS,
            pltpu.VMEM((TILE, D_POOL), F32),
            pltpu.VMEM((TILE, D_NA), BF16),
            pltpu.VMEM((TILE, D_MODEL), F32),
            pltpu.VMEM((TILE, D_MODEL), F32),
            pltpu.VMEM((N_HEADS, TILE, N_LOCAL), F32),
        ],
        compiler_params=_params(),
        name="mixer_out",
    )(x, x, x, mod, k, v, cache_k, cache_v, *weights)


def _ff_slabs(rows):
    return pltpu.VMEM((U_SLOTS, 2, FF_CHUNK // LANES, rows + 2 * CONV_HALO, LANES), F32)


def _conv_ffn_kernel(*refs, halo, tiles_per_seq):
    if halo:
        (x_ref, xprev_ref, xnext_ref, mod_ref, g_ref, wup_ref, cw_ref, cb_ref, wdn_ref,
         o_ref, h_ref, u_ref, acc_ref) = refs
    else:
        (x_ref, mod_ref, g_ref, wup_ref, cw_ref, cb_ref, wdn_ref,
         o_ref, h_ref, u_ref, acc_ref) = refs
    i = pl.program_id(0)
    t_in_seq = i % tiles_per_seq
    rows = x_ref.shape[0]
    mod = _mod_row(mod_ref, tiles_per_seq if halo else None)
    shift = mod[:, 3 * D_MODEL:4 * D_MODEL]
    scale = mod[:, 4 * D_MODEL:5 * D_MODEL]

    if halo:
        x_ext = jnp.concatenate([xprev_ref[...], x_ref[...], xnext_ref[...]], axis=0)
        u_rows = slice(0, rows + 2 * CONV_HALO)
    else:
        x_ext = x_ref[...]
        u_rows = slice(CONV_HALO, CONV_HALO + rows)
        zeros = jnp.zeros(u_ref.shape[:3] + (CONV_HALO, LANES), F32)
        u_ref[:, :, :, 0:CONV_HALO] = zeros
        u_ref[:, :, :, CONV_HALO + rows:] = zeros
    h_ref[...] = _modulated_norm(x_ext, g_ref[...], shift, scale).astype(BF16)

    def cols(c, part):
        lo = part * D_FF + c * FF_CHUNK
        return slice(lo, lo + FF_CHUNK)

    n_slabs = FF_CHUNK // LANES

    def up_proj(c):
        for part in range(2):
            u = _dot(h_ref[...], wup_ref[:, cols(c, part)])
            for k in range(n_slabs):
                u_ref[c % U_SLOTS, part, k, u_rows] = u[:, k * LANES:(k + 1) * LANES]
                if halo:
                    last_prev = CONV_HALO - 1
                    first_next = CONV_HALO + rows
                    u_ref[c % U_SLOTS, part, k, last_prev:last_prev + 1] = jnp.where(
                        t_in_seq == 0, 0.0, u[last_prev:last_prev + 1, k * LANES:(k + 1) * LANES])
                    u_ref[c % U_SLOTS, part, k, first_next:first_next + 1] = jnp.where(
                        t_in_seq == tiles_per_seq - 1, 0.0,
                        u[first_next:first_next + 1, k * LANES:(k + 1) * LANES])

    def conv(c, part):
        slabs = []
        for k in range(n_slabs):
            lo = cols(c, part).start + k * LANES
            sl = slice(lo, lo + LANES)
            u_prev = u_ref[c % U_SLOTS, part, k, CONV_HALO - 1:CONV_HALO - 1 + rows]
            u_here = u_ref[c % U_SLOTS, part, k, CONV_HALO:CONV_HALO + rows]
            u_next = u_ref[c % U_SLOTS, part, k, CONV_HALO + 1:CONV_HALO + 1 + rows]
            slabs.append(u_prev * cw_ref[0:1, sl] + u_here * cw_ref[1:2, sl]
                         + u_next * cw_ref[2:3, sl] + cb_ref[:, sl])
        return jnp.concatenate(slabs, axis=-1)

    def gated_down(c):
        a = conv(c, 0)
        act = (a * _sigmoid(a) * conv(c, 1)).astype(BF16)
        return _dot(act, wdn_ref[c * FF_CHUNK:(c + 1) * FF_CHUNK, :])

    for c in range(min(PIPE_DEPTH, N_FF_CHUNKS)):
        up_proj(c)
    for c in range(N_FF_CHUNKS):
        if c + PIPE_DEPTH < N_FF_CHUNKS:
            up_proj(c + PIPE_DEPTH)
        part = gated_down(c)
        if c == 0:
            acc_ref[...] = part
        elif c + 1 < N_FF_CHUNKS:
            acc_ref[...] += part
        else:
            o_ref[...] = x_ref[...] + mod[:, 5 * D_MODEL:6 * D_MODEL] * (acc_ref[...] + part)


def _conv_ffn(x, mod, norm_g, w_up, conv_w, conv_b, w_down, *, seq_len, rows):
    n_tok = x.shape[0]
    tiles_per_seq = seq_len // rows
    assert tiles_per_seq > 1
    tok = lambda i: (i, 0)
    weights = (norm_g, w_up, conv_w, conv_b, w_down)
    return pl.pallas_call(
        functools.partial(_conv_ffn_kernel, halo=True, tiles_per_seq=tiles_per_seq),
        grid=(n_tok // rows,),
        in_specs=[pl.BlockSpec((rows, D_MODEL), tok)] + _halo_specs(n_tok, rows, CONV_HALO, D_MODEL) + [
            _resident(w.shape) for w in (mod,) + weights],
        out_specs=pl.BlockSpec((rows, D_MODEL), tok),
        out_shape=jax.ShapeDtypeStruct((n_tok, D_MODEL), F32),
        scratch_shapes=[
            pltpu.VMEM((rows + 2 * CONV_HALO, D_MODEL), BF16),
            _ff_slabs(rows),
            pltpu.VMEM((rows, D_MODEL), F32),
        ],
        compiler_params=_params(),
        name="conv_ffn",
    )(x, x, x, mod, *weights)


def _block_seq_kernel(x_ref, mod_ref, gmix_ref, win_ref, qg_ref, kg_ref,
                      wpool_ref, spool_ref, wpp_ref, wnp_ref, wo_ref,
                      gffn_ref, wup_ref, cw_ref, cb_ref, wdn_ref,
                      o_ref, nk_ref, nv_ref,
                      p_ref, q_ref, k_ref, v_ref, gp_ref, gn_ref, na_ref, pslab_ref,
                      x1_ref, h_ref, u_ref, acc_ref):
    _mixer_in_kernel(x_ref, mod_ref, gmix_ref, win_ref, qg_ref, kg_ref,
                     p_ref, q_ref, k_ref.at[0], v_ref.at[0], gp_ref, gn_ref, nk_ref, nv_ref)
    _mixer_out_kernel(x_ref, mod_ref, p_ref, q_ref, k_ref, v_ref, gp_ref, gn_ref,
                      wpool_ref, spool_ref, wpp_ref, wnp_ref, wo_ref, x1_ref, na_ref, pslab_ref,
                      latent=False, tiles_per_seq=1)
    _conv_ffn_kernel(x1_ref, mod_ref, gffn_ref, wup_ref, cw_ref, cb_ref, wdn_ref,
                     o_ref, h_ref, u_ref, acc_ref, halo=False, tiles_per_seq=1)


def _block_seq(x, mod, g_mix, w_in, q_g, k_g, w_pool, s_pool, w_pp, w_np, w_o,
               g_ffn, w_up, conv_w, conv_b, w_down):
    n_tok = x.shape[0]
    n_tiles = n_tok // TILE
    tok = lambda i: (i, 0)
    kv_shape = jax.ShapeDtypeStruct((n_tiles, 1, N_HEADS, TILE, HEAD_DIM), F32)
    kv_spec = pl.BlockSpec((1, 1, N_HEADS, TILE, HEAD_DIM), lambda i: (i, 0, 0, 0, 0))
    weights = (g_mix, w_in, q_g, k_g, w_pool, s_pool, w_pp, w_np, w_o,
               g_ffn, w_up, conv_w, conv_b, w_down)
    return pl.pallas_call(
        _block_seq_kernel,
        grid=(n_tiles,),
        in_specs=[
            pl.BlockSpec((TILE, D_MODEL), tok),
        ] + [_resident(w.shape) for w in (mod,) + weights],
        out_specs=[pl.BlockSpec((TILE, D_MODEL), tok), kv_spec, kv_spec],
        out_shape=[jax.ShapeDtypeStruct((n_tok, D_MODEL), F32), kv_shape, kv_shape],
        scratch_shapes=[
            pltpu.VMEM((TILE, D_POOL), F32),
            pltpu.VMEM((TILE, D_NA), BF16),
            pltpu.VMEM((1, TILE, D_NA), BF16),
            pltpu.VMEM((1, TILE, D_NA), BF16),
            pltpu.VMEM((TILE, D_MODEL), F32),
            pltpu.VMEM((TILE, D_MODEL), F32),
            pltpu.VMEM((TILE, D_NA), BF16),
            _POOL_SLABS,
            pltpu.VMEM((TILE, D_MODEL), F32),
            pltpu.VMEM((TILE, D_MODEL), BF16),
            _ff_slabs(TILE),
            pltpu.VMEM((TILE, D_MODEL), F32),
        ],
        compiler_params=_params(),
        name="block_seq",
    )(x, mod, *weights)


def _toeplitz_bias(rpb):
    n_dr, n_dc = 2 * WIN_H - 1, 2 * WIN_W - 1
    col = np.arange(GRID_W)
    dc = col[None, :] - col[:, None] + (WIN_W - 1)
    win_start = np.clip(col - WIN_W // 2, 0, GRID_W - WIN_W)
    col_ok = (col[None, :] >= win_start[:, None]) & (col[None, :] < win_start[:, None] + WIN_W)
    col_sel = (dc[None] == np.arange(n_dc)[:, None, None]).astype(np.float32)
    col_sel = np.concatenate([col_sel, col_sel], axis=-1)
    valid = np.concatenate([col_ok, col_ok], axis=-1)[None] & (np.arange(n_dr + 1) < n_dr)[:, None, None]
    rows = jnp.pad(rpb, ((0, 0), (0, 1), (0, 0)))
    blocks = jnp.einsum("hdc,cqk->hdqk", rows, col_sel, precision=lax.Precision.HIGHEST)
    return jnp.where(valid[None], blocks * LOG2_E, NEG_INF)


def _pair_heads(t):
    return jnp.concatenate([t[:, 0::2], t[:, 1::2]], axis=-1).astype(BF16)


def kernel(x_prompt, x_sample, cache_k, cache_v, c, c_ctx, norm_mix_g, norm_ffn_g, w_mod, b_mod,
           w_in, q_norm_g, k_norm_g, pool_w, pool_scale, na_rpb, w_pool_proj, w_na_proj, w_o,
           w_up, ffn_conv_w, ffn_conv_b, w_down):
    depth = w_in.shape[0]
    assert depth == 1
    batch, seq, _ = x_prompt.shape
    dec_batch, dec_seq, _ = x_sample.shape
    assert seq == TILE and dec_seq % KV_ROWS == 0 and dec_batch + 1 <= MOD_ROWS
    l = 0

    cond = jnp.concatenate(
        [c_ctx[None], c, jnp.zeros((MOD_ROWS - 1 - dec_batch, D_MODEL), F32)], axis=0)
    mod = _modulation(cond, w_mod[l], b_mod[l])

    w_in_b = w_in[l].astype(BF16)
    w_pool_b = pool_w[l].astype(BF16)
    w_pp_b = w_pool_proj[l].astype(BF16)
    w_np_b = w_na_proj[l].astype(BF16)
    w_o_b = w_o[l].astype(BF16)
    w_up_b = w_up[l].astype(BF16)
    w_down_b = w_down[l].astype(BF16)
    conv_w = ffn_conv_w[l]
    g_mix = norm_mix_g[l].reshape(1, D_MODEL)
    g_ffn = norm_ffn_g[l].reshape(1, D_MODEL)
    q_g = jnp.tile(q_norm_g[l], N_HEADS).reshape(1, D_NA)
    k_g = jnp.tile(k_norm_g[l], N_HEADS).reshape(1, D_NA)
    s_pool = pool_scale[l].reshape(1, D_POOL)
    conv_b = ffn_conv_b[l].reshape(1, 2 * D_FF)

    xs = x_sample.reshape(dec_batch * dec_seq, D_MODEL)
    k, v = _kv_proj(xs, mod, g_mix, w_in_b, k_g, seq_len=dec_seq, rows=KV_ROWS)

    xc = x_prompt.reshape(batch * seq, D_MODEL)
    xc, new_k, new_v = _block_seq(xc, mod, g_mix, w_in_b, q_g, k_g,
                                  w_pool_b, s_pool, w_pp_b, w_np_b, w_o_b,
                                  g_ffn, w_up_b, conv_w, conv_b, w_down_b)
    y_prompt = xc.reshape(batch, seq, D_MODEL)

    xs = _mixer_out(xs, mod, k, v, _pair_heads(cache_k[:, l]), _pair_heads(cache_v[:, l]),
                    g_mix, w_in_b, q_g, _toeplitz_bias(na_rpb[l]),
                    w_pool_b, s_pool, w_pp_b, w_np_b, w_o_b, seq_len=dec_seq)
    xs = _conv_ffn(xs, mod, g_ffn, w_up_b, conv_w, conv_b, w_down_b,
                   seq_len=dec_seq, rows=CONV_FFN_ROWS)
    y_sample = xs.reshape(dec_batch, dec_seq, D_MODEL)
    return (y_prompt, y_sample, new_k, new_v)
```

```python
import functools

import jax
import jax.numpy as jnp
import numpy as np
from jax import lax
from jax.experimental import pallas as pl
from jax.experimental.pallas import tpu as pltpu

D_MODEL = 1024
GRID_W = 64
N_HEADS = 8
HEAD_DIM = 64
D_NA = N_HEADS * HEAD_DIM
D_POOL = 512
POOL_WINDOWS = (2, 4, 8, 16)
POOL_GROUP = D_POOL // len(POOL_WINDOWS)
WIN_H = 8
WIN_W = 16
D_FF = 2816
Q_BLOCK = 128
EPS = 1e-6
NEG_INF = -1e30
LOG2_E = 1.4426950408889634
OFF_Q = D_POOL
OFF_K = D_POOL + D_NA
OFF_V = D_POOL + 2 * D_NA
OFF_GP = D_POOL + 3 * D_NA
OFF_GN = OFF_GP + D_MODEL
D_IN = OFF_GN + D_MODEL

TILE = 256
TILE_ROWS = TILE // GRID_W
KEY_ROWS = TILE_ROWS + WIN_H
N_LOCAL = KEY_ROWS * GRID_W
SUBLANES = 8
LANES = 128
POOL_HALO = SUBLANES
CONV_HALO = SUBLANES
FF_CHUNK = 256
N_FF_CHUNKS = D_FF // FF_CHUNK
PIPE_DEPTH = 8
U_SLOTS = PIPE_DEPTH + 1
ATTN_DEPTH = 1
KV_ROWS = 2048
CONV_FFN_ROWS = 512
MOD_ROWS = 8
MOD_TILE = 3072
VMEM_LIMIT = 56 * 1024 * 1024

BF16 = jnp.bfloat16
F32 = jnp.float32


def _dot(a, b):
    return jnp.dot(a, b, preferred_element_type=F32)


def _dot_nt(a, b):
    return lax.dot_general(a, b, (((1,), (1,)), ((), ())), preferred_element_type=F32)


def _modulated_norm(x, g, shift, scale):
    ms = jnp.mean(x * x, axis=-1, keepdims=True)
    return (x * lax.rsqrt(ms + EPS) * g) * (1.0 + scale) + shift


def _sigmoid(x):
    return 1.0 / (1.0 + jnp.exp(-x))


def _mod_row(mod_ref, steps_per_cond):
    if steps_per_cond is None:
        return mod_ref[0:1, :]
    return mod_ref[pl.ds(1 + pl.program_id(0) // steps_per_cond, 1), :]


def _params(**kwargs):
    return pltpu.CompilerParams(
        dimension_semantics=("arbitrary",), vmem_limit_bytes=VMEM_LIMIT, **kwargs)


def _resident(shape):
    return pl.BlockSpec(shape, lambda i: (0,) * len(shape), pipeline_mode=pl.Buffered(1))


def _mod_kernel(cond_ref, w_ref, b_ref, o_ref):
    cnd = cond_ref[...]
    s = (cnd * _sigmoid(cnd)).astype(BF16)
    o_ref[...] = _dot(s, w_ref[...].astype(BF16)) + b_ref[...]


def _modulation(cond, w_mod, b_mod):
    n = w_mod.shape[1]
    return pl.pallas_call(
        _mod_kernel,
        grid=(n // MOD_TILE,),
        in_specs=[
            pl.BlockSpec((MOD_ROWS, D_MODEL), lambda j: (0, 0)),
            pl.BlockSpec((D_MODEL, MOD_TILE), lambda j: (0, j)),
            pl.BlockSpec((1, MOD_TILE), lambda j: (0, j)),
        ],
        out_specs=pl.BlockSpec((MOD_ROWS, MOD_TILE), lambda j: (0, j)),
        out_shape=jax.ShapeDtypeStruct((MOD_ROWS, n), F32),
        compiler_params=_params(),
        name="modulation",
    )(cond, w_mod, b_mod.reshape(1, n))


def _head_norm(t, g):
    assert 2 * HEAD_DIM == LANES
    first_head = lax.broadcasted_iota(jnp.int32, (1, LANES), 1) < HEAD_DIM
    tt = t * t
    blocks = []
    for j in range(D_NA // LANES):
        blk = tt[:, j * LANES:(j + 1) * LANES]
        both = jnp.sum(blk, axis=-1, keepdims=True)
        one = jnp.sum(jnp.where(first_head, blk, 0.0), axis=-1, keepdims=True)
        blocks.append(jnp.where(first_head, one, both - one))
    ss = jnp.concatenate(blocks, axis=-1)
    return t * lax.rsqrt(ss * (1.0 / HEAD_DIM) + EPS) * g


def _scaled_query(q, g):
    return (_head_norm(q, g) * (HEAD_DIM ** -0.5 * LOG2_E)).astype(BF16)


def _mixer_norm(x_ref, mod_ref, g_ref, steps_per_cond):
    mod = _mod_row(mod_ref, steps_per_cond)
    h = _modulated_norm(x_ref[...], g_ref[...], mod[:, 0:D_MODEL], mod[:, D_MODEL:2 * D_MODEL])
    return h.astype(BF16)


def _mixer_in_kernel(x_ref, mod_ref, g_ref, w_ref, qg_ref, kg_ref,
                     p_ref, q_ref, k_ref, v_ref, gp_ref, gn_ref, nk_ref, nv_ref):
    h = _mixer_norm(x_ref, mod_ref, g_ref, None)

    def proj(lo, hi):
        return _dot(h, w_ref[:, lo:hi])

    q = proj(OFF_Q, OFF_K)
    k = proj(OFF_K, OFF_V)
    p_ref[...] = proj(0, OFF_Q)
    v = proj(OFF_V, OFF_GP)
    v_ref[...] = v.astype(BF16)
    q = _scaled_query(q, qg_ref[...])
    gp_ref[...] = proj(OFF_GP, OFF_GN)
    k = _head_norm(k, kg_ref[...])
    k_ref[...] = k.astype(BF16)
    gn_ref[...] = proj(OFF_GN, D_IN)
    t = lax.broadcasted_iota(jnp.int32, (TILE, TILE), 0)
    j = lax.broadcasted_iota(jnp.int32, (TILE, TILE), 1)
    n_blocks = TILE // Q_BLOCK
    perm = jnp.where(j == (t % n_blocks) * Q_BLOCK + t // n_blocks, 1.0, 0.0).astype(BF16)
    q_ref[...] = _dot(perm, q).astype(BF16)
    for hd in range(N_HEADS):
        sl = slice(hd * HEAD_DIM, (hd + 1) * HEAD_DIM)
        nk_ref[0, 0, hd] = k[:, sl]
        nv_ref[0, 0, hd] = v[:, sl]


def _kv_kernel(x_ref, mod_ref, g_ref, wkv_ref, kg_ref, k_ref, v_ref, *, steps_per_cond):
    h = _mixer_norm(x_ref, mod_ref, g_ref, steps_per_cond)
    k_ref[...] = _head_norm(_dot(h, wkv_ref[:, 0:D_NA]), kg_ref[...]).astype(BF16)
    v_ref[...] = _dot(h, wkv_ref[:, D_NA:2 * D_NA]).astype(BF16)


def _kv_proj(x, mod, norm_g, w_in, k_g, *, seq_len, rows):
    n_tok = x.shape[0]
    tok = lambda i: (i, 0)
    assert OFF_K % (2 * D_NA) == 0
    kv_cols = pl.BlockSpec((D_MODEL, 2 * D_NA), lambda i: (0, OFF_K // (2 * D_NA)),
                           pipeline_mode=pl.Buffered(1))
    return pl.pallas_call(
        functools.partial(_kv_kernel, steps_per_cond=seq_len // rows),
        grid=(n_tok // rows,),
        in_specs=[pl.BlockSpec((rows, D_MODEL), tok), _resident(mod.shape), _resident(norm_g.shape),
                  kv_cols, _resident(k_g.shape)],
        out_specs=[pl.BlockSpec((rows, D_NA), tok)] * 2,
        out_shape=[jax.ShapeDtypeStruct((n_tok, D_NA), BF16)] * 2,
        compiler_params=_params(),
        name="kv_proj",
    )(x, mod, norm_g, w_in, k_g)


_POOL_SLABS = pltpu.VMEM((len(POOL_WINDOWS), TILE + 2 * POOL_HALO, LANES), F32)


def _key_row0(tile, n_rows):
    return jnp.clip(tile * TILE_ROWS - WIN_H // 2, 0, n_rows - KEY_ROWS)


def _fill_local_bias(bias_ref, toep_ref, tile, n_rows):
    key_row0 = int(np.clip(tile * TILE_ROWS - WIN_H // 2, 0, n_rows - KEY_ROWS))
    for a in range(TILE_ROWS):
        r = tile * TILE_ROWS + a
        row_start = int(np.clip(r - WIN_H // 2, 0, n_rows - WIN_H))
        for b in range(KEY_ROWS):
            kr = key_row0 + b
            dr = kr - r + (WIN_H - 1) if row_start <= kr < row_start + WIN_H else 2 * WIN_H - 1
            half = slice((b % 2) * GRID_W, (b % 2 + 1) * GRID_W)
            bias_ref[:, a * GRID_W:(a + 1) * GRID_W, b * GRID_W:(b + 1) * GRID_W] = (
                toep_ref[:, dr, :, half])


def _mixer_out_kernel(*refs, latent, tiles_per_seq):
    if latent:
        (x_ref, xprev_ref, xnext_ref, mod_ref, k_ref, v_ref, ck_ref, cv_ref,
         gmix_ref, win_ref, qg_ref, toep_ref, wpool_ref, spool_ref, wpp_ref, wnp_ref, wo_ref,
         o_ref, na_ref, pslab_ref, p_ref, q_ref, gp_ref, gn_ref, bias_ref) = refs
    else:
        (x_ref, mod_ref, p_ref, q_ref, k_ref, v_ref, gp_ref, gn_ref,
         wpool_ref, spool_ref, wpp_ref, wnp_ref, wo_ref, o_ref, na_ref, pslab_ref) = refs
    i = pl.program_id(0)
    t_in_seq = i % tiles_per_seq
    seq_len = TILE * tiles_per_seq

    if latent:
        n_rows = seq_len // GRID_W
        key0 = pl.multiple_of(_key_row0(t_in_seq, n_rows) * GRID_W, TILE)
        for tile in (0, 1, tiles_per_seq - 1):
            pl.when(t_in_seq == tile)(
                functools.partial(_fill_local_bias, bias_ref, toep_ref, tile, n_rows))

        mod = _mod_row(mod_ref, tiles_per_seq)
        x_ext = jnp.concatenate([x_ref[...], xprev_ref[...], xnext_ref[...]], axis=0)
        h_ext = _modulated_norm(x_ext, gmix_ref[...], mod[:, 0:D_MODEL],
                                mod[:, D_MODEL:2 * D_MODEL]).astype(BF16)
        h = h_ext[:TILE]
        q_ref[...] = _scaled_query(_dot(h, win_ref[:, OFF_Q:OFF_K]), qg_ref[...])
        p_ext = _dot(h_ext, win_ref[:, 0:OFF_Q])
        p_ref[...] = p_ext[:TILE]
        p_prev, p_next = p_ext[TILE:TILE + POOL_HALO], p_ext[TILE + POOL_HALO:]
        gp_ref[...] = _dot(h, win_ref[:, OFF_GP:OFF_GN])
        gn_ref[...] = _dot(h, win_ref[:, OFF_GN:D_IN])

    assert 2 * HEAD_DIM == LANES
    lane = lax.broadcasted_iota(jnp.int32, (1, LANES), 1)
    only = [jnp.where(lane < HEAD_DIM, 1.0, 0.0).astype(BF16),
            jnp.where(lane < HEAD_DIM, 0.0, 1.0).astype(BF16)]

    def pair(j):
        return slice(j * LANES, (j + 1) * LANES)

    def scores(j):
        qp = q_ref[:, pair(j)]
        out = []
        for half in range(2):
            qh = qp * only[half]
            if latent:
                s_loc = (_dot_nt(qh, k_ref[0, pl.ds(key0, N_LOCAL), pair(j)])
                         + bias_ref[2 * j + half])
                out.append((s_loc, _dot_nt(qh, ck_ref[0, j])))
            else:
                out.append((_dot_nt(qh, k_ref[0, :, pair(j)]),))
        return out

    def attend(j, halves):
        if latent:
            vals = (v_ref[0, pl.ds(key0, N_LOCAL), pair(j)], cv_ref[0, j])
        else:
            vals = (v_ref[0, :, pair(j)],)
        normed = []
        for half, parts in enumerate(halves):
            m = functools.reduce(jnp.maximum, [jnp.max(s, axis=-1, keepdims=True) for s in parts])
            es = [jnp.exp2(s - m) for s in parts]
            if latent:
                oh = sum(_dot(e.astype(BF16), v * only[half] + only[1 - half])
                         for e, v in zip(es, vals))
                normed.append(oh / pltpu.roll(oh, HEAD_DIM, axis=1))
            else:
                denom = sum(jnp.sum(e, axis=-1, keepdims=True) for e in es)
                oh = sum(_dot(e.astype(BF16), v * only[half]) for e, v in zip(es, vals))
                normed.append(oh / denom)
        na_ref[:, pair(j)] = jnp.where(lane < HEAD_DIM, normed[0], normed[1]).astype(BF16)

    n_pairs = N_HEADS // 2
    pending = {j: scores(j) for j in range(min(ATTN_DEPTH, n_pairs))}

    assert POOL_GROUP == LANES and POOL_HALO >= max(POOL_WINDOWS) // 2
    pos = t_in_seq * TILE + lax.broadcasted_iota(jnp.int32, (TILE, 1), 0)
    mixed = []
    for gi, w in enumerate(POOL_WINDOWS):
        gs = slice(gi * POOL_GROUP, (gi + 1) * POOL_GROUP)
        if latent:
            pslab_ref[gi, 0:POOL_HALO] = jnp.where(t_in_seq > 0, p_prev[:, gs], 0.0)
            pslab_ref[gi, POOL_HALO + TILE:] = jnp.where(
                t_in_seq < tiles_per_seq - 1, p_next[:, gs], 0.0)
        else:
            zeros = jnp.zeros((POOL_HALO, LANES), F32)
            pslab_ref[gi, 0:POOL_HALO] = zeros
            pslab_ref[gi, POOL_HALO + TILE:] = zeros
        pslab_ref[gi, POOL_HALO:POOL_HALO + TILE] = p_ref[:, gs]
        sums = sum(pslab_ref[gi, POOL_HALO + d:POOL_HALO + d + TILE]
                   for d in range(-(w // 2), w - w // 2))
        lo = jnp.maximum(pos - w // 2, 0)
        hi = jnp.minimum(pos + (w - w // 2), seq_len)
        pooled = sums / (hi - lo).astype(F32) - p_ref[:, gs]
        mixed.append(_dot(pooled.astype(BF16), wpool_ref[gi]))
    pool_out = jnp.concatenate(mixed, axis=-1) * spool_ref[...]
    gated_pool = _sigmoid(gp_ref[...]) * _dot(pool_out.astype(BF16), wpp_ref[...])

    for j in range(n_pairs):
        if j + ATTN_DEPTH < n_pairs:
            pending[j + ATTN_DEPTH] = scores(j + ATTN_DEPTH)
        attend(j, pending.pop(j))

    b = _dot(na_ref[...], wnp_ref[...])
    merged = gated_pool + _sigmoid(gn_ref[...]) * b
    y = _dot(merged.astype(BF16), wo_ref[...])
    gate = _mod_row(mod_ref, tiles_per_seq if latent else None)[:, 2 * D_MODEL:3 * D_MODEL]
    o_ref[...] = x_ref[...] + gate * y


def _halo_specs(n_tok, rows, halo, width):
    per_tile = rows // halo
    last = n_tok // halo - 1
    return [
        pl.BlockSpec((halo, width), lambda i: (jnp.maximum(i * per_tile - 1, 0), 0)),
        pl.BlockSpec((halo, width), lambda i: (jnp.minimum((i + 1) * per_tile, last), 0)),
    ]


def _mixer_out(x, mod, k, v, cache_k, cache_v, g_mix, w_in, q_g, toeplitz,
               w_pool, s_pool, w_pp, w_np, w_o, *, seq_len):
    n_tok = x.shape[0]
    tiles_per_seq = seq_len // TILE
    assert tiles_per_seq >= 3 and seq_len // GRID_W >= KEY_ROWS
    tok = lambda i: (i, 0)
    seq3 = lambda i: (i // tiles_per_seq, 0, 0)
    seq4 = lambda i: (i // tiles_per_seq, 0, 0, 0)
    k = k.reshape(n_tok // seq_len, seq_len, D_NA)
    v = v.reshape(n_tok // seq_len, seq_len, D_NA)
    weights = (g_mix, w_in, q_g, toeplitz, w_pool, s_pool, w_pp, w_np, w_o)
    return pl.pallas_call(
        functools.partial(_mixer_out_kernel, latent=True, tiles_per_seq=tiles_per_seq),
        grid=(n_tok // TILE,),
        in_specs=[pl.BlockSpec((TILE, D_MODEL), tok)] + _halo_specs(n_tok, TILE, POOL_HALO, D_MODEL) + [
            _resident(mod.shape),
            pl.BlockSpec((1, seq_len, D_NA), seq3),
            pl.BlockSpec((1, seq_len, D_NA), seq3),
            pl.BlockSpec((1,) + cache_k.shape[1:], seq4),
            pl.BlockSpec((1,) + cache_v.shape[1:], seq4),
        ] + [_resident(w.shape) for w in weights],
        out_specs=pl.BlockSpec((TILE, D_MODEL), tok),
        out_shape=jax.ShapeDtypeStruct((n_tok, D_MODEL), F32),
        scratch_shapes=[
            pltpu.VMEM((TILE, D_NA), BF16),
            _POOL_SLABS,
            pltpu.VMEM((TILE, D_POOL), F32),
            pltpu.VMEM((TILE, D_NA), BF16),
            pltpu.VMEM((TILE, D_MODEL), F32),
            pltpu.VMEM((TILE, D_MODEL), F32),
            pltpu.VMEM((N_HEADS, TILE, N_LOCAL), F32),
        ],
        compiler_params=_params(),
        name="mixer_out",
    )(x, x, x, mod, k, v, cache_k, cache_v, *weights)


def _ff_slabs(rows):
    return pltpu.VMEM((U_SLOTS, 2, FF_CHUNK // LANES, rows + 2 * CONV_HALO, LANES), F32)


def _conv_ffn_kernel(*refs, halo, tiles_per_seq):
    if halo:
        (x_ref, xprev_ref, xnext_ref, mod_ref, g_ref, wup_ref, cw_ref, cb_ref, wdn_ref,
         o_ref, h_ref, u_ref, acc_ref) = refs
    else:
        (x_ref, mod_ref, g_ref, wup_ref, cw_ref, cb_ref, wdn_ref,
         o_ref, h_ref, u_ref, acc_ref) = refs
    i = pl.program_id(0)
    t_in_seq = i % tiles_per_seq
    rows = x_ref.shape[0]
    mod = _mod_row(mod_ref, tiles_per_seq if halo else None)
    shift = mod[:, 3 * D_MODEL:4 * D_MODEL]
    scale = mod[:, 4 * D_MODEL:5 * D_MODEL]

    if halo:
        x_ext = jnp.concatenate([xprev_ref[...], x_ref[...], xnext_ref[...]], axis=0)
        u_rows = slice(0, rows + 2 * CONV_HALO)
    else:
        x_ext = x_ref[...]
        u_rows = slice(CONV_HALO, CONV_HALO + rows)
        zeros = jnp.zeros(u_ref.shape[:3] + (CONV_HALO, LANES), F32)
        u_ref[:, :, :, 0:CONV_HALO] = zeros
        u_ref[:, :, :, CONV_HALO + rows:] = zeros
    h_ref[...] = _modulated_norm(x_ext, g_ref[...], shift, scale).astype(BF16)

    def cols(c, part):
        lo = part * D_FF + c * FF_CHUNK
        return slice(lo, lo + FF_CHUNK)

    n_slabs = FF_CHUNK // LANES

    def up_proj(c):
        for part in range(2):
            u = _dot(h_ref[...], wup_ref[:, cols(c, part)])
            for k in range(n_slabs):
                u_ref[c % U_SLOTS, part, k, u_rows] = u[:, k * LANES:(k + 1) * LANES]
                if halo:
                    last_prev = CONV_HALO - 1
                    first_next = CONV_HALO + rows
                    u_ref[c % U_SLOTS, part, k, last_prev:last_prev + 1] = jnp.where(
                        t_in_seq == 0, 0.0, u[last_prev:last_prev + 1, k * LANES:(k + 1) * LANES])
                    u_ref[c % U_SLOTS, part, k, first_next:first_next + 1] = jnp.where(
                        t_in_seq == tiles_per_seq - 1, 0.0,
                        u[first_next:first_next + 1, k * LANES:(k + 1) * LANES])

    def conv(c, part):
        slabs = []
        for k in range(n_slabs):
            lo = cols(c, part).start + k * LANES
            sl = slice(lo, lo + LANES)
            u_prev = u_ref[c % U_SLOTS, part, k, CONV_HALO - 1:CONV_HALO - 1 + rows]
            u_here = u_ref[c % U_SLOTS, part, k, CONV_HALO:CONV_HALO + rows]
            u_next = u_ref[c % U_SLOTS, part, k, CONV_HALO + 1:CONV_HALO + 1 + rows]
            slabs.append(u_prev * cw_ref[0:1, sl] + u_here * cw_ref[1:2, sl]
                         + u_next * cw_ref[2:3, sl] + cb_ref[:, sl])
        return jnp.concatenate(slabs, axis=-1)

    def gated_down(c):
        a = conv(c, 0)
        act = (a * _sigmoid(a) * conv(c, 1)).astype(BF16)
        return _dot(act, wdn_ref[c * FF_CHUNK:(c + 1) * FF_CHUNK, :])

    for c in range(min(PIPE_DEPTH, N_FF_CHUNKS)):
        up_proj(c)
    for c in range(N_FF_CHUNKS):
        if c + PIPE_DEPTH < N_FF_CHUNKS:
            up_proj(c + PIPE_DEPTH)
        part = gated_down(c)
        if c == 0:
            acc_ref[...] = part
        elif c + 1 < N_FF_CHUNKS:
            acc_ref[...] += part
        else:
            o_ref[...] = x_ref[...] + mod[:, 5 * D_MODEL:6 * D_MODEL] * (acc_ref[...] + part)


def _conv_ffn(x, mod, norm_g, w_up, conv_w, conv_b, w_down, *, seq_len, rows):
    n_tok = x.shape[0]
    tiles_per_seq = seq_len // rows
    assert tiles_per_seq > 1
    tok = lambda i: (i, 0)
    weights = (norm_g, w_up, conv_w, conv_b, w_down)
    return pl.pallas_call(
        functools.partial(_conv_ffn_kernel, halo=True, tiles_per_seq=tiles_per_seq),
        grid=(n_tok // rows,),
        in_specs=[pl.BlockSpec((rows, D_MODEL), tok)] + _halo_specs(n_tok, rows, CONV_HALO, D_MODEL) + [
            _resident(w.shape) for w in (mod,) + weights],
        out_specs=pl.BlockSpec((rows, D_MODEL), tok),
        out_shape=jax.ShapeDtypeStruct((n_tok, D_MODEL), F32),
        scratch_shapes=[
            pltpu.VMEM((rows + 2 * CONV_HALO, D_MODEL), BF16),
            _ff_slabs(rows),
            pltpu.VMEM((rows, D_MODEL), F32),
        ],
        compiler_params=_params(),
        name="conv_ffn",
    )(x, x, x, mod, *weights)


def _block_seq_kernel(x_ref, mod_ref, gmix_ref, win_ref, qg_ref, kg_ref,
                      wpool_ref, spool_ref, wpp_ref, wnp_ref, wo_ref,
                      gffn_ref, wup_ref, cw_ref, cb_ref, wdn_ref,
                      o_ref, nk_ref, nv_ref,
                      p_ref, q_ref, k_ref, v_ref, gp_ref, gn_ref, na_ref, pslab_ref,
                      x1_ref, h_ref, u_ref, acc_ref):
    _mixer_in_kernel(x_ref, mod_ref, gmix_ref, win_ref, qg_ref, kg_ref,
                     p_ref, q_ref, k_ref.at[0], v_ref.at[0], gp_ref, gn_ref, nk_ref, nv_ref)
    _mixer_out_kernel(x_ref, mod_ref, p_ref, q_ref, k_ref, v_ref, gp_ref, gn_ref,
                      wpool_ref, spool_ref, wpp_ref, wnp_ref, wo_ref, x1_ref, na_ref, pslab_ref,
                      latent=False, tiles_per_seq=1)
    _conv_ffn_kernel(x1_ref, mod_ref, gffn_ref, wup_ref, cw_ref, cb_ref, wdn_ref,
                     o_ref, h_ref, u_ref, acc_ref, halo=False, tiles_per_seq=1)


def _block_seq(x, mod, g_mix, w_in, q_g, k_g, w_pool, s_pool, w_pp, w_np, w_o,
               g_ffn, w_up, conv_w, conv_b, w_down):
    n_tok = x.shape[0]
    n_tiles = n_tok // TILE
    tok = lambda i: (i, 0)
    kv_shape = jax.ShapeDtypeStruct((n_tiles, 1, N_HEADS, TILE, HEAD_DIM), F32)
    kv_spec = pl.BlockSpec((1, 1, N_HEADS, TILE, HEAD_DIM), lambda i: (i, 0, 0, 0, 0))
    weights = (g_mix, w_in, q_g, k_g, w_pool, s_pool, w_pp, w_np, w_o,
               g_ffn, w_up, conv_w, conv_b, w_down)
    return pl.pallas_call(
        _block_seq_kernel,
        grid=(n_tiles,),
        in_specs=[
            pl.BlockSpec((TILE, D_MODEL), tok),
        ] + [_resident(w.shape) for w in (mod,) + weights],
        out_specs=[pl.BlockSpec((TILE, D_MODEL), tok), kv_spec, kv_spec],
        out_shape=[jax.ShapeDtypeStruct((n_tok, D_MODEL), F32), kv_shape, kv_shape],
        scratch_shapes=[
            pltpu.VMEM((TILE, D_POOL), F32),
            pltpu.VMEM((TILE, D_NA), BF16),
            pltpu.VMEM((1, TILE, D_NA), BF16),
            pltpu.VMEM((1, TILE, D_NA), BF16),
            pltpu.VMEM((TILE, D_MODEL), F32),
            pltpu.VMEM((TILE, D_MODEL), F32),
            pltpu.VMEM((TILE, D_NA), BF16),
            _POOL_SLABS,
            pltpu.VMEM((TILE, D_MODEL), F32),
            pltpu.VMEM((TILE, D_MODEL), BF16),
            _ff_slabs(TILE),
            pltpu.VMEM((TILE, D_MODEL), F32),
        ],
        compiler_params=_params(),
        name="block_seq",
    )(x, mod, *weights)


def _toeplitz_bias(rpb):
    n_dr, n_dc = 2 * WIN_H - 1, 2 * WIN_W - 1
    col = np.arange(GRID_W)
    dc = col[None, :] - col[:, None] + (WIN_W - 1)
    win_start = np.clip(col - WIN_W // 2, 0, GRID_W - WIN_W)
    col_ok = (col[None, :] >= win_start[:, None]) & (col[None, :] < win_start[:, None] + WIN_W)
    col_sel = (dc[None] == np.arange(n_dc)[:, None, None]).astype(np.float32)
    col_sel = np.concatenate([col_sel, col_sel], axis=-1)
    valid = np.concatenate([col_ok, col_ok], axis=-1)[None] & (np.arange(n_dr + 1) < n_dr)[:, None, None]
    rows = jnp.pad(rpb, ((0, 0), (0, 1), (0, 0)))
    blocks = jnp.einsum("hdc,cqk->hdqk", rows, col_sel, precision=lax.Precision.HIGHEST)
    return jnp.where(valid[None], blocks * LOG2_E, NEG_INF)


def _pair_heads(t):
    return jnp.concatenate([t[:, 0::2], t[:, 1::2]], axis=-1).astype(BF16)


def kernel(x_prompt, x_sample, cache_k, cache_v, c, c_ctx, norm_mix_g, norm_ffn_g, w_mod, b_mod,
           w_in, q_norm_g, k_norm_g, pool_w, pool_scale, na_rpb, w_pool_proj, w_na_proj, w_o,
           w_up, ffn_conv_w, ffn_conv_b, w_down):
    depth = w_in.shape[0]
    assert depth == 1
    batch, seq, _ = x_prompt.shape
    dec_batch, dec_seq, _ = x_sample.shape
    assert seq == TILE and dec_seq % KV_ROWS == 0 and dec_batch + 1 <= MOD_ROWS
    l = 0

    cond = jnp.concatenate(
        [c_ctx[None], c, jnp.zeros((MOD_ROWS - 1 - dec_batch, D_MODEL), F32)], axis=0)
    mod = _modulation(cond, w_mod[l], b_mod[l])

    w_in_b = w_in[l].astype(BF16)
    w_pool_b = pool_w[l].astype(BF16)
    w_pp_b = w_pool_proj[l].astype(BF16)
    w_np_b = w_na_proj[l].astype(BF16)
    w_o_b = w_o[l].astype(BF16)
    w_up_b = w_up[l].astype(BF16)
    w_down_b = w_down[l].astype(BF16)
    conv_w = ffn_conv_w[l]
    g_mix = norm_mix_g[l].reshape(1, D_MODEL)
    g_ffn = norm_ffn_g[l].reshape(1, D_MODEL)
    q_g = jnp.tile(q_norm_g[l], N_HEADS).reshape(1, D_NA)
    k_g = jnp.tile(k_norm_g[l], N_HEADS).reshape(1, D_NA)
    s_pool = pool_scale[l].reshape(1, D_POOL)
    conv_b = ffn_conv_b[l].reshape(1, 2 * D_FF)

    xc = x_prompt.reshape(batch * seq, D_MODEL)
    xc, new_k, new_v = _block_seq(xc, mod, g_mix, w_in_b, q_g, k_g,
                                  w_pool_b, s_pool, w_pp_b, w_np_b, w_o_b,
                                  g_ffn, w_up_b, conv_w, conv_b, w_down_b)
    y_prompt = xc.reshape(batch, seq, D_MODEL)

    xs = x_sample.reshape(dec_batch * dec_seq, D_MODEL)
    k, v = _kv_proj(xs, mod, g_mix, w_in_b, k_g, seq_len=dec_seq, rows=KV_ROWS)
    xs = _mixer_out(xs, mod, k, v, _pair_heads(cache_k[:, l]), _pair_heads(cache_v[:, l]),
                    g_mix, w_in_b, q_g, _toeplitz_bias(na_rpb[l]),
                    w_pool_b, s_pool, w_pp_b, w_np_b, w_o_b, seq_len=dec_seq)
    xs = _conv_ffn(xs, mod, g_ffn, w_up_b, conv_w, conv_b, w_down_b,
                   seq_len=dec_seq, rows=CONV_FFN_ROWS)
    y_sample = xs.reshape(dec_batch, dec_seq, D_MODEL)
    return (y_prompt, y_sample, new_k, new_v)
```

```python
import functools

import jax
import jax.numpy as jnp
import numpy as np
from jax import lax
from jax.experimental import pallas as pl
from jax.experimental.pallas import tpu as pltpu

D_MODEL = 1024
GRID_W = 64
N_HEADS = 8
HEAD_DIM = 64
D_NA = N_HEADS * HEAD_DIM
D_POOL = 512
POOL_WINDOWS = (2, 4, 8, 16)
POOL_GROUP = D_POOL // len(POOL_WINDOWS)
WIN_H = 8
WIN_W = 16
D_FF = 2816
Q_BLOCK = 128
EPS = 1e-6
NEG_INF = -1e30
LOG2_E = 1.4426950408889634
OFF_Q = D_POOL
OFF_K = D_POOL + D_NA
OFF_V = D_POOL + 2 * D_NA
OFF_GP = D_POOL + 3 * D_NA
OFF_GN = OFF_GP + D_MODEL
D_IN = OFF_GN + D_MODEL

TILE = 256
TILE_ROWS = TILE // GRID_W
KEY_ROWS = TILE_ROWS + WIN_H
N_LOCAL = KEY_ROWS * GRID_W
SUBLANES = 8
LANES = 128
POOL_HALO = SUBLANES
CONV_HALO = SUBLANES
FF_CHUNK = 256
N_FF_CHUNKS = D_FF // FF_CHUNK
PIPE_DEPTH = 8
U_SLOTS = PIPE_DEPTH + 1
ATTN_DEPTH = 1
KV_ROWS = 1024
CONV_FFN_ROWS = 512
MOD_ROWS = 8
MOD_TILE = 1536
VMEM_LIMIT = 56 * 1024 * 1024

BF16 = jnp.bfloat16
F32 = jnp.float32


def _dot(a, b):
    return jnp.dot(a, b, preferred_element_type=F32)


def _dot_nt(a, b):
    return lax.dot_general(a, b, (((1,), (1,)), ((), ())), preferred_element_type=F32)


def _modulated_norm(x, g, shift, scale):
    ms = jnp.mean(x * x, axis=-1, keepdims=True)
    return (x * lax.rsqrt(ms + EPS) * g) * (1.0 + scale) + shift


def _sigmoid(x):
    return 1.0 / (1.0 + jnp.exp(-x))


def _mod_row(mod_ref, steps_per_cond):
    if steps_per_cond is None:
        return mod_ref[0:1, :]
    return mod_ref[pl.ds(1 + pl.program_id(0) // steps_per_cond, 1), :]


def _params(**kwargs):
    return pltpu.CompilerParams(
        dimension_semantics=("arbitrary",), vmem_limit_bytes=VMEM_LIMIT, **kwargs)


def _resident(shape):
    return pl.BlockSpec(shape, lambda i: (0,) * len(shape), pipeline_mode=pl.Buffered(1))


def _mod_kernel(cond_ref, w_ref, b_ref, o_ref):
    cnd = cond_ref[...]
    s = (cnd * _sigmoid(cnd)).astype(BF16)
    o_ref[...] = _dot(s, w_ref[...].astype(BF16)) + b_ref[...]


def _modulation(cond, w_mod, b_mod):
    n = w_mod.shape[1]
    return pl.pallas_call(
        _mod_kernel,
        grid=(n // MOD_TILE,),
        in_specs=[
            pl.BlockSpec((MOD_ROWS, D_MODEL), lambda j: (0, 0)),
            pl.BlockSpec((D_MODEL, MOD_TILE), lambda j: (0, j)),
            pl.BlockSpec((1, MOD_TILE), lambda j: (0, j)),
        ],
        out_specs=pl.BlockSpec((MOD_ROWS, MOD_TILE), lambda j: (0, j)),
        out_shape=jax.ShapeDtypeStruct((MOD_ROWS, n), F32),
        compiler_params=_params(),
        name="modulation",
    )(cond, w_mod, b_mod.reshape(1, n))


def _head_norm(t, g):
    assert 2 * HEAD_DIM == LANES
    first_head = lax.broadcasted_iota(jnp.int32, (1, LANES), 1) < HEAD_DIM
    tt = t * t
    blocks = []
    for j in range(D_NA // LANES):
        blk = tt[:, j * LANES:(j + 1) * LANES]
        both = jnp.sum(blk, axis=-1, keepdims=True)
        one = jnp.sum(jnp.where(first_head, blk, 0.0), axis=-1, keepdims=True)
        blocks.append(jnp.where(first_head, one, both - one))
    ss = jnp.concatenate(blocks, axis=-1)
    return t * lax.rsqrt(ss * (1.0 / HEAD_DIM) + EPS) * g


def _scaled_query(q, g):
    return (_head_norm(q, g) * (HEAD_DIM ** -0.5 * LOG2_E)).astype(BF16)


def _mixer_norm(x_ref, mod_ref, g_ref, steps_per_cond):
    mod = _mod_row(mod_ref, steps_per_cond)
    h = _modulated_norm(x_ref[...], g_ref[...], mod[:, 0:D_MODEL], mod[:, D_MODEL:2 * D_MODEL])
    return h.astype(BF16)


def _mixer_in_kernel(x_ref, mod_ref, g_ref, w_ref, qg_ref, kg_ref,
                     p_ref, q_ref, k_ref, v_ref, gp_ref, gn_ref, nk_ref, nv_ref):
    h = _mixer_norm(x_ref, mod_ref, g_ref, None)

    def proj(lo, hi):
        return _dot(h, w_ref[:, lo:hi])

    q = proj(OFF_Q, OFF_K)
    k = proj(OFF_K, OFF_V)
    p_ref[...] = proj(0, OFF_Q)
    v = proj(OFF_V, OFF_GP)
    v_ref[...] = v.astype(BF16)
    q = _scaled_query(q, qg_ref[...])
    gp_ref[...] = proj(OFF_GP, OFF_GN)
    k = _head_norm(k, kg_ref[...])
    k_ref[...] = k.astype(BF16)
    gn_ref[...] = proj(OFF_GN, D_IN)
    t = lax.broadcasted_iota(jnp.int32, (TILE, TILE), 0)
    j = lax.broadcasted_iota(jnp.int32, (TILE, TILE), 1)
    n_blocks = TILE // Q_BLOCK
    perm = jnp.where(j == (t % n_blocks) * Q_BLOCK + t // n_blocks, 1.0, 0.0).astype(BF16)
    q_ref[...] = _dot(perm, q).astype(BF16)
    for hd in range(N_HEADS):
        sl = slice(hd * HEAD_DIM, (hd + 1) * HEAD_DIM)
        nk_ref[0, 0, hd] = k[:, sl]
        nv_ref[0, 0, hd] = v[:, sl]


def _kv_kernel(x_ref, mod_ref, g_ref, wkv_ref, kg_ref, k_ref, v_ref, *, steps_per_cond):
    h = _mixer_norm(x_ref, mod_ref, g_ref, steps_per_cond)
    k_ref[...] = _head_norm(_dot(h, wkv_ref[:, 0:D_NA]), kg_ref[...]).astype(BF16)
    v_ref[...] = _dot(h, wkv_ref[:, D_NA:2 * D_NA]).astype(BF16)


def _kv_proj(x, mod, norm_g, w_in, k_g, *, seq_len, rows):
    n_tok = x.shape[0]
    tok = lambda i: (i, 0)
    assert OFF_K % (2 * D_NA) == 0
    kv_cols = pl.BlockSpec((D_MODEL, 2 * D_NA), lambda i: (0, OFF_K // (2 * D_NA)),
                           pipeline_mode=pl.Buffered(1))
    return pl.pallas_call(
        functools.partial(_kv_kernel, steps_per_cond=seq_len // rows),
        grid=(n_tok // rows,),
        in_specs=[pl.BlockSpec((rows, D_MODEL), tok), _resident(mod.shape), _resident(norm_g.shape),
                  kv_cols, _resident(k_g.shape)],
        out_specs=[pl.BlockSpec((rows, D_NA), tok)] * 2,
        out_shape=[jax.ShapeDtypeStruct((n_tok, D_NA), BF16)] * 2,
        compiler_params=_params(),
        name="kv_proj",
    )(x, mod, norm_g, w_in, k_g)


_POOL_SLABS = pltpu.VMEM((len(POOL_WINDOWS), TILE + 2 * POOL_HALO, LANES), F32)


def _key_row0(tile, n_rows):
    return jnp.clip(tile * TILE_ROWS - WIN_H // 2, 0, n_rows - KEY_ROWS)


def _fill_local_bias(bias_ref, toep_ref, tile, n_rows):
    key_row0 = int(np.clip(tile * TILE_ROWS - WIN_H // 2, 0, n_rows - KEY_ROWS))
    first_half = lax.broadcasted_iota(jnp.int32, (1, 1, 2 * GRID_W), 2) < GRID_W
    for a in range(TILE_ROWS):
        r = tile * TILE_ROWS + a
        row_start = int(np.clip(r - WIN_H // 2, 0, n_rows - WIN_H))

        def block_index(b):
            kr = key_row0 + b
            return kr - r + (WIN_H - 1) if row_start <= kr < row_start + WIN_H else 2 * WIN_H - 1

        for b in range(0, KEY_ROWS, 2):
            pair = jnp.where(first_half, toep_ref[:, block_index(b)], toep_ref[:, block_index(b + 1)])
            bias_ref[:, a * GRID_W:(a + 1) * GRID_W, b * GRID_W:(b + 2) * GRID_W] = pair


def _mixer_out_kernel(*refs, latent, tiles_per_seq):
    if latent:
        (x_ref, xprev_ref, xnext_ref, mod_ref, k_ref, v_ref, ck_ref, cv_ref,
         gmix_ref, win_ref, qg_ref, toep_ref, wpool_ref, spool_ref, wpp_ref, wnp_ref, wo_ref,
         o_ref, na_ref, pslab_ref, p_ref, q_ref, gp_ref, gn_ref, bias_ref) = refs
    else:
        (x_ref, mod_ref, p_ref, q_ref, k_ref, v_ref, gp_ref, gn_ref,
         wpool_ref, spool_ref, wpp_ref, wnp_ref, wo_ref, o_ref, na_ref, pslab_ref) = refs
    i = pl.program_id(0)
    t_in_seq = i % tiles_per_seq
    seq_len = TILE * tiles_per_seq

    if latent:
        n_rows = seq_len // GRID_W
        key0 = pl.multiple_of(_key_row0(t_in_seq, n_rows) * GRID_W, TILE)
        for tile in (0, 1, tiles_per_seq - 1):
            pl.when(t_in_seq == tile)(
                functools.partial(_fill_local_bias, bias_ref, toep_ref, tile, n_rows))

        mod = _mod_row(mod_ref, tiles_per_seq)
        x_ext = jnp.concatenate([x_ref[...], xprev_ref[...], xnext_ref[...]], axis=0)
        h_ext = _modulated_norm(x_ext, gmix_ref[...], mod[:, 0:D_MODEL],
                                mod[:, D_MODEL:2 * D_MODEL]).astype(BF16)
        h = h_ext[:TILE]
        q_ref[...] = _scaled_query(_dot(h, win_ref[:, OFF_Q:OFF_K]), qg_ref[...])
        p_ext = _dot(h_ext, win_ref[:, 0:OFF_Q])
        p_ref[...] = p_ext[:TILE]
        p_prev, p_next = p_ext[TILE:TILE + POOL_HALO], p_ext[TILE + POOL_HALO:]
        gp_ref[...] = _dot(h, win_ref[:, OFF_GP:OFF_GN])
        gn_ref[...] = _dot(h, win_ref[:, OFF_GN:D_IN])

    assert 2 * HEAD_DIM == LANES
    lane = lax.broadcasted_iota(jnp.int32, (1, LANES), 1)
    only = [jnp.where(lane < HEAD_DIM, 1.0, 0.0).astype(BF16),
            jnp.where(lane < HEAD_DIM, 0.0, 1.0).astype(BF16)]

    def pair(j):
        return slice(j * LANES, (j + 1) * LANES)

    def scores(j):
        qp = q_ref[:, pair(j)]
        out = []
        for half in range(2):
            qh = qp * only[half]
            if latent:
                s_loc = (_dot_nt(qh, k_ref[0, pl.ds(key0, N_LOCAL), pair(j)])
                         + bias_ref[2 * j + half])
                out.append((s_loc, _dot_nt(qh, ck_ref[0, j])))
            else:
                out.append((_dot_nt(qh, k_ref[0, :, pair(j)]),))
        return out

    def attend(j, halves):
        if latent:
            vals = (v_ref[0, pl.ds(key0, N_LOCAL), pair(j)], cv_ref[0, j])
        else:
            vals = (v_ref[0, :, pair(j)],)
        normed = []
        for half, parts in enumerate(halves):
            m = functools.reduce(jnp.maximum, [jnp.max(s, axis=-1, keepdims=True) for s in parts])
            es = [jnp.exp2(s - m) for s in parts]
            if latent:
                oh = sum(_dot(e.astype(BF16), v * only[half] + only[1 - half])
                         for e, v in zip(es, vals))
                normed.append(oh / pltpu.roll(oh, HEAD_DIM, axis=1))
            else:
                denom = sum(jnp.sum(e, axis=-1, keepdims=True) for e in es)
                oh = sum(_dot(e.astype(BF16), v * only[half]) for e, v in zip(es, vals))
                normed.append(oh / denom)
        na_ref[:, pair(j)] = jnp.where(lane < HEAD_DIM, normed[0], normed[1]).astype(BF16)

    n_pairs = N_HEADS // 2
    pending = {j: scores(j) for j in range(min(ATTN_DEPTH, n_pairs))}

    assert POOL_GROUP == LANES and POOL_HALO >= max(POOL_WINDOWS) // 2
    pos = t_in_seq * TILE + lax.broadcasted_iota(jnp.int32, (TILE, 1), 0)
    mixed = []
    for gi, w in enumerate(POOL_WINDOWS):
        gs = slice(gi * POOL_GROUP, (gi + 1) * POOL_GROUP)
        if latent:
            pslab_ref[gi, 0:POOL_HALO] = jnp.where(t_in_seq > 0, p_prev[:, gs], 0.0)
            pslab_ref[gi, POOL_HALO + TILE:] = jnp.where(
                t_in_seq < tiles_per_seq - 1, p_next[:, gs], 0.0)
        else:
            zeros = jnp.zeros((POOL_HALO, LANES), F32)
            pslab_ref[gi, 0:POOL_HALO] = zeros
            pslab_ref[gi, POOL_HALO + TILE:] = zeros
        pslab_ref[gi, POOL_HALO:POOL_HALO + TILE] = p_ref[:, gs]
        sums = sum(pslab_ref[gi, POOL_HALO + d:POOL_HALO + d + TILE]
                   for d in range(-(w // 2), w - w // 2))
        lo = jnp.maximum(pos - w // 2, 0)
        hi = jnp.minimum(pos + (w - w // 2), seq_len)
        pooled = sums / (hi - lo).astype(F32) - p_ref[:, gs]
        mixed.append(_dot(pooled.astype(BF16), wpool_ref[gi]))
    pool_out = jnp.concatenate(mixed, axis=-1) * spool_ref[...]
    gated_pool = _sigmoid(gp_ref[...]) * _dot(pool_out.astype(BF16), wpp_ref[...])

    for j in range(n_pairs):
        if j + ATTN_DEPTH < n_pairs:
            pending[j + ATTN_DEPTH] = scores(j + ATTN_DEPTH)
        attend(j, pending.pop(j))

    b = _dot(na_ref[...], wnp_ref[...])
    merged = gated_pool + _sigmoid(gn_ref[...]) * b
    y = _dot(merged.astype(BF16), wo_ref[...])
    gate = _mod_row(mod_ref, tiles_per_seq if latent else None)[:, 2 * D_MODEL:3 * D_MODEL]
    o_ref[...] = x_ref[...] + gate * y


def _halo_specs(n_tok, rows, halo, width):
    per_tile = rows // halo
    last = n_tok // halo - 1
    return [
        pl.BlockSpec((halo, width), lambda i: (jnp.maximum(i * per_tile - 1, 0), 0)),
        pl.BlockSpec((halo, width), lambda i: (jnp.minimum((i + 1) * per_tile, last), 0)),
    ]


def _mixer_out(x, mod, k, v, cache_k, cache_v, g_mix, w_in, q_g, toeplitz,
               w_pool, s_pool, w_pp, w_np, w_o, *, seq_len):
    n_tok = x.shape[0]
    tiles_per_seq = seq_len // TILE
    assert tiles_per_seq >= 3 and seq_len // GRID_W >= KEY_ROWS
    tok = lambda i: (i, 0)
    seq3 = lambda i: (i // tiles_per_seq, 0, 0)
    seq4 = lambda i: (i // tiles_per_seq, 0, 0, 0)
    k = k.reshape(n_tok // seq_len, seq_len, D_NA)
    v = v.reshape(n_tok // seq_len, seq_len, D_NA)
    weights = (g_mix, w_in, q_g, toeplitz, w_pool, s_pool, w_pp, w_np, w_o)
    return pl.pallas_call(
        functools.partial(_mixer_out_kernel, latent=True, tiles_per_seq=tiles_per_seq),
        grid=(n_tok // TILE,),
        in_specs=[pl.BlockSpec((TILE, D_MODEL), tok)] + _halo_specs(n_tok, TILE, POOL_HALO, D_MODEL) + [
            _resident(mod.shape),
            pl.BlockSpec((1, seq_len, D_NA), seq3),
            pl.BlockSpec((1, seq_len, D_NA), seq3),
            pl.BlockSpec((1,) + cache_k.shape[1:], seq4),
            pl.BlockSpec((1,) + cache_v.shape[1:], seq4),
        ] + [_resident(w.shape) for w in weights],
        out_specs=pl.BlockSpec((TILE, D_MODEL), tok),
        out_shape=jax.ShapeDtypeStruct((n_tok, D_MODEL), F32),
        scratch_shapes=[
            pltpu.VMEM((TILE, D_NA), BF16),
            _POOL_SLABS,
            pltpu.VMEM((TILE, D_POOL), F32),
            pltpu.VMEM((TILE, D_NA), BF16),
            pltpu.VMEM((TILE, D_MODEL), F32),
            pltpu.VMEM((TILE, D_MODEL), F32),
            pltpu.VMEM((N_HEADS, TILE, N_LOCAL), F32),
        ],
        compiler_params=_params(),
        name="mixer_out",
    )(x, x, x, mod, k, v, cache_k, cache_v, *weights)


def _ff_slabs(rows):
    return pltpu.VMEM((U_SLOTS, 2, FF_CHUNK // LANES, rows + 2 * CONV_HALO, LANES), F32)


def _conv_ffn_kernel(*refs, halo, tiles_per_seq):
    if halo:
        (x_ref, xprev_ref, xnext_ref, mod_ref, g_ref, wup_ref, cw_ref, cb_ref, wdn_ref,
         o_ref, h_ref, u_ref, acc_ref) = refs
    else:
        (x_ref, mod_ref, g_ref, wup_ref, cw_ref, cb_ref, wdn_ref,
         o_ref, h_ref, u_ref, acc_ref) = refs
    i = pl.program_id(0)
    t_in_seq = i % tiles_per_seq
    rows = x_ref.shape[0]
    mod = _mod_row(mod_ref, tiles_per_seq if halo else None)
    shift = mod[:, 3 * D_MODEL:4 * D_MODEL]
    scale = mod[:, 4 * D_MODEL:5 * D_MODEL]

    if halo:
        x_ext = jnp.concatenate([xprev_ref[...], x_ref[...], xnext_ref[...]], axis=0)
        u_rows = slice(0, rows + 2 * CONV_HALO)
    else:
        x_ext = x_ref[...]
        u_rows = slice(CONV_HALO, CONV_HALO + rows)
        zeros = jnp.zeros(u_ref.shape[:3] + (CONV_HALO, LANES), F32)
        u_ref[:, :, :, 0:CONV_HALO] = zeros
        u_ref[:, :, :, CONV_HALO + rows:] = zeros
    h_ref[...] = _modulated_norm(x_ext, g_ref[...], shift, scale).astype(BF16)

    def cols(c, part):
        lo = part * D_FF + c * FF_CHUNK
        return slice(lo, lo + FF_CHUNK)

    n_slabs = FF_CHUNK // LANES

    def up_proj(c):
        for part in range(2):
            u = _dot(h_ref[...], wup_ref[:, cols(c, part)])
            for k in range(n_slabs):
                u_ref[c % U_SLOTS, part, k, u_rows] = u[:, k * LANES:(k + 1) * LANES]
                if halo:
                    last_prev = CONV_HALO - 1
                    first_next = CONV_HALO + rows
                    u_ref[c % U_SLOTS, part, k, last_prev:last_prev + 1] = jnp.where(
                        t_in_seq == 0, 0.0, u[last_prev:last_prev + 1, k * LANES:(k + 1) * LANES])
                    u_ref[c % U_SLOTS, part, k, first_next:first_next + 1] = jnp.where(
                        t_in_seq == tiles_per_seq - 1, 0.0,
                        u[first_next:first_next + 1, k * LANES:(k + 1) * LANES])

    def conv(c, part):
        slabs = []
        for k in range(n_slabs):
            lo = cols(c, part).start + k * LANES
            sl = slice(lo, lo + LANES)
            u_prev = u_ref[c % U_SLOTS, part, k, CONV_HALO - 1:CONV_HALO - 1 + rows]
            u_here = u_ref[c % U_SLOTS, part, k, CONV_HALO:CONV_HALO + rows]
            u_next = u_ref[c % U_SLOTS, part, k, CONV_HALO + 1:CONV_HALO + 1 + rows]
            slabs.append(u_prev * cw_ref[0:1, sl] + u_here * cw_ref[1:2, sl]
                         + u_next * cw_ref[2:3, sl] + cb_ref[:, sl])
        return jnp.concatenate(slabs, axis=-1)

    def gated_down(c):
        a = conv(c, 0)
        act = (a * _sigmoid(a) * conv(c, 1)).astype(BF16)
        return _dot(act, wdn_ref[c * FF_CHUNK:(c + 1) * FF_CHUNK, :])

    for c in range(min(PIPE_DEPTH, N_FF_CHUNKS)):
        up_proj(c)
    for c in range(N_FF_CHUNKS):
        if c + PIPE_DEPTH < N_FF_CHUNKS:
            up_proj(c + PIPE_DEPTH)
        part = gated_down(c)
        if c == 0:
            acc_ref[...] = part
        elif c + 1 < N_FF_CHUNKS:
            acc_ref[...] += part
        else:
            o_ref[...] = x_ref[...] + mod[:, 5 * D_MODEL:6 * D_MODEL] * (acc_ref[...] + part)


def _conv_ffn(x, mod, norm_g, w_up, conv_w, conv_b, w_down, *, seq_len, rows):
    n_tok = x.shape[0]
    tiles_per_seq = seq_len // rows
    assert tiles_per_seq > 1
    tok = lambda i: (i, 0)
    weights = (norm_g, w_up, conv_w, conv_b, w_down)
    return pl.pallas_call(
        functools.partial(_conv_ffn_kernel, halo=True, tiles_per_seq=tiles_per_seq),
        grid=(n_tok // rows,),
        in_specs=[pl.BlockSpec((rows, D_MODEL), tok)] + _halo_specs(n_tok, rows, CONV_HALO, D_MODEL) + [
            _resident(w.shape) for w in (mod,) + weights],
        out_specs=pl.BlockSpec((rows, D_MODEL), tok),
        out_shape=jax.ShapeDtypeStruct((n_tok, D_MODEL), F32),
        scratch_shapes=[
            pltpu.VMEM((rows + 2 * CONV_HALO, D_MODEL), BF16),
            _ff_slabs(rows),
            pltpu.VMEM((rows, D_MODEL), F32),
        ],
        compiler_params=_params(),
        name="conv_ffn",
    )(x, x, x, mod, *weights)


def _block_seq_kernel(x_ref, mod_ref, gmix_ref, win_ref, qg_ref, kg_ref,
                      wpool_ref, spool_ref, wpp_ref, wnp_ref, wo_ref,
                      gffn_ref, wup_ref, cw_ref, cb_ref, wdn_ref,
                      o_ref, nk_ref, nv_ref,
                      p_ref, q_ref, k_ref, v_ref, gp_ref, gn_ref, na_ref, pslab_ref,
                      x1_ref, h_ref, u_ref, acc_ref):
    _mixer_in_kernel(x_ref, mod_ref, gmix_ref, win_ref, qg_ref, kg_ref,
                     p_ref, q_ref, k_ref.at[0], v_ref.at[0], gp_ref, gn_ref, nk_ref, nv_ref)
    _mixer_out_kernel(x_ref, mod_ref, p_ref, q_ref, k_ref, v_ref, gp_ref, gn_ref,
                      wpool_ref, spool_ref, wpp_ref, wnp_ref, wo_ref, x1_ref, na_ref, pslab_ref,
                      latent=False, tiles_per_seq=1)
    _conv_ffn_kernel(x1_ref, mod_ref, gffn_ref, wup_ref, cw_ref, cb_ref, wdn_ref,
                     o_ref, h_ref, u_ref, acc_ref, halo=False, tiles_per_seq=1)


def _block_seq(x, mod, g_mix, w_in, q_g, k_g, w_pool, s_pool, w_pp, w_np, w_o,
               g_ffn, w_up, conv_w, conv_b, w_down):
    n_tok = x.shape[0]
    n_tiles = n_tok // TILE
    tok = lambda i: (i, 0)
    kv_shape = jax.ShapeDtypeStruct((n_tiles, 1, N_HEADS, TILE, HEAD_DIM), F32)
    kv_spec = pl.BlockSpec((1, 1, N_HEADS, TILE, HEAD_DIM), lambda i: (i, 0, 0, 0, 0))
    weights = (g_mix, w_in, q_g, k_g, w_pool, s_pool, w_pp, w_np, w_o,
               g_ffn, w_up, conv_w, conv_b, w_down)
    return pl.pallas_call(
        _block_seq_kernel,
        grid=(n_tiles,),
        in_specs=[
            pl.BlockSpec((TILE, D_MODEL), tok),
        ] + [_resident(w.shape) for w in (mod,) + weights],
        out_specs=[pl.BlockSpec((TILE, D_MODEL), tok), kv_spec, kv_spec],
        out_shape=[jax.ShapeDtypeStruct((n_tok, D_MODEL), F32), kv_shape, kv_shape],
        scratch_shapes=[
            pltpu.VMEM((TILE, D_POOL), F32),
            pltpu.VMEM((TILE, D_NA), BF16),
            pltpu.VMEM((1, TILE, D_NA), BF16),
            pltpu.VMEM((1, TILE, D_NA), BF16),
            pltpu.VMEM((TILE, D_MODEL), F32),
            pltpu.VMEM((TILE, D_MODEL), F32),
            pltpu.VMEM((TILE, D_NA), BF16),
            _POOL_SLABS,
            pltpu.VMEM((TILE, D_MODEL), F32),
            pltpu.VMEM((TILE, D_MODEL), BF16),
            _ff_slabs(TILE),
            pltpu.VMEM((TILE, D_MODEL), F32),
        ],
        compiler_params=_params(),
        name="block_seq",
    )(x, mod, *weights)


def _toeplitz_bias(rpb):
    n_dr, n_dc = 2 * WIN_H - 1, 2 * WIN_W - 1
    col = np.arange(GRID_W)
    dc = col[None, :] - col[:, None] + (WIN_W - 1)
    win_start = np.clip(col - WIN_W // 2, 0, GRID_W - WIN_W)
    col_ok = (col[None, :] >= win_start[:, None]) & (col[None, :] < win_start[:, None] + WIN_W)
    col_sel = (dc[None] == np.arange(n_dc)[:, None, None]).astype(np.float32)
    col_sel = np.concatenate([col_sel, col_sel], axis=-1)
    valid = np.concatenate([col_ok, col_ok], axis=-1)[None] & (np.arange(n_dr + 1) < n_dr)[:, None, None]
    rows = jnp.pad(rpb, ((0, 0), (0, 1), (0, 0)))
    blocks = jnp.einsum("hdc,cqk->hdqk", rows, col_sel, precision=lax.Precision.HIGHEST)
    return jnp.where(valid[None], blocks * LOG2_E, NEG_INF)


def _pair_heads(t):
    return jnp.concatenate([t[:, 0::2], t[:, 1::2]], axis=-1).astype(BF16)


def kernel(x_prompt, x_sample, cache_k, cache_v, c, c_ctx, norm_mix_g, norm_ffn_g, w_mod, b_mod,
           w_in, q_norm_g, k_norm_g, pool_w, pool_scale, na_rpb, w_pool_proj, w_na_proj, w_o,
           w_up, ffn_conv_w, ffn_conv_b, w_down):
    depth = w_in.shape[0]
    assert depth == 1
    batch, seq, _ = x_prompt.shape
    dec_batch, dec_seq, _ = x_sample.shape
    assert seq == TILE and dec_seq % KV_ROWS == 0 and dec_batch + 1 <= MOD_ROWS
    l = 0

    cond = jnp.concatenate(
        [c_ctx[None], c, jnp.zeros((MOD_ROWS - 1 - dec_batch, D_MODEL), F32)], axis=0)
    mod = _modulation(cond, w_mod[l], b_mod[l])

    w_in_b = w_in[l].astype(BF16)
    w_pool_b = pool_w[l].astype(BF16)
    w_pp_b = w_pool_proj[l].astype(BF16)
    w_np_b = w_na_proj[l].astype(BF16)
    w_o_b = w_o[l].astype(BF16)
    w_up_b = w_up[l].astype(BF16)
    w_down_b = w_down[l].astype(BF16)
    conv_w = ffn_conv_w[l]
    g_mix = norm_mix_g[l].reshape(1, D_MODEL)
    g_ffn = norm_ffn_g[l].reshape(1, D_MODEL)
    q_g = jnp.tile(q_norm_g[l], N_HEADS).reshape(1, D_NA)
    k_g = jnp.tile(k_norm_g[l], N_HEADS).reshape(1, D_NA)
    s_pool = pool_scale[l].reshape(1, D_POOL)
    conv_b = ffn_conv_b[l].reshape(1, 2 * D_FF)

    xc = x_prompt.reshape(batch * seq, D_MODEL)
    xc, new_k, new_v = _block_seq(xc, mod, g_mix, w_in_b, q_g, k_g,
                                  w_pool_b, s_pool, w_pp_b, w_np_b, w_o_b,
                                  g_ffn, w_up_b, conv_w, conv_b, w_down_b)
    y_prompt = xc.reshape(batch, seq, D_MODEL)

    xs = x_sample.reshape(dec_batch * dec_seq, D_MODEL)
    k, v = _kv_proj(xs, mod, g_mix, w_in_b, k_g, seq_len=dec_seq, rows=KV_ROWS)
    xs = _mixer_out(xs, mod, k, v, _pair_heads(cache_k[:, l]), _pair_heads(cache_v[:, l]),
                    g_mix, w_in_b, q_g, _toeplitz_bias(na_rpb[l]),
                    w_pool_b, s_pool, w_pp_b, w_np_b, w_o_b, seq_len=dec_seq)
    xs = _conv_ffn(xs, mod, g_ffn, w_up_b, conv_w, conv_b, w_down_b,
                   seq_len=dec_seq, rows=CONV_FFN_ROWS)
    y_sample = xs.reshape(dec_batch, dec_seq, D_MODEL)
    return (y_prompt, y_sample, new_k, new_v)
```
